```python
import math
import functools
import numpy as np
import jax
import jax.numpy as jnp
from jax import lax

D_MODEL = 1024
BATCH = 4
SEQ = 4096
DEPTH = 2
DEC_BATCH = 32
DEC_SEQ = 1
PAST_LEN = 16384
PAGE_SIZE = 128

M_HEADS = 4
M_HEAD_DIM = 128
M_WIDTH = M_HEADS * M_HEAD_DIM
M_CONV = 4
M_CHUNK = 64
A_HEADS = 8
A_KV_HEADS = 2
A_HEAD_DIM = 64
A_WIDTH = A_HEADS * A_HEAD_DIM
A_GROUP = A_HEADS // A_KV_HEADS
A_SCALE = A_HEAD_DIM ** -0.5
CMP_STRIDE = 16
CMP_BLOCK = 2 * CMP_STRIDE
CMP_HIDDEN = A_HEAD_DIM
SEL_BLOCK = 64
N_SEL = 16
WINDOW = 512
Q_BLOCK = 128
R_HEADS = 8
R_HEAD_DIM = 64
R_WIDTH = R_HEADS * R_HEAD_DIM
R_DECAY_RANK = 64
R_A_RANK = 64
R_GATE_RANK = 128
R_COLS = 3 * R_WIDTH + R_DECAY_RANK + R_A_RANK + R_GATE_RANK
LN_X_EPS = 64e-5
D_FF = ((-(-8 * D_MODEL // 3)) + 255) // 256 * 256
IN_SPLITS = (2 * M_WIDTH, M_WIDTH, M_WIDTH, 2 * M_HEADS,
             A_WIDTH, 6 * A_KV_HEADS * A_HEAD_DIM, 3 * A_HEADS,
             R_COLS, 3 * D_MODEL)
IN_COLS = sum(IN_SPLITS)
RMS_EPS = 1e-6
NEG = -1e30
FORCE = 1e9

kernel_name = 'hybrid_mlstm_nsa_rwkv7_step'


def split_last(z, sizes):
    return jnp.split(z, np.cumsum(sizes)[:-1].tolist(), axis=-1)


def rmsnorm(x, g):
    xf = x.astype(jnp.float32)
    y = xf * lax.rsqrt(jnp.mean(xf * xf, axis=-1, keepdims=True) + RMS_EPS)
    return (y * g.astype(jnp.float32)).astype(x.dtype)


def mlstm_chunk(carry, xs):
    C, n, m = carry
    q, k, v, li, lf = xs
    L = q.shape[1]
    b = jnp.cumsum(lf, axis=1)
    dmat = b[:, :, None, :] - b[:, None, :, :] + li[:, None, :, :]
    causal = jnp.tril(jnp.ones((L, L), bool))
    dmat = jnp.where(causal[None, :, :, None], dmat, -jnp.inf)
    inter = b + m[:, None, :]
    m_t = jnp.maximum(inter, jnp.max(dmat, axis=2))
    w_intra = jnp.exp(dmat - m_t[:, :, None, :])
    w_inter = jnp.exp(inter - m_t)
    att = w_intra * jnp.einsum('bthd,bshd->btsh', q, k)
    num = (jnp.einsum('btsh,bshv->bthv', att, v)
           + w_inter[..., None] * jnp.einsum('bthd,bhdv->bthv', q, C))
    den = jnp.sum(att, axis=2) + w_inter * jnp.einsum('bthd,bhd->bth', q, n)
    h = num / jnp.maximum(jnp.abs(den), jnp.exp(-m_t))[..., None]
    m_new = m_t[:, -1]
    carry_scale = jnp.exp(b[:, -1] + m - m_new)
    w_end = jnp.exp(b[:, -1:, :] - b + li - m_new[:, None, :])
    C_new = carry_scale[..., None, None] * C + jnp.einsum('bsh,bshd,bshv->bhdv', w_end, k, v)
    n_new = carry_scale[..., None] * n + jnp.einsum('bsh,bshd->bhd', w_end, k)
    return (C_new, n_new, m_new), h


def mlstm_mix(m_qk, m_v, m_o, m_if, m_state, conv_w, conv_b, if_bias):
    conv_buf, C, n, m = m_state
    B, T, _ = m_qk.shape
    f32 = jnp.float32
    full = jnp.concatenate([conv_buf.astype(m_qk.dtype), m_qk], axis=1)
    qk = jax.nn.silu(conv_b + sum(full[:, j:j + T] * conv_w[j] for j in range(M_CONV)))
    q, k = jnp.split(qk.astype(f32), 2, axis=-1)
    q = q.reshape(B, T, M_HEADS, M_HEAD_DIM)
    k = k.reshape(B, T, M_HEADS, M_HEAD_DIM) * (M_HEAD_DIM ** -0.5)
    v = m_v.astype(f32).reshape(B, T, M_HEADS, M_HEAD_DIM)
    gates = m_if.astype(f32).reshape(B, T, 2, M_HEADS) + if_bias.astype(f32)
    li = gates[:, :, 0]
    lf = jax.nn.log_sigmoid(gates[:, :, 1])
    L = math.gcd(T, M_CHUNK)
    nc = T // L

    def chunks(a):
        return jnp.moveaxis(a.reshape(B, nc, L, *a.shape[2:]), 1, 0)

    carry0 = (C.astype(f32), n.astype(f32), m.astype(f32))
    (C, n, m), h = lax.scan(mlstm_chunk, carry0,
                            (chunks(q), chunks(k), chunks(v), chunks(li), chunks(lf)))
    h = jnp.moveaxis(h, 0, 1).reshape(B, T, M_WIDTH)
    y = jax.nn.sigmoid(m_o.astype(f32)) * h
    return y.astype(m_qk.dtype), (full[:, -(M_CONV - 1):], C, n, m)


def compress(rows, pe, w1, w2):
    B, T, G, d = rows.shape
    nchunk = T // CMP_STRIDE
    c = rows[:, :nchunk * CMP_STRIDE].reshape(B, nchunk, CMP_STRIDE, G, d)
    w1 = w1.reshape(2, CMP_STRIDE, d, CMP_HIDDEN)
    halves = jnp.einsum('bnlgd,jldh->jbngh', c, w1)
    pe_term = jnp.einsum('jld,jldh->h', pe.reshape(2, CMP_STRIDE, d), w1)
    hid = jax.nn.silu(halves[0, :, :-1] + halves[1, :, 1:] + pe_term)
    return jnp.einsum('bngh,hd->bngd', hid, w2)


def cmp_attend(q, kc, vc, q_pos):
    B, Q = q.shape[:2]
    N = kc.shape[1]
    qg = q.reshape(B, Q, A_KV_HEADS, A_GROUP, A_HEAD_DIM)
    s = jnp.einsum('bqgjd,bngd->bgjqn', qg, kc).astype(jnp.float32) * A_SCALE
    blk_end = jnp.arange(N) * CMP_STRIDE + CMP_BLOCK - 1
    mask = blk_end[None, :] <= q_pos[:, None]
    p = jax.nn.softmax(jnp.where(mask, s, NEG), axis=-1)
    p = jnp.where(mask, p, 0.0)
    o = jnp.einsum('bgjqn,bngd->bqgjd', p.astype(vc.dtype), vc)
    return o.reshape(B, Q, A_HEADS, A_HEAD_DIM), jnp.sum(p, axis=2)


def select_blocks(p_grp, q_pos, n_total):
    N = p_grp.shape[-1]
    n_blocks = -(-n_total // SEL_BLOCK)
    i = np.arange(N)[:, None]
    j = np.arange(n_blocks)[None, :]
    covers = (i * CMP_STRIDE < (j + 1) * SEL_BLOCK) & (i * CMP_STRIDE + CMP_BLOCK > j * SEL_BLOCK)
    score = jnp.einsum('bgqn,nj->bgqj', p_grp, jnp.asarray(covers, jnp.float32))
    cur = q_pos[:, None] // SEL_BLOCK
    jj = jnp.arange(n_blocks)[None, :]
    forced = (jj == 0) | (jj == cur) | (jj == cur - 1)
    rank = jnp.where(jj <= cur, jnp.where(forced, FORCE, score), NEG)
    _, idx = lax.top_k(rank, min(N_SEL, n_blocks))
    return idx


def block_tokens(idx):
    tok = idx[..., None] * SEL_BLOCK + jnp.arange(SEL_BLOCK, dtype=idx.dtype)
    return tok.reshape(*idx.shape[:-1], idx.shape[-1] * SEL_BLOCK)


def gathered_attend(q, kg, vg, tok, q_pos):
    B, Q = q.shape[:2]
    qg = q.reshape(B, Q, A_KV_HEADS, A_GROUP, A_HEAD_DIM)
    s = jnp.einsum('bqgjd,bgqsd->bgjqs', qg, kg).astype(jnp.float32) * A_SCALE
    mask = (tok <= q_pos[None, None, :, None])[:, :, None]
    p = jax.nn.softmax(jnp.where(mask, s, NEG), axis=-1)
    o = jnp.einsum('bgjqs,bgqsd->bqgjd', p.astype(vg.dtype), vg)
    return o.reshape(B, Q, A_HEADS, A_HEAD_DIM)


def band_attend(q, k, v, q_pos, k_pos):
    B, Q = q.shape[:2]
    qg = q.reshape(B, Q, A_KV_HEADS, A_GROUP, A_HEAD_DIM)
    s = jnp.einsum('bqgjd,bkgd->bgjqk', qg, k).astype(jnp.float32) * A_SCALE
    rel = q_pos[:, None] - k_pos[None, :]
    mask = (rel >= 0) & (rel < WINDOW) & (k_pos[None, :] >= 0)
    p = jax.nn.softmax(jnp.where(mask, s, NEG), axis=-1)
    o = jnp.einsum('bgjqk,bkgd->bqgjd', p.astype(v.dtype), v)
    return o.reshape(B, Q, A_HEADS, A_HEAD_DIM)


def combine_branches(a_gate, o_cmp, o_sel, o_win):
    B, T = a_gate.shape[:2]
    g = jax.nn.sigmoid(a_gate.astype(jnp.float32)).reshape(B, T, 3, A_HEADS, 1)
    o = g[:, :, 0] * o_cmp + g[:, :, 1] * o_sel + g[:, :, 2] * o_win
    return o.reshape(B, T, A_WIDTH).astype(a_gate.dtype)


def split_nsa(a_q, a_kv):
    B, T, _ = a_q.shape
    q = a_q.reshape(B, T, A_HEADS, A_HEAD_DIM)
    rows = [r.reshape(B, T, A_KV_HEADS, A_HEAD_DIM) for r in jnp.split(a_kv, 6, axis=-1)]
    return q, rows


def nsa_prompt(a_q, a_kv, a_gate, cmp_pe, cmp_w1, cmp_w2):
    B, T, _ = a_q.shape
    q, (kcr, vcr, ksr, vsr, kwr, vwr) = split_nsa(a_q, a_kv)
    q_pos = jnp.arange(T)
    kc = compress(kcr, cmp_pe[0], cmp_w1[0], cmp_w2[0])
    vc = compress(vcr, cmp_pe[1], cmp_w1[1], cmp_w2[1])
    o_cmp, p_grp = cmp_attend(q, kc, vc, q_pos)
    idx = select_blocks(p_grp, q_pos, T)
    nqb = T // Q_BLOCK
    qb = jnp.moveaxis(q.reshape(B, nqb, Q_BLOCK, A_HEADS, A_HEAD_DIM), 1, 0)
    idxb = jnp.moveaxis(idx.reshape(B, A_KV_HEADS, nqb, Q_BLOCK, idx.shape[-1]), 2, 0)
    starts = jnp.arange(nqb) * Q_BLOCK
    ks_t = jnp.swapaxes(ksr, 1, 2)
    vs_t = jnp.swapaxes(vsr, 1, 2)
    bi = jnp.arange(B)[:, None, None]
    gi = jnp.arange(A_KV_HEADS)[None, :, None]

    def sel_block(args):
        qq, ii, st = args
        tok = block_tokens(ii)
        flat = tok.reshape(B, A_KV_HEADS, -1)
        kg = ks_t[bi, gi, flat].reshape(*tok.shape, A_HEAD_DIM)
        vg = vs_t[bi, gi, flat].reshape(*tok.shape, A_HEAD_DIM)
        return gathered_attend(qq, kg, vg, tok, st + jnp.arange(Q_BLOCK))

    o_sel = lax.map(sel_block, (qb, idxb, starts))
    o_sel = jnp.moveaxis(o_sel, 0, 1).reshape(B, T, A_HEADS, A_HEAD_DIM)
    pad = ((0, 0), (WINDOW, 0), (0, 0), (0, 0))
    kw_pad = jnp.pad(kwr, pad)
    vw_pad = jnp.pad(vwr, pad)

    def win_block(args):
        qq, st = args
        kb = lax.dynamic_slice_in_dim(kw_pad, st, WINDOW + Q_BLOCK, axis=1)
        vb = lax.dynamic_slice_in_dim(vw_pad, st, WINDOW + Q_BLOCK, axis=1)
        return band_attend(qq, kb, vb, st + jnp.arange(Q_BLOCK),
                           st - WINDOW + jnp.arange(WINDOW + Q_BLOCK))

    o_win = lax.map(win_block, (qb, starts))
    o_win = jnp.moveaxis(o_win, 0, 1).reshape(B, T, A_HEADS, A_HEAD_DIM)
    out = combine_branches(a_gate, o_cmp, o_sel, o_win)
    keep = min(WINDOW, T)
    return out, (kcr, vcr, ksr, vsr, kwr[:, T - keep:], vwr[:, T - keep:])


def gather_paged(pool, new_rows, page_table, tok):
    B, G = tok.shape[:2]
    n_pages = page_table.shape[1]
    past_len = n_pages * PAGE_SIZE
    b = jnp.arange(B)[:, None, None, None]
    g = jnp.arange(G)[None, :, None, None]
    phys = page_table[b, jnp.minimum(tok // PAGE_SIZE, n_pages - 1)]
    from_pool = pool[phys, tok % PAGE_SIZE, g].astype(new_rows.dtype)
    from_new = new_rows[b, jnp.clip(tok - past_len, 0, new_rows.shape[1] - 1), g]
    return jnp.where((tok < past_len)[..., None], from_pool, from_new)


def nsa_sample(a_q, a_kv, a_gate, pool_kc, pool_vc, pool_ks, pool_vs, win_k, win_v,
               page_table, cmp_pe, cmp_w1, cmp_w2):
    B, Q, _ = a_q.shape
    q, (kcr, vcr, ksr, vsr, kwr, vwr) = split_nsa(a_q, a_kv)
    q_pos = PAST_LEN + jnp.arange(Q)

    def full_rows(pool, new):
        past = pool[page_table].reshape(B, -1, A_KV_HEADS, A_HEAD_DIM)
        return jnp.concatenate([past.astype(new.dtype), new], axis=1)

    kc = compress(full_rows(pool_kc, kcr), cmp_pe[0], cmp_w1[0], cmp_w2[0])
    vc = compress(full_rows(pool_vc, vcr), cmp_pe[1], cmp_w1[1], cmp_w2[1])
    o_cmp, p_grp = cmp_attend(q, kc, vc, q_pos)
    tok = block_tokens(select_blocks(p_grp, q_pos, PAST_LEN + Q))
    kg = gather_paged(pool_ks, ksr, page_table, tok)
    vg = gather_paged(pool_vs, vsr, page_table, tok)
    o_sel = gathered_attend(q, kg, vg, tok, q_pos)
    keep = win_k.shape[1]
    kw = jnp.concatenate([win_k.astype(kwr.dtype), kwr], axis=1)
    vw = jnp.concatenate([win_v.astype(vwr.dtype), vwr], axis=1)
    o_win = band_attend(q, kw, vw, q_pos, PAST_LEN - keep + jnp.arange(keep + Q))
    out = combine_branches(a_gate, o_cmp, o_sel, o_win)
    return out, (kcr, vcr, ksr, vsr, kw[:, -keep:], vw[:, -keep:])


def rwkv_step(S, xs):
    r, w, k, v, a, b = xs
    sa = jnp.einsum('bhij,bhj->bhi', S, a)
    S = S * w[:, :, None, :] + sa[..., None] * b[:, :, None, :] + v[..., None] * k[:, :, None, :]
    return S, jnp.einsum('bhij,bhj->bhi', S, r)


def rwkv_mix(cols, r_state, lp):
    shift_buf, S = r_state
    B, T, _ = cols.shape
    f32 = jnp.float32
    prev = jnp.concatenate([shift_buf.astype(cols.dtype), cols[:, :-1]], axis=1)
    mixed = (cols + (prev - cols) * lp['rwkv_mu']).astype(f32)
    r, k, v, wd, ad, gd = split_last(mixed, (R_WIDTH, R_WIDTH, R_WIDTH, R_DECAY_RANK, R_A_RANK, R_GATE_RANK))
    w = -jax.nn.softplus(-(lp['rwkv_w0'] + jnp.tanh(wd) @ lp['rwkv_w2'])) - 0.5
    decay = jnp.exp(-jnp.exp(w))
    a = jax.nn.sigmoid(lp['rwkv_a0'] + ad @ lp['rwkv_a2'])
    g = jax.nn.sigmoid(gd) @ lp['rwkv_g2']

    def heads(z):
        return z.reshape(B, T, R_HEADS, R_HEAD_DIM)

    def time_major(z):
        return jnp.moveaxis(z, 1, 0)

    kk = heads(k * lp['rwkv_k_k'])
    kk = kk * lax.rsqrt(jnp.maximum(jnp.sum(kk * kk, axis=-1, keepdims=True), 1e-24))
    k = k * (1.0 + (a - 1.0) * lp['rwkv_k_a'])
    r_h, k_h, v_h, a_h = heads(r), heads(k), heads(v), heads(a)
    S, y = lax.scan(rwkv_step, S.astype(f32),
                    (time_major(r_h), time_major(heads(decay)), time_major(k_h), time_major(v_h),
                     time_major(-kk), time_major(kk * a_h)))
    y = jnp.moveaxis(y, 0, 1)
    mean = jnp.mean(y, axis=-1, keepdims=True)
    var = jnp.mean(jnp.square(y - mean), axis=-1, keepdims=True)
    y = ((y - mean) * lax.rsqrt(var + LN_X_EPS)).reshape(B, T, R_WIDTH) * lp['rwkv_ln_g'] + lp['rwkv_ln_b']
    bonus = jnp.sum(r_h * k_h * lp['rwkv_r_k'], axis=-1, keepdims=True) * v_h
    y = (y + bonus.reshape(B, T, R_WIDTH)) * g
    return y.astype(cols.dtype), (cols[:, -1:], S)


def trunk_layer(x, lp, nsa_fn, m_state, r_state):
    B, T, _ = x.shape
    h = rmsnorm(x, lp['norm1_g'])
    m_qk, m_v, m_o, m_if, a_q, a_kv, a_gate, r_cols, merge = split_last(h @ lp['w_in'], IN_SPLITS)
    y_m, m_new = mlstm_mix(m_qk, m_v, m_o, m_if, m_state, lp['mlstm_conv_w'], lp['mlstm_conv_b'], lp['mlstm_if_bias'])
    y_a, nsa_new = nsa_fn(a_q, a_kv, a_gate)
    y_r, r_new = rwkv_mix(r_cols, r_state, lp)
    gates = jax.nn.sigmoid(merge.astype(jnp.float32)).reshape(B, T, 3, D_MODEL)
    merged = (gates[:, :, 0] * (y_m @ lp['w_branch_m'])
              + gates[:, :, 1] * (y_a @ lp['w_branch_a'])
              + gates[:, :, 2] * (y_r @ lp['w_branch_r']))
    x = x + merged.astype(x.dtype) @ lp['w_out']
    h2 = rmsnorm(x, lp['norm2_g'])
    gte, up = jnp.split(h2 @ lp['w_gate_up'], 2, axis=-1)
    x = x + (jax.nn.silu(gte) * up) @ lp['w_down']
    return x, nsa_new, m_new, r_new


def setup_inputs(seed: int = 0) -> dict:
    key = jax.random.key(seed)
    keys = iter(jax.random.split(key, 64))
    f32 = jnp.float32
    n_pages = PAST_LEN // PAGE_SIZE
    n_used = DEC_BATCH * n_pages
    n_pool = n_used + max(1, n_used // 4)
    win_keep = min(WINDOW, PAST_LEN)
    L = DEPTH

    def normal(shape, scale=1.0):
        return scale * jax.random.normal(next(keys), shape, f32)

    def uniform(shape, lo, hi):
        return jax.random.uniform(next(keys), shape, f32, lo, hi)

    pool = (L, n_pool, PAGE_SIZE, A_KV_HEADS, A_HEAD_DIM)
    win = (L, DEC_BATCH, win_keep, A_KV_HEADS, A_HEAD_DIM)
    page_table = jax.random.permutation(next(keys), n_pool)[:n_used].reshape(DEC_BATCH, n_pages).astype(jnp.int32)
    return {
        'x_prompt': normal((BATCH, SEQ, D_MODEL)),
        'x_sample': normal((DEC_BATCH, DEC_SEQ, D_MODEL)),
        'cache_cmp_k': normal(pool),
        'cache_cmp_v': normal(pool),
        'cache_slc_k': normal(pool),
        'cache_slc_v': normal(pool),
        'cache_win_k': normal(win),
        'cache_win_v': normal(win),
        'state_mlstm_conv': normal((L, DEC_BATCH, M_CONV - 1, 2 * M_WIDTH)),
        'state_mlstm_C': normal((L, DEC_BATCH, M_HEADS, M_HEAD_DIM, M_HEAD_DIM), 0.1),
        'state_mlstm_n': normal((L, DEC_BATCH, M_HEADS, M_HEAD_DIM), 0.1),
        'state_mlstm_m': normal((L, DEC_BATCH, M_HEADS)),
        'state_rwkv_shift': normal((L, DEC_BATCH, 1, R_COLS)),
        'state_rwkv': normal((L, DEC_BATCH, R_HEADS, R_HEAD_DIM, R_HEAD_DIM), 0.1),
        'page_table': page_table,
        'norm1_g': 1.0 + normal((L, D_MODEL), 0.02),
        'w_in': normal((L, D_MODEL, IN_COLS), D_MODEL ** -0.5),
        'mlstm_if_bias': jnp.stack([normal((L, M_HEADS), 0.1), uniform((L, M_HEADS), 3.0, 6.0)], axis=1),
        'mlstm_conv_w': normal((L, M_CONV, 2 * M_WIDTH), 0.5),
        'mlstm_conv_b': normal((L, 2 * M_WIDTH), 0.02),
        'nsa_cmp_pe': normal((L, 2, CMP_BLOCK, A_HEAD_DIM), 0.1),
        'nsa_cmp_w1': normal((L, 2, CMP_BLOCK * A_HEAD_DIM, CMP_HIDDEN), (CMP_BLOCK * A_HEAD_DIM) ** -0.5),
        'nsa_cmp_w2': normal((L, 2, CMP_HIDDEN, A_HEAD_DIM), CMP_HIDDEN ** -0.5),
        'rwkv_mu': uniform((L, R_COLS), 0.0, 1.0),
        'rwkv_w0': uniform((L, R_WIDTH), -6.0, -1.0),
        'rwkv_w2': normal((L, R_DECAY_RANK, R_WIDTH), 0.1),
        'rwkv_a0': normal((L, R_WIDTH), 0.1),
        'rwkv_a2': normal((L, R_A_RANK, R_WIDTH), R_A_RANK ** -0.5),
        'rwkv_g2': normal((L, R_GATE_RANK, R_WIDTH), R_GATE_RANK ** -0.5),
        'rwkv_k_k': 0.85 + normal((L, R_WIDTH), 0.02),
        'rwkv_k_a': 1.0 + normal((L, R_WIDTH), 0.02),
        'rwkv_r_k': normal((L, R_HEADS, R_HEAD_DIM), 0.1),
        'rwkv_ln_g': 1.0 + normal((L, R_WIDTH), 0.02),
        'rwkv_ln_b': normal((L, R_WIDTH), 0.02),
        'w_branch_m': normal((L, M_WIDTH, D_MODEL), M_WIDTH ** -0.5),
        'w_branch_a': normal((L, A_WIDTH, D_MODEL), A_WIDTH ** -0.5),
        'w_branch_r': normal((L, R_WIDTH, D_MODEL), R_WIDTH ** -0.5),
        'w_out': normal((L, D_MODEL, D_MODEL), D_MODEL ** -0.5),
        'norm2_g': 1.0 + normal((L, D_MODEL), 0.02),
        'w_gate_up': normal((L, D_MODEL, 2 * D_FF), D_MODEL ** -0.5),
        'w_down': normal((L, D_FF, D_MODEL), D_FF ** -0.5),
        'final_norm_g': 1.0 + normal((D_MODEL,), 0.02),
    }


def reference(x_prompt, x_sample, cache_cmp_k, cache_cmp_v, cache_slc_k, cache_slc_v,
              cache_win_k, cache_win_v, state_mlstm_conv, state_mlstm_C, state_mlstm_n,
              state_mlstm_m, state_rwkv_shift, state_rwkv, page_table, norm1_g, w_in,
              mlstm_if_bias, mlstm_conv_w, mlstm_conv_b, nsa_cmp_pe, nsa_cmp_w1, nsa_cmp_w2,
              rwkv_mu, rwkv_w0, rwkv_w2, rwkv_a0, rwkv_a2, rwkv_g2, rwkv_k_k, rwkv_k_a,
              rwkv_r_k, rwkv_ln_g, rwkv_ln_b, w_branch_m, w_branch_a, w_branch_r, w_out,
              norm2_g, w_gate_up, w_down, final_norm_g):
    f32 = jnp.float32
    bp = x_prompt.shape[0]
    xp, xs = x_prompt, x_sample
    states_p, states_s = [], []
    for l in range(DEPTH):
        lp = {'norm1_g': norm1_g[l], 'w_in': w_in[l], 'mlstm_if_bias': mlstm_if_bias[l],
              'mlstm_conv_w': mlstm_conv_w[l], 'mlstm_conv_b': mlstm_conv_b[l],
              'rwkv_mu': rwkv_mu[l], 'rwkv_w0': rwkv_w0[l], 'rwkv_w2': rwkv_w2[l],
              'rwkv_a0': rwkv_a0[l], 'rwkv_a2': rwkv_a2[l], 'rwkv_g2': rwkv_g2[l],
              'rwkv_k_k': rwkv_k_k[l], 'rwkv_k_a': rwkv_k_a[l], 'rwkv_r_k': rwkv_r_k[l],
              'rwkv_ln_g': rwkv_ln_g[l], 'rwkv_ln_b': rwkv_ln_b[l],
              'w_branch_m': w_branch_m[l], 'w_branch_a': w_branch_a[l], 'w_branch_r': w_branch_r[l],
              'w_out': w_out[l], 'norm2_g': norm2_g[l], 'w_gate_up': w_gate_up[l], 'w_down': w_down[l]}
        nsa_p = functools.partial(nsa_prompt, cmp_pe=nsa_cmp_pe[l], cmp_w1=nsa_cmp_w1[l], cmp_w2=nsa_cmp_w2[l])
        m_zero = (jnp.zeros((bp, M_CONV - 1, 2 * M_WIDTH), xp.dtype),
                  jnp.zeros((bp, M_HEADS, M_HEAD_DIM, M_HEAD_DIM), f32),
                  jnp.zeros((bp, M_HEADS, M_HEAD_DIM), f32),
                  jnp.zeros((bp, M_HEADS), f32))
        r_zero = (jnp.zeros((bp, 1, R_COLS), xp.dtype),
                  jnp.zeros((bp, R_HEADS, R_HEAD_DIM, R_HEAD_DIM), f32))
        xp, nsa_new, m_new, r_new = trunk_layer(xp, lp, nsa_p, m_zero, r_zero)
        states_p.append(nsa_new + m_new + r_new)
        nsa_s = functools.partial(nsa_sample, pool_kc=cache_cmp_k[l], pool_vc=cache_cmp_v[l],
                                  pool_ks=cache_slc_k[l], pool_vs=cache_slc_v[l],
                                  win_k=cache_win_k[l], win_v=cache_win_v[l], page_table=page_table,
                                  cmp_pe=nsa_cmp_pe[l], cmp_w1=nsa_cmp_w1[l], cmp_w2=nsa_cmp_w2[l])
        xs, nsa_new, m_new, r_new = trunk_layer(
            xs, lp, nsa_s,
            (state_mlstm_conv[l], state_mlstm_C[l], state_mlstm_n[l], state_mlstm_m[l]),
            (state_rwkv_shift[l], state_rwkv[l]))
        states_s.append(nsa_new + m_new + r_new)
    y_prompt = rmsnorm(xp, final_norm_g)
    y_sample = rmsnorm(xs, final_norm_g)
    (p_cmp_k, p_cmp_v, p_slc_k, p_slc_v, p_win_k, p_win_v, p_m_conv, p_m_C, p_m_n, p_m_m,
     p_r_shift, p_r_S) = [jnp.stack(a) for a in zip(*states_p)]
    (s_cmp_k, s_cmp_v, s_slc_k, s_slc_v, s_win_k, s_win_v, s_m_conv, s_m_C, s_m_n, s_m_m,
     s_r_shift, s_r_S) = [jnp.stack(a) for a in zip(*states_s)]
    return (y_prompt, y_sample,
            p_cmp_k, p_cmp_v, p_slc_k, p_slc_v, p_win_k, p_win_v,
            p_m_conv, p_m_C, p_m_n, p_m_m, p_r_shift, p_r_S,
            s_cmp_k, s_cmp_v, s_slc_k, s_slc_v, s_win_k, s_win_v,
            s_m_conv, s_m_C, s_m_n, s_m_m, s_r_shift, s_r_S)
```

```python
import functools
import math

import numpy as np
import jax
import jax.numpy as jnp
from jax import lax
from jax.experimental import pallas as pl
from jax.experimental.pallas import tpu as pltpu

F32 = jnp.float32
BF16 = jnp.bfloat16

D_MODEL = 1024
DEPTH = 2
PAGE_SIZE = 128
M_HEADS = 4
M_HEAD_DIM = 128
M_WIDTH = M_HEADS * M_HEAD_DIM
M_CONV = 4
A_HEADS = 8
A_KV_HEADS = 2
A_HEAD_DIM = 64
A_WIDTH = A_HEADS * A_HEAD_DIM
A_GROUP = A_HEADS // A_KV_HEADS
A_SCALE = A_HEAD_DIM ** -0.5
A_KV_LANES = A_KV_HEADS * A_HEAD_DIM
CMP_STRIDE = 16
CMP_BLOCK = 2 * CMP_STRIDE
CMP_HIDDEN = A_HEAD_DIM
SEL_BLOCK = 64
N_SEL = 16
WINDOW = 512
Q_BLOCK = 128
R_HEADS = 8
R_HEAD_DIM = 64
R_WIDTH = R_HEADS * R_HEAD_DIM
R_DECAY_RANK = 64
R_A_RANK = 64
R_GATE_RANK = 128
R_COLS = 3 * R_WIDTH + R_DECAY_RANK + R_A_RANK + R_GATE_RANK
LN_X_EPS = 64e-5
D_FF = ((-(-8 * D_MODEL // 3)) + 255) // 256 * 256
IN_SPLITS = (2 * M_WIDTH, M_WIDTH, M_WIDTH, 2 * M_HEADS,
             A_WIDTH, 6 * A_KV_HEADS * A_HEAD_DIM, 3 * A_HEADS,
             R_COLS, 3 * D_MODEL)
RMS_EPS = 1e-6
NEG = -1e30
FORCE = 1e9
LOWEST = -3e38

VMEM_LIMIT_BYTES = 56 * 1024 * 1024
SUBLANES = 8
LANES = 128


def _cparams(*sem):
    return pltpu.CompilerParams(dimension_semantics=sem, vmem_limit_bytes=VMEM_LIMIT_BYTES)


def _resident(shape):
    nd = len(shape)
    return pl.BlockSpec(shape, lambda *_: (0,) * nd, pipeline_mode=pl.Buffered(1))


def _dot(a, b):
    return jnp.dot(a.astype(BF16), b.astype(BF16), preferred_element_type=F32)


def _dot_nt(a, b):
    return lax.dot_general(a.astype(BF16), b.astype(BF16), (((1,), (1,)), ((), ())),
                           preferred_element_type=F32)


def _split3(x):
    hi = x.astype(BF16)
    r = x - hi.astype(F32)
    mid = r.astype(BF16)
    lo = (r - mid.astype(F32)).astype(BF16)
    return hi, mid, lo


def _dot_f32_rhs(a_bf16, x):
    hi, mid, lo = _split3(x)
    d = lambda p: jnp.dot(a_bf16, p, preferred_element_type=F32)
    return d(hi) + d(mid) + d(lo)


def _dot_f32_lhs(x, a_bf16):
    hi, mid, lo = _split3(x)
    d = lambda p: jnp.dot(p, a_bf16, preferred_element_type=F32)
    return d(hi) + d(mid) + d(lo)


def _transpose_f32(x, eye_bf16):
    hi, mid, lo = _split3(x)
    d = lambda p: lax.dot_general(eye_bf16, p, (((1,), (1,)), ((), ())), preferred_element_type=F32)
    return d(hi) + d(mid) + d(lo)


def _eye(n, dtype=BF16):
    r = lax.broadcasted_iota(jnp.int32, (n, n), 0)
    c = lax.broadcasted_iota(jnp.int32, (n, n), 1)
    return (r == c).astype(dtype)


def _sigmoid(x):
    return 1.0 / (1.0 + jnp.exp(-x))


def _silu(x):
    return x * _sigmoid(x)


def _softplus(x):
    return jnp.maximum(x, 0.0) + jnp.log(1.0 + jnp.exp(-jnp.abs(x)))


def _rmsnorm(x, g):
    return x * lax.rsqrt(jnp.mean(x * x, axis=-1, keepdims=True) + RMS_EPS) * g


def _norm_proj_kernel(x_ref, g_ref, *refs):
    n = len(refs) // 2
    hb = _rmsnorm(x_ref[...], g_ref[...]).astype(BF16)
    for w_ref, o_ref in zip(refs[:n], refs[n:]):
        o_ref[...] = jnp.dot(hb, w_ref[...], preferred_element_type=F32)


def norm_proj(x2d, g, weights, tm):
    m, d = x2d.shape
    assert m % tm == 0
    row = lambda i: (i, 0)
    in_specs = [pl.BlockSpec((tm, d), row), _resident((1, d))]
    in_specs += [_resident(w.shape) for w in weights]
    out_specs = [pl.BlockSpec((tm, w.shape[1]), row) for w in weights]
    out_shape = [jax.ShapeDtypeStruct((m, w.shape[1]), F32) for w in weights]
    return pl.pallas_call(
        _norm_proj_kernel, grid=(m // tm,), in_specs=in_specs, out_specs=out_specs,
        out_shape=out_shape, compiler_params=_cparams("parallel"), name="norm_proj",
    )(x2d, g.reshape(1, d), *weights)


def _merge_ffn_kernel(x_ref, ym_ref, ya_ref, yr_ref, mg_ref, wm_ref, wa_ref, wr_ref, wo_ref,
                      g2_ref, wgu_ref, wd_ref, gf_ref, o_ref, *, final_norm):
    mg = mg_ref[...]
    merged = (_sigmoid(mg[:, :D_MODEL]) * _dot(ym_ref[...], wm_ref[...])
              + _sigmoid(mg[:, D_MODEL:2 * D_MODEL]) * _dot(ya_ref[...], wa_ref[...])
              + _sigmoid(mg[:, 2 * D_MODEL:]) * _dot(yr_ref[...], wr_ref[...]))
    x1 = x_ref[...] + _dot(merged, wo_ref[...])
    gu = _dot(_rmsnorm(x1, g2_ref[...]), wgu_ref[...])
    x2 = x1 + _dot(_silu(gu[:, :D_FF]) * gu[:, D_FF:], wd_ref[...])
    o_ref[...] = _rmsnorm(x2, gf_ref[...]) if final_norm else x2


def merge_ffn(x2d, ym, ya, yr, mg, wm, wa, wr, wo, g2, wgu, wd, gf, tm, final_norm):
    m, d = x2d.shape
    row = lambda i: (i, 0)
    acts = [x2d, ym, ya, yr, mg]
    consts = [wm, wa, wr, wo, g2.reshape(1, d), wgu, wd, gf.reshape(1, d)]
    in_specs = [pl.BlockSpec((tm, a.shape[1]), row) for a in acts]
    in_specs += [_resident(c.shape) for c in consts]
    return pl.pallas_call(
        functools.partial(_merge_ffn_kernel, final_norm=final_norm),
        grid=(m // tm,), in_specs=in_specs, out_specs=pl.BlockSpec((tm, d), row),
        out_shape=jax.ShapeDtypeStruct((m, d), F32), compiler_params=_cparams("parallel"),
        name="merge_ffn",
    )(*acts, *consts)


def _mlstm_kernel(qk_ref, v_ref, o_ref, if_ref, conv0_ref, c0_ref, n0_ref, m0_ref,
                  cw_ref, cb_ref, ifb_ref,
                  y_ref, conv_out_ref, c_out_ref, n_out_ref, m_out_ref,
                  cbuf, c_scr, n_scr, m_scr, *, chunk, n_last):
    c = pl.program_id(1)
    nc = pl.num_programs(1)
    L = chunk
    pad = SUBLANES

    @pl.when(c == 0)
    def _():
        cbuf[0:pad, :] = conv0_ref[0]
        c_scr[...] = c0_ref[0]
        n_scr[...] = n0_ref[0]
        m_scr[...] = m0_ref[0]

    cbuf[pad:pad + L, :] = qk_ref[0]
    acc = cb_ref[...]
    for j in range(M_CONV):
        acc = acc + cbuf[pad - (M_CONV - 1) + j: pad - (M_CONV - 1) + j + L, :] * cw_ref[j:j + 1, :]
    qk = _silu(acc)

    @pl.when(c == nc - 1)
    def _():
        conv_out_ref[0] = cbuf[n_last:n_last + pad, :]

    cbuf[0:pad, :] = cbuf[L:L + pad, :]

    gates = if_ref[0] + ifb_ref[...]
    lane8 = lax.broadcasted_iota(jnp.int32, (L, 2 * M_HEADS), 1)
    lf_all = -_softplus(-gates)
    lif = jnp.where(lane8 < M_HEADS, gates, lf_all)
    row_id = lax.broadcasted_iota(jnp.int32, (L, 2 * M_HEADS), 0)
    is_last_chunk = c == nc - 1
    if n_last < L:
        dead = jnp.logical_and(is_last_chunk, row_id >= n_last)
        lif = jnp.where(dead, jnp.where(lane8 < M_HEADS, NEG, 0.0), lif)
    t_i = lax.broadcasted_iota(jnp.int32, (L, L), 0)
    s_i = lax.broadcasted_iota(jnp.int32, (L, L), 1)
    causal = s_i <= t_i
    tri = causal.astype(BF16)
    csum = _dot_f32_rhs(tri, lif)
    lib = jnp.where(lane8 < M_HEADS, lif, csum)
    lib_t = _transpose_f32(lib, _eye(2 * M_HEADS))
    eye_h = _eye(M_HEAD_DIM)
    last = n_last - 1

    for h in range(M_HEADS):
        sl = slice(h * M_HEAD_DIM, (h + 1) * M_HEAD_DIM)
        q = qk[:, sl]
        k = qk[:, M_WIDTH + h * M_HEAD_DIM: M_WIDTH + (h + 1) * M_HEAD_DIM] * (M_HEAD_DIM ** -0.5)
        v = v_ref[0, :, sl]
        li_col = lib[:, h:h + 1]
        b_col = lib[:, M_HEADS + h:M_HEADS + h + 1]
        li_row = lib_t[h:h + 1, :]
        b_row = lib_t[M_HEADS + h:M_HEADS + h + 1, :]
        m_prev = m_scr[0:1, h:h + 1]
        dmat = jnp.where(causal, b_col - b_row + li_row, NEG)
        inter = b_col + m_prev
        m_t = jnp.maximum(inter, jnp.max(dmat, axis=1, keepdims=True))
        w_intra = jnp.exp(dmat - m_t)
        w_inter = jnp.exp(inter - m_t)
        att = w_intra * _dot_nt(q, k)
        c_h = c_scr[h]
        n_h = n_scr[h:h + 1, :]
        num = _dot(att, v) + w_inter * _dot(q, c_h)
        den = jnp.sum(att, axis=1, keepdims=True) + w_inter * jnp.sum(q * n_h, axis=1, keepdims=True)
        hh = num / jnp.maximum(jnp.abs(den), jnp.exp(-m_t))
        y_ref[0, :, sl] = _sigmoid(o_ref[0, :, sl]) * hh

        m_new = m_t[last:last + 1, :]
        b_last = b_col[last:last + 1, :]
        scale = jnp.exp(b_last + m_prev - m_new)
        w_end = jnp.exp(b_last - b_col + li_col - m_new)
        kw = k * w_end
        kw_t = lax.dot_general(eye_h, kw.astype(BF16), (((1,), (1,)), ((), ())),
                               preferred_element_type=F32)
        c_scr[h] = scale * c_h + _dot(kw_t, v)
        n_scr[h:h + 1, :] = scale * n_h + jnp.sum(kw, axis=0, keepdims=True)
        m_scr[0:1, h:h + 1] = m_new

    @pl.when(is_last_chunk)
    def _():
        c_out_ref[0] = c_scr[...]
        n_out_ref[0] = n_scr[...]
        m_out_ref[0] = m_scr[...]


def mlstm(m_qk, m_v, m_o, m_if, conv0, c0, n0, m0, conv_w, conv_b, if_bias, chunk):
    B, T, _ = m_qk.shape
    nc = -(-T // chunk)
    tp = nc * chunk
    n_last = T - (nc - 1) * chunk
    if tp != T:
        padt = lambda a: jnp.pad(a, ((0, 0), (0, tp - T), (0, 0)))
        m_qk, m_v, m_o, m_if = padt(m_qk), padt(m_v), padt(m_o), padt(m_if)
    conv0p = jnp.pad(conv0, ((0, 0), (SUBLANES - (M_CONV - 1), 0), (0, 0)))
    n0p = jnp.pad(n0, ((0, 0), (0, SUBLANES - M_HEADS), (0, 0)))
    m0p = jnp.pad(m0, ((0, 0), (0, LANES - M_HEADS))).reshape(B, 1, LANES)
    cw = jnp.pad(conv_w, ((0, SUBLANES - M_CONV), (0, 0)))
    tile = lambda w: pl.BlockSpec((1, chunk, w), lambda b, c: (b, c, 0))
    perb = lambda *s: pl.BlockSpec((1,) + s, lambda b, c: (b,) + (0,) * len(s))
    outs = pl.pallas_call(
        functools.partial(_mlstm_kernel, chunk=chunk, n_last=n_last),
        grid=(B, nc),
        in_specs=[tile(2 * M_WIDTH), tile(M_WIDTH), tile(M_WIDTH), tile(2 * M_HEADS),
                  perb(SUBLANES, 2 * M_WIDTH), perb(M_HEADS, M_HEAD_DIM, M_HEAD_DIM),
                  perb(SUBLANES, M_HEAD_DIM), perb(1, LANES),
                  _resident((SUBLANES, 2 * M_WIDTH)), _resident((1, 2 * M_WIDTH)),
                  _resident((1, 2 * M_HEADS))],
        out_specs=[tile(M_WIDTH), perb(SUBLANES, 2 * M_WIDTH),
                   perb(M_HEADS, M_HEAD_DIM, M_HEAD_DIM), perb(SUBLANES, M_HEAD_DIM), perb(1, LANES)],
        out_shape=[jax.ShapeDtypeStruct((B, tp, M_WIDTH), F32),
                   jax.ShapeDtypeStruct((B, SUBLANES, 2 * M_WIDTH), F32),
                   jax.ShapeDtypeStruct((B, M_HEADS, M_HEAD_DIM, M_HEAD_DIM), F32),
                   jax.ShapeDtypeStruct((B, SUBLANES, M_HEAD_DIM), F32),
                   jax.ShapeDtypeStruct((B, 1, LANES), F32)],
        scratch_shapes=[pltpu.VMEM((chunk + SUBLANES, 2 * M_WIDTH), F32),
                        pltpu.VMEM((M_HEADS, M_HEAD_DIM, M_HEAD_DIM), F32),
                        pltpu.VMEM((SUBLANES, M_HEAD_DIM), F32),
                        pltpu.VMEM((1, LANES), F32)],
        compiler_params=_cparams("parallel", "arbitrary"), name="mlstm",
    )(m_qk, m_v, m_o, m_if, conv0p, c0, n0p, m0p, cw, conv_b.reshape(1, -1), if_bias.reshape(1, -1))
    y, conv_o, c_o, n_o, m_o_ = outs
    return (y[:, :T], conv_o[:, SUBLANES - (M_CONV - 1):], c_o, n_o[:, :M_HEADS], m_o_[:, 0, :M_HEADS])


def _rwkv_prep_kernel(cols_ref, shift0_ref, mu_ref, w0_ref, w2_ref, a0_ref, a2_ref, g2_ref,
                      kk_ref, ka_ref, rk_ref,
                      r_o, w_o, k_o, v_o, a_o, b_o, g_o, bonus_o, carry):
    t = pl.program_id(1)

    @pl.when(t == 0)
    def _():
        carry[...] = shift0_ref[0]

    cols = cols_ref[0]
    tm = cols.shape[0]
    row = lax.broadcasted_iota(jnp.int32, cols.shape, 0)
    prev = jnp.where(row == 0, carry[...], pltpu.roll(cols, 1, 0))
    carry[...] = cols[tm - 1:tm, :]
    mixed = cols + (prev - cols) * mu_ref[...]
    W = R_WIDTH
    r = mixed[:, :W]
    k = mixed[:, W:2 * W]
    v = mixed[:, 2 * W:3 * W]
    o = 3 * W
    wd = mixed[:, o:o + R_DECAY_RANK]
    ad = mixed[:, o + R_DECAY_RANK:o + R_DECAY_RANK + R_A_RANK]
    gd = mixed[:, o + R_DECAY_RANK + R_A_RANK:]
    w = -_softplus(-(w0_ref[...] + _dot(jnp.tanh(wd), w2_ref[...]))) - 0.5
    decay = jnp.exp(-jnp.exp(w))
    a = _sigmoid(a0_ref[...] + _dot(ad, a2_ref[...]))
    g_o[0] = _dot(_sigmoid(gd), g2_ref[...])
    kk = k * kk_ref[...]
    k2 = k * (1.0 + (a - 1.0) * ka_ref[...])
    rkk = r * k2 * rk_ref[...]
    for h in range(R_HEADS):
        sl = slice(h * R_HEAD_DIM, (h + 1) * R_HEAD_DIM)
        kkh = kk[:, sl]
        kkh = kkh * lax.rsqrt(jnp.maximum(jnp.sum(kkh * kkh, axis=1, keepdims=True), 1e-24))
        r_o[0, h] = r[:, sl]
        w_o[0, h] = decay[:, sl]
        k_o[0, h] = k2[:, sl]
        v_o[0, h] = v[:, sl]
        a_o[0, h] = -kkh
        b_o[0, h] = kkh * a[:, sl]
        bonus_o[0, :, sl] = jnp.sum(rkk[:, sl], axis=1, keepdims=True) * v[:, sl]


def _rwkv_scan_kernel(r_ref, w_ref, k_ref, v_ref, a_ref, b_ref, s0_ref, y_ref, s_out_ref, s_scr,
                      *, steps, n_valid):
    t = pl.program_id(1)
    nt = pl.num_programs(1)
    G = SUBLANES
    N = R_HEAD_DIM

    @pl.when(t == 0)
    def _():
        s_scr[...] = s0_ref[0]

    eye_n = _eye(N)
    eye_g = _eye(G)
    lane_g = lax.broadcasted_iota(jnp.int32, (N, G), 1)
    row_g = lax.broadcasted_iota(jnp.int32, (G, N), 0)

    def group(gi, carry):
        off = pl.multiple_of(gi * G, G)
        for h in range(R_HEADS):
            ld = lambda ref: ref[0, h, pl.ds(off, G), :]
            rb, wb, kb, vb, ab, bb = ld(r_ref), ld(w_ref), ld(k_ref), ld(v_ref), ld(a_ref), ld(b_ref)
            if n_valid % G:
                dead = (t * steps + off + row_g) >= n_valid
                wb = jnp.where(dead, 1.0, wb)
                kb = jnp.where(dead, 0.0, kb)
                vb = jnp.where(dead, 0.0, vb)
                ab = jnp.where(dead, 0.0, ab)
                bb = jnp.where(dead, 0.0, bb)
            v_t = _transpose_f32(vb, eye_n)
            s = s_scr[h]
            yc = jnp.zeros((N, G), F32)
            for u in range(G):
                sa = jnp.sum(s * ab[u:u + 1, :], axis=1, keepdims=True)
                s = s * wb[u:u + 1, :] + sa * bb[u:u + 1, :] + v_t[:, u:u + 1] * kb[u:u + 1, :]
                y = jnp.sum(s * rb[u:u + 1, :], axis=1, keepdims=True)
                yc = jnp.where(lane_g == u, y, yc)
            s_scr[h] = s
            y_ref[0, h, pl.ds(off, G), :] = _transpose_f32(yc, eye_g)
        return carry

    lax.fori_loop(0, steps // G, group, 0)

    @pl.when(t == nt - 1)
    def _():
        s_out_ref[0] = s_scr[...]


def _rwkv_post_kernel(y_ref, bonus_ref, g_ref, lng_ref, lnb_ref, o_ref):
    for h in range(R_HEADS):
        sl = slice(h * R_HEAD_DIM, (h + 1) * R_HEAD_DIM)
        y = y_ref[0, h]
        mean = jnp.mean(y, axis=1, keepdims=True)
        yc = y - mean
        var = jnp.mean(yc * yc, axis=1, keepdims=True)
        yn = yc * lax.rsqrt(var + LN_X_EPS) * lng_ref[:, sl] + lnb_ref[:, sl]
        o_ref[0, :, sl] = (yn + bonus_ref[0, :, sl]) * g_ref[0, :, sl]


def rwkv(cols, shift0, s0, p, tm, steps):
    B, T, _ = cols.shape
    tp = -(-T // tm) * tm
    assert tm % steps == 0 and steps % SUBLANES == 0
    colsp = jnp.pad(cols, ((0, 0), (0, tp - T), (0, 0))) if tp != T else cols
    nt = tp // tm
    vec = lambda a: a.reshape(1, -1)
    consts = [vec(p['rwkv_mu']), vec(p['rwkv_w0']), p['rwkv_w2'].astype(BF16), vec(p['rwkv_a0']),
              p['rwkv_a2'].astype(BF16), p['rwkv_g2'].astype(BF16), vec(p['rwkv_k_k']),
              vec(p['rwkv_k_a']), vec(p['rwkv_r_k'])]
    hm = jax.ShapeDtypeStruct((B, R_HEADS, tp, R_HEAD_DIM), F32)
    fl = jax.ShapeDtypeStruct((B, tp, R_WIDTH), F32)
    hm_spec = lambda n: pl.BlockSpec((1, R_HEADS, n, R_HEAD_DIM), lambda b, t: (b, 0, t, 0))
    fl_spec = lambda n, w: pl.BlockSpec((1, n, w), lambda b, t: (b, t, 0))
    r, w, k, v, a, bb, g, bonus = pl.pallas_call(
        _rwkv_prep_kernel, grid=(B, nt),
        in_specs=[fl_spec(tm, R_COLS), pl.BlockSpec((1, 1, R_COLS), lambda b, t: (b, 0, 0))]
                 + [_resident(c.shape) for c in consts],
        out_specs=[hm_spec(tm)] * 6 + [fl_spec(tm, R_WIDTH)] * 2,
        out_shape=[hm] * 6 + [fl] * 2,
        scratch_shapes=[pltpu.VMEM((1, R_COLS), F32)],
        compiler_params=_cparams("parallel", "arbitrary"), name="rwkv_prep",
    )(colsp, shift0, *consts)
    st_spec = pl.BlockSpec((1, R_HEADS, R_HEAD_DIM, R_HEAD_DIM), lambda b, t: (b, 0, 0, 0))
    y, s_new = pl.pallas_call(
        functools.partial(_rwkv_scan_kernel, steps=steps, n_valid=T),
        grid=(B, tp // steps),
        in_specs=[hm_spec(steps)] * 6 + [st_spec],
        out_specs=[hm_spec(steps), st_spec],
        out_shape=[hm, jax.ShapeDtypeStruct(s0.shape, F32)],
        scratch_shapes=[pltpu.VMEM((R_HEADS, R_HEAD_DIM, R_HEAD_DIM), F32)],
        compiler_params=_cparams("parallel", "arbitrary"), name="rwkv_scan",
    )(r, w, k, v, a, bb, s0)
    out = pl.pallas_call(
        _rwkv_post_kernel, grid=(B, nt),
        in_specs=[hm_spec(tm), fl_spec(tm, R_WIDTH), fl_spec(tm, R_WIDTH),
                  _resident((1, R_WIDTH)), _resident((1, R_WIDTH))],
        out_specs=fl_spec(tm, R_WIDTH), out_shape=fl,
        compiler_params=_cparams("parallel", "parallel"), name="rwkv_post",
    )(y, bonus, g, vec(p['rwkv_ln_g']), vec(p['rwkv_ln_b']))
    return out[:, :T], s_new


def compress_weights(pe, w1, w2):
    d, hdim, G = A_HEAD_DIM, CMP_HIDDEN, A_KV_HEADS
    w1r = w1.reshape(2, CMP_STRIDE, d, hdim)
    wl = jnp.zeros((CMP_STRIDE, G, d, 2, G, hdim), F32)
    for g in range(G):
        wl = wl.at[:, g, :, :, g, :].set(jnp.transpose(w1r, (1, 2, 0, 3)))
    wl = wl.reshape(CMP_STRIDE, G * d, 2 * G * hdim).astype(BF16)
    wpe = jnp.transpose(w1r, (1, 0, 2, 3)).reshape(CMP_STRIDE, 2 * d, hdim)
    wpe = jnp.concatenate([wpe] * G, axis=-1).astype(BF16)
    pe_l = jnp.transpose(pe.reshape(2, CMP_STRIDE, d), (1, 0, 2)).reshape(CMP_STRIDE, 1, 2 * d)
    pe_l = jnp.broadcast_to(pe_l, (CMP_STRIDE, SUBLANES, 2 * d))
    w2bd = jnp.zeros((G, hdim, G, d), F32)
    for g in range(G):
        w2bd = w2bd.at[g, :, g, :].set(w2)
    return wl, wpe, pe_l, w2bd.reshape(G * hdim, G * d).astype(BF16)


def _compress_halves(x_ref, wl_ref, n):
    acc = None
    for l in range(CMP_STRIDE):
        part = _dot(x_ref[0, pl.ds(l, n, stride=CMP_STRIDE), :], wl_ref[l])
        acc = part if acc is None else acc + part
    return acc


def _compress_finish(halves, wpe_ref, pe_ref, w2_ref):
    n = halves.shape[0]
    gh = A_KV_HEADS * CMP_HIDDEN
    pe_term = None
    for l in range(CMP_STRIDE):
        part = _dot(pe_ref[l], wpe_ref[l])
        pe_term = part if pe_term is None else pe_term + part
    second = pltpu.roll(halves[:, gh:], n - 1, 0)
    hid = _silu(halves[:, :gh] + second + pe_term[0:1, :])
    return _dot(hid, w2_ref[...])


def _compress_kernel(x_ref, wl_ref, wpe_ref, pe_ref, w2_ref, o_ref):
    n = o_ref.shape[1]
    o_ref[0] = _compress_finish(_compress_halves(x_ref, wl_ref, n), wpe_ref, pe_ref, w2_ref)


def compress_rows(rows, cw):
    B, T, W = rows.shape
    n = T // CMP_STRIDE
    return pl.pallas_call(
        _compress_kernel, grid=(B,),
        in_specs=[pl.BlockSpec((1, T, W), lambda b: (b, 0, 0))] + [_resident(c.shape) for c in cw],
        out_specs=pl.BlockSpec((1, n, W), lambda b: (b, 0, 0)),
        out_shape=jax.ShapeDtypeStruct((B, n, W), F32),
        compiler_params=_cparams("parallel"), name="nsa_compress",
    )(rows, *cw)


def _topk_mask(rank, k):
    nb = rank.shape[1]
    lane = lax.broadcasted_iota(jnp.int32, rank.shape, 1).astype(F32)
    sel = jnp.zeros(rank.shape, F32)
    for _ in range(k):
        m = jnp.max(rank, axis=1, keepdims=True)
        idx = jnp.min(jnp.where(rank == m, lane, float(nb)), axis=1, keepdims=True)
        hit = lane == idx
        sel = jnp.where(hit, 1.0, sel)
        rank = jnp.where(hit, LOWEST, rank)
    return sel


def _flash_update(state, s, mask, v):
    m_old, l_old, acc = state
    m_new = jnp.maximum(m_old, jnp.max(jnp.where(mask, s, NEG), axis=1, keepdims=True))
    p = jnp.where(mask, jnp.exp(s - m_new), 0.0)
    alpha = jnp.exp(m_old - m_new)
    return (m_new, alpha * l_old + jnp.sum(p, axis=1, keepdims=True), alpha * acc + _dot(p, v))


def _nsa_prompt_kernel(q_ref, gate_ref, kc_ref, vc_ref, ks_ref, vs_ref, kw_ref, vw_ref, cov_ref,
                       o_ref, *, seq, tk):
    i = pl.program_id(1)
    QB, d, J = Q_BLOCK, A_HEAD_DIM, A_GROUP
    nc = kc_ref.shape[1]
    nb = cov_ref.shape[1]
    st = i * QB
    q = q_ref[0] * A_SCALE
    gate = _sigmoid(gate_ref[0])
    rows = J * QB
    t_q = st + lax.broadcasted_iota(jnp.int32, (QB, 1), 0)
    t_rows = jnp.concatenate([t_q] * J, axis=0)
    blk_end = lax.broadcasted_iota(jnp.int32, (rows, nc), 1) * CMP_STRIDE + (CMP_BLOCK - 1)
    cmp_mask = blk_end <= t_rows
    jj = lax.broadcasted_iota(jnp.int32, (QB, nb), 1)
    cur = t_q // SEL_BLOCK
    forced = (jj == 0) | (jj == cur) | (jj == cur - 1)
    zeros = lambda w: jnp.zeros((rows, w), F32)

    for g in range(A_KV_HEADS):
        gl = slice(g * d, (g + 1) * d)
        qg = jnp.concatenate([q[:, (g * J + j) * d:(g * J + j + 1) * d] for j in range(J)],
                             axis=0).astype(BF16)
        s = jnp.where(cmp_mask, _dot_nt(qg, kc_ref[0, :, gl]), NEG)
        e = jnp.exp(s - jnp.max(s, axis=1, keepdims=True))
        p = jnp.where(cmp_mask, e / jnp.sum(e, axis=1, keepdims=True), 0.0)
        o_cmp = _dot(p, vc_ref[0, :, gl])
        p_grp = p[0:QB]
        for j in range(1, J):
            p_grp = p_grp + p[j * QB:(j + 1) * QB]
        score = _dot_f32_lhs(p_grp, cov_ref[...])
        rank = jnp.where(jj <= cur, jnp.where(forced, FORCE, score), NEG)
        sel = _topk_mask(rank, min(N_SEL, nb)).astype(BF16)

        def sel_tile(kt, state):
            k0 = pl.multiple_of(kt * tk, tk)
            s_pos = k0 + lax.broadcasted_iota(jnp.int32, (1, tk), 1)
            blk_of = k0 // SEL_BLOCK + lax.broadcasted_iota(jnp.int32, (nb, tk), 1) // SEL_BLOCK
            expand = (lax.broadcasted_iota(jnp.int32, (nb, tk), 0) == blk_of).astype(BF16)
            chosen = jnp.dot(sel, expand, preferred_element_type=F32) > 0.5
            mask = jnp.concatenate([chosen & (s_pos <= t_q)] * J, axis=0)
            sc = _dot_nt(qg, ks_ref[0, pl.ds(k0, tk), gl])
            return _flash_update(state, sc, mask, vs_ref[0, pl.ds(k0, tk), gl])

        init = (jnp.full((rows, 1), NEG, F32), jnp.zeros((rows, 1), F32), zeros(d))
        _, l_sel, acc_sel = lax.fori_loop(0, (st + QB - 1) // tk + 1, sel_tile, init)
        o_sel = acc_sel / l_sel

        def win_tile(kt, state):
            k0 = pl.multiple_of(kt * QB, QB)
            rel = t_rows - (k0 + lax.broadcasted_iota(jnp.int32, (1, QB), 1))
            mask = (rel >= 0) & (rel < WINDOW)
            sc = _dot_nt(qg, kw_ref[0, pl.ds(k0, QB), gl])
            return _flash_update(state, sc, mask, vw_ref[0, pl.ds(k0, QB), gl])

        _, l_win, acc_win = lax.fori_loop(jnp.maximum(i - WINDOW // QB, 0), i + 1, win_tile, init)
        o_win = acc_win / l_win

        for j in range(J):
            h = g * J + j
            rs = slice(j * QB, (j + 1) * QB)
            o_ref[0, :, h * d:(h + 1) * d] = (gate[:, h:h + 1] * o_cmp[rs]
                                              + gate[:, A_HEADS + h:A_HEADS + h + 1] * o_sel[rs]
                                              + gate[:, 2 * A_HEADS + h:2 * A_HEADS + h + 1] * o_win[rs])


def _covers(n_cmp, n_blocks):
    i = np.arange(n_cmp)[:, None]
    j = np.arange(n_blocks)[None, :]
    return (i * CMP_STRIDE < (j + 1) * SEL_BLOCK) & (i * CMP_STRIDE + CMP_BLOCK > j * SEL_BLOCK)


def nsa_prompt_attend(a_q, a_gate, kc, vc, ks, vs, kw, vw):
    B, T, _ = a_q.shape
    nc = kc.shape[1]
    nb = -(-T // SEL_BLOCK)
    tk = min(4 * Q_BLOCK, T)
    cov = jnp.asarray(_covers(nc, nb), BF16)
    tile = lambda w: pl.BlockSpec((1, Q_BLOCK, w), lambda b, i: (b, i, 0))
    full = lambda n, w: pl.BlockSpec((1, n, w), lambda b, i: (b, 0, 0))
    W = A_KV_LANES
    return pl.pallas_call(
        functools.partial(_nsa_prompt_kernel, seq=T, tk=tk),
        grid=(B, T // Q_BLOCK),
        in_specs=[tile(A_WIDTH), tile(3 * A_HEADS), full(nc, W), full(nc, W),
                  full(T, W), full(T, W), full(T, W), full(T, W), _resident(cov.shape)],
        out_specs=tile(A_WIDTH),
        out_shape=jax.ShapeDtypeStruct((B, T, A_WIDTH), F32),
        compiler_params=_cparams("parallel", "arbitrary"), name="nsa_prompt",
    )(a_q, a_gate, kc, vc, ks, vs, kw, vw, cov)


POOL_TILE_PAGES = 64
GATHER_PAGES = 16
CHUNKS_PER_PAGE = PAGE_SIZE // CMP_STRIDE
HEAD_ROWS = SUBLANES


def _pool_halves_kernel(x_ref, wl_ref, o_ref):
    o_ref[0] = _compress_halves(x_ref, wl_ref, o_ref.shape[1])


def pool_halves(pool, wl):
    n_pool, page, W = pool.shape
    pt = POOL_TILE_PAGES
    assert n_pool % pt == 0
    nt = n_pool // pt
    hw = wl.shape[2]
    out = pl.pallas_call(
        _pool_halves_kernel, grid=(nt,),
        in_specs=[pl.BlockSpec((1, pt * page, W), lambda i: (i, 0, 0)), _resident(wl.shape)],
        out_specs=pl.BlockSpec((1, pt * CHUNKS_PER_PAGE, hw), lambda i: (i, 0, 0)),
        out_shape=jax.ShapeDtypeStruct((nt, pt * CHUNKS_PER_PAGE, hw), F32),
        compiler_params=_cparams("parallel"), name="nsa_pool_halves",
    )(pool.reshape(nt, pt * page, W), wl)
    return out.reshape(n_pool, CHUNKS_PER_PAGE, hw)


def _nsa_sample_cmp_kernel(pt_ref, *refs, past_len):
    del pt_ref
    ng = GATHER_PAGES
    hk_refs, hv_refs = refs[:ng], refs[ng:2 * ng]
    (q_ref, wink_ref, winv_ref, kwn_ref, vwn_ref, wpek_ref, pek_ref, w2k_ref,
     wpev_ref, pev_ref, w2v_ref, cov_ref, ocmp_ref, owin_ref, idx_ref, hk_scr, hv_scr) = refs[2 * ng:]
    c = pl.program_id(1)
    rows_per_step = ng * CHUNKS_PER_PAGE
    base = pl.multiple_of(c * rows_per_step, rows_per_step)
    for i in range(ng):
        hk_scr[pl.ds(base + i * CHUNKS_PER_PAGE, CHUNKS_PER_PAGE), :] = hk_refs[i][0]
        hv_scr[pl.ds(base + i * CHUNKS_PER_PAGE, CHUNKS_PER_PAGE), :] = hv_refs[i][0]

    @pl.when(c == pl.num_programs(1) - 1)
    def _():
        d, R8 = A_HEAD_DIM, HEAD_ROWS
        kc = _compress_finish(hk_scr[...], wpek_ref, pek_ref, w2k_ref)
        vc = _compress_finish(hv_scr[...], wpev_ref, pev_ref, w2v_ref)
        nc = kc.shape[0]
        nb = cov_ref.shape[1]
        keep = wink_ref.shape[1]
        q_pos = past_len
        blk_end = lax.broadcasted_iota(jnp.int32, (R8, nc), 1) * CMP_STRIDE + (CMP_BLOCK - 1)
        cmp_mask = (blk_end <= q_pos) & (lax.broadcasted_iota(jnp.int32, (R8, nc), 1) < nc - 1)
        real_head = lax.broadcasted_iota(jnp.int32, (R8, nc), 0) < A_GROUP
        row8 = lax.broadcasted_iota(jnp.int32, (R8, nb), 0)
        jj = lax.broadcasted_iota(jnp.int32, (R8, nb), 1)
        cur = q_pos // SEL_BLOCK
        forced = (jj == 0) | (jj == cur) | (jj == cur - 1)
        k_idx = lax.broadcasted_iota(jnp.int32, (R8, keep), 1)
        rel = q_pos - (past_len - keep + k_idx)
        win_mask = (rel >= 0) & (rel < WINDOW)
        rank = None
        for g in range(A_KV_HEADS):
            gl = slice(g * d, (g + 1) * d)
            qg = q_ref[0, g] * A_SCALE
            s = jnp.where(cmp_mask, _dot_nt(qg, kc[:, gl]), NEG)
            e = jnp.exp(s - jnp.max(s, axis=1, keepdims=True))
            p = jnp.where(cmp_mask, e / jnp.sum(e, axis=1, keepdims=True), 0.0)
            ocmp_ref[0, g] = _dot(p, vc[:, gl])
            p_grp = jnp.sum(jnp.where(real_head, p, 0.0), axis=0, keepdims=True)
            score = _dot_f32_lhs(jnp.broadcast_to(p_grp, (R8, nc)), cov_ref[...])
            rank_g = jnp.where(jj <= cur, jnp.where(forced, FORCE, score), NEG)
            rank = rank_g if rank is None else jnp.where(row8 == g, rank_g, rank)
            sw = jnp.where(win_mask, _dot_nt(qg, wink_ref[0, :, gl]), NEG)
            s_new = jnp.sum(qg * kwn_ref[0, :, gl], axis=1, keepdims=True)
            m = jnp.maximum(jnp.max(sw, axis=1, keepdims=True), s_new)
            ew = jnp.where(win_mask, jnp.exp(sw - m), 0.0)
            e_new = jnp.exp(s_new - m)
            owin_ref[0, g] = ((_dot(ew, winv_ref[0, :, gl]) + e_new * vwn_ref[0, :, gl])
                              / (jnp.sum(ew, axis=1, keepdims=True) + e_new))
        lane = jj.astype(F32)
        out_lane = lax.broadcasted_iota(jnp.int32, (R8, LANES), 1)
        picks = jnp.zeros((R8, LANES), F32)
        for it in range(N_SEL - 1):
            mx = jnp.max(rank, axis=1, keepdims=True)
            idx = jnp.min(jnp.where(rank == mx, lane, float(nb)), axis=1, keepdims=True)
            picks = jnp.where(out_lane == it, idx, picks)
            rank = jnp.where(lane == idx, LOWEST, rank)
        idx_ref[0] = picks.astype(jnp.int32)


def nsa_sample_cmp(page_table, hk_pool, hv_pool, q_hm, win_k, win_v, kw_new, vw_new, cwk, cwv, past_len):
    B, n_pages = page_table.shape
    ng = GATHER_PAGES
    assert n_pages % ng == 0
    nc = n_pages * CHUNKS_PER_PAGE
    hw = hk_pool.shape[2]
    nb = past_len // SEL_BLOCK
    cov = jnp.asarray(_covers(nc, nb), BF16)
    keep = win_k.shape[1]
    W = A_KV_LANES
    page_spec = lambda i: pl.BlockSpec((1, CHUNKS_PER_PAGE, hw), lambda b, c, pt: (pt[b, c * ng + i], 0, 0))
    perb = lambda *s: pl.BlockSpec((1,) + s, lambda b, c, pt: (b,) + (0,) * len(s))
    const = lambda a: pl.BlockSpec(a.shape, lambda b, c, pt: (0,) * a.ndim)
    _, wpek, pek, w2k = cwk
    _, wpev, pev, w2v = cwv
    consts = [wpek, pek, w2k, wpev, pev, w2v, cov]
    hm = jax.ShapeDtypeStruct((B, A_KV_HEADS, HEAD_ROWS, A_HEAD_DIM), F32)
    grid_spec = pltpu.PrefetchScalarGridSpec(
        num_scalar_prefetch=1, grid=(B, n_pages // ng),
        in_specs=[page_spec(i) for i in range(ng)] * 2
                 + [perb(A_KV_HEADS, HEAD_ROWS, A_HEAD_DIM), perb(keep, W), perb(keep, W), perb(1, W), perb(1, W)]
                 + [const(a) for a in consts],
        out_specs=[perb(A_KV_HEADS, HEAD_ROWS, A_HEAD_DIM), perb(A_KV_HEADS, HEAD_ROWS, A_HEAD_DIM),
                   perb(HEAD_ROWS, LANES)],
        scratch_shapes=[pltpu.VMEM((nc, hw), F32), pltpu.VMEM((nc, hw), F32)])
    return pl.pallas_call(
        functools.partial(_nsa_sample_cmp_kernel, past_len=past_len), grid_spec=grid_spec,
        out_shape=[hm, hm, jax.ShapeDtypeStruct((B, HEAD_ROWS, LANES), jnp.int32)],
        compiler_params=_cparams("parallel", "arbitrary"), name="nsa_sample_cmp",
    )(page_table, *([hk_pool] * ng), *([hv_pool] * ng), q_hm, win_k, win_v, kw_new, vw_new, *consts)


def _nsa_sample_sel_kernel(pt_ref, pk_ref, k0_ref, k1_ref, v0_ref, v1_ref, q_ref, gate_ref,
                           ksn_ref, vsn_ref, ocmp_ref, owin_ref, o_ref, m_scr, l_scr, acc_scr):
    del pt_ref, pk_ref
    k = pl.program_id(1)
    d = A_HEAD_DIM
    k_refs, v_refs = (k0_ref, k1_ref), (v0_ref, v1_ref)
    for g in range(A_KV_HEADS):
        gl = slice(g * d, (g + 1) * d)
        qg = q_ref[0, g] * A_SCALE

        @pl.when(k == 0)
        def _():
            m_scr[g] = jnp.sum(qg * ksn_ref[0, :, gl], axis=1, keepdims=True)
            l_scr[g] = jnp.ones((HEAD_ROWS, 1), F32)
            acc_scr[g] = jnp.broadcast_to(vsn_ref[0, :, gl], (HEAD_ROWS, d))

        s = _dot_nt(qg, k_refs[g][0, :, gl])
        m_old = m_scr[g]
        m_new = jnp.maximum(m_old, jnp.max(s, axis=1, keepdims=True))
        p = jnp.exp(s - m_new)
        alpha = jnp.exp(m_old - m_new)
        m_scr[g] = m_new
        l_scr[g] = alpha * l_scr[g] + jnp.sum(p, axis=1, keepdims=True)
        acc_scr[g] = alpha * acc_scr[g] + _dot(p, v_refs[g][0, :, gl])

        @pl.when(k == pl.num_programs(1) - 1)
        def _():
            gate = _sigmoid(gate_ref[0, g])
            o_ref[0, g] = (gate[:, 0:1] * ocmp_ref[0, g] + gate[:, 1:2] * (acc_scr[g] / l_scr[g])
                           + gate[:, 2:3] * owin_ref[0, g])


def nsa_sample_sel(page_table, picks, pool_k, pool_v, q_hm, gate_hm, ks_new, vs_new, o_cmp, o_win):
    B = page_table.shape[0]
    n_pool, page, W = pool_k.shape
    per_page = page // SEL_BLOCK
    pk2 = pool_k.reshape(n_pool * per_page, SEL_BLOCK, W)
    pv2 = pool_v.reshape(n_pool * per_page, SEL_BLOCK, W)
    npk = picks.shape[2]
    picks = picks.reshape(B, A_KV_HEADS * npk)

    def blk(g):
        def index(b, k, pt, pk):
            j = pk[b, g * npk + k]
            return (pt[b, j // per_page] * per_page + j % per_page, 0, 0)
        return pl.BlockSpec((1, SEL_BLOCK, W), index)

    perb = lambda *s: pl.BlockSpec((1,) + s, lambda b, k, pt, pk: (b,) + (0,) * len(s))
    hm_spec = perb(A_KV_HEADS, HEAD_ROWS, A_HEAD_DIM)
    grid_spec = pltpu.PrefetchScalarGridSpec(
        num_scalar_prefetch=2, grid=(B, npk),
        in_specs=[blk(0), blk(1), blk(0), blk(1), hm_spec, perb(A_KV_HEADS, HEAD_ROWS, 3),
                  perb(1, W), perb(1, W), hm_spec, hm_spec],
        out_specs=hm_spec,
        scratch_shapes=[pltpu.VMEM((A_KV_HEADS, HEAD_ROWS, 1), F32), pltpu.VMEM((A_KV_HEADS, HEAD_ROWS, 1), F32),
                        pltpu.VMEM((A_KV_HEADS, HEAD_ROWS, A_HEAD_DIM), F32)])
    return pl.pallas_call(
        _nsa_sample_sel_kernel, grid_spec=grid_spec,
        out_shape=jax.ShapeDtypeStruct((B, A_KV_HEADS, HEAD_ROWS, A_HEAD_DIM), F32),
        compiler_params=_cparams("parallel", "arbitrary"), name="nsa_sample_sel",
    )(page_table, picks, pk2, pk2, pv2, pv2, q_hm, gate_hm, ks_new, vs_new, o_cmp, o_win)


def nsa_sample(a_q, a_gate, new_rows, pools, win_k, win_v, page_table, cwk, cwv):
    B = a_q.shape[0]
    past_len = page_table.shape[1] * PAGE_SIZE
    _, _, ksn, vsn, kwn, vwn = new_rows
    pool_kc, pool_vc, pool_ks, pool_vs = pools
    padj = ((0, 0), (0, 0), (0, HEAD_ROWS - A_GROUP), (0, 0))
    q_hm = jnp.pad(a_q.reshape(B, A_KV_HEADS, A_GROUP, A_HEAD_DIM), padj)
    gate_hm = jnp.pad(jnp.transpose(a_gate.reshape(B, 3, A_KV_HEADS, A_GROUP), (0, 2, 3, 1)), padj)
    hk = pool_halves(pool_kc, cwk[0])
    hv = pool_halves(pool_vc, cwv[0])
    o_cmp, o_win, picks = nsa_sample_cmp(page_table, hk, hv, q_hm, win_k, win_v, kwn, vwn, cwk, cwv, past_len)
    picks = picks[:, :A_KV_HEADS, :N_SEL - 1]
    o = nsa_sample_sel(page_table, picks, pool_ks, pool_vs, q_hm, gate_hm, ksn, vsn, o_cmp, o_win)
    return o[:, :, :A_GROUP].reshape(B, 1, A_WIDTH)


PROJ_GROUPS = (2 * M_WIDTH, M_WIDTH, M_WIDTH, 2 * M_HEADS, A_WIDTH) + (A_KV_LANES,) * 6 + (
    3 * A_HEADS, R_COLS, 3 * D_MODEL)
PROMPT_ROW_TILE = 256
MLSTM_CHUNK = 128
RWKV_STEPS = 64


def _layer_params(l, w_in, w_branch_m, w_branch_a, w_branch_r, w_out, w_gate_up, w_down,
                  nsa_cmp_pe, nsa_cmp_w1, nsa_cmp_w2):
    offs = np.cumsum((0,) + PROJ_GROUPS)
    assert offs[-1] == w_in.shape[2]
    wi = w_in[l].astype(BF16)
    return {
        'proj': [wi[:, a:b] for a, b in zip(offs[:-1], offs[1:])],
        'wm': w_branch_m[l].astype(BF16), 'wa': w_branch_a[l].astype(BF16), 'wr': w_branch_r[l].astype(BF16),
        'wo': w_out[l].astype(BF16), 'wgu': w_gate_up[l].astype(BF16), 'wd': w_down[l].astype(BF16),
        'cwk': compress_weights(nsa_cmp_pe[l, 0], nsa_cmp_w1[l, 0], nsa_cmp_w2[l, 0]),
        'cwv': compress_weights(nsa_cmp_pe[l, 1], nsa_cmp_w1[l, 1], nsa_cmp_w2[l, 1]),
    }


def _trunk_layer(x, lw, lp, nsa_fn, m_state, r_state, tm, chunk, rwkv_tm, rwkv_steps, gf, final_norm):
    B, T, D = x.shape
    x2d = x.reshape(B * T, D)
    outs = norm_proj(x2d, lp['norm1_g'], lw['proj'], tm)
    (m_qk, m_v, m_o, m_if, a_q), new_rows, (a_gate, r_cols, merge) = outs[:5], outs[5:11], outs[11:]
    b3 = lambda a: a.reshape(B, T, a.shape[-1])
    new_rows = [b3(r) for r in new_rows]
    y_m, conv_n, c_n, n_n, m_n = mlstm(b3(m_qk), b3(m_v), b3(m_o), b3(m_if), *m_state,
                                        lp['mlstm_conv_w'], lp['mlstm_conv_b'], lp['mlstm_if_bias'], chunk)
    y_a = nsa_fn(b3(a_q), b3(a_gate), new_rows)
    r_cols = b3(r_cols)
    y_r, s_n = rwkv(r_cols, r_state[0], r_state[1], lp, rwkv_tm, rwkv_steps)
    x_new = merge_ffn(x2d, y_m.reshape(B * T, -1), y_a.reshape(B * T, -1), y_r.reshape(B * T, -1), merge,
                      lw['wm'], lw['wa'], lw['wr'], lw['wo'], lp['norm2_g'], lw['wgu'], lw['wd'], gf,
                      tm, final_norm)
    return x_new.reshape(B, T, D), new_rows, (conv_n, c_n, n_n, m_n), (r_cols[:, -1:], s_n)


def kernel(x_prompt, x_sample, cache_cmp_k, cache_cmp_v, cache_slc_k, cache_slc_v, cache_win_k, cache_win_v, state_mlstm_conv, state_mlstm_C, state_mlstm_n, state_mlstm_m, state_rwkv_shift, state_rwkv, page_table, norm1_g, w_in, mlstm_if_bias, mlstm_conv_w, mlstm_conv_b, nsa_cmp_pe, nsa_cmp_w1, nsa_cmp_w2, rwkv_mu, rwkv_w0, rwkv_w2, rwkv_a0, rwkv_a2, rwkv_g2, rwkv_k_k, rwkv_k_a, rwkv_r_k, rwkv_ln_g, rwkv_ln_b, w_branch_m, w_branch_a, w_branch_r, w_out, norm2_g, w_gate_up, w_down, final_norm_g):
    bp, tp, _ = x_prompt.shape
    bs, ts, _ = x_sample.shape
    assert ts == 1
    depth = w_in.shape[0]
    kvshape = lambda a: a.reshape(a.shape[0], a.shape[1], A_KV_HEADS, A_HEAD_DIM)
    xp, xs = x_prompt, x_sample
    states_p, states_s = [], []
    for l in range(depth):
        lw = _layer_params(l, w_in, w_branch_m, w_branch_a, w_branch_r, w_out, w_gate_up, w_down,
                           nsa_cmp_pe, nsa_cmp_w1, nsa_cmp_w2)
        lp = {'norm1_g': norm1_g[l], 'mlstm_if_bias': mlstm_if_bias[l], 'mlstm_conv_w': mlstm_conv_w[l],
              'mlstm_conv_b': mlstm_conv_b[l], 'rwkv_mu': rwkv_mu[l], 'rwkv_w0': rwkv_w0[l],
              'rwkv_w2': rwkv_w2[l], 'rwkv_a0': rwkv_a0[l], 'rwkv_a2': rwkv_a2[l], 'rwkv_g2': rwkv_g2[l],
              'rwkv_k_k': rwkv_k_k[l], 'rwkv_k_a': rwkv_k_a[l], 'rwkv_r_k': rwkv_r_k[l],
              'rwkv_ln_g': rwkv_ln_g[l], 'rwkv_ln_b': rwkv_ln_b[l], 'norm2_g': norm2_g[l]}
        last = l == depth - 1

        def nsa_p(a_q, a_gate, rows):
            kc = compress_rows(rows[0], lw['cwk'])
            vc = compress_rows(rows[1], lw['cwv'])
            return nsa_prompt_attend(a_q, a_gate, kc, vc, *rows[2:])

        m_zero = (jnp.zeros((bp, M_CONV - 1, 2 * M_WIDTH), F32),
                  jnp.zeros((bp, M_HEADS, M_HEAD_DIM, M_HEAD_DIM), F32),
                  jnp.zeros((bp, M_HEADS, M_HEAD_DIM), F32), jnp.zeros((bp, M_HEADS), F32))
        r_zero = (jnp.zeros((bp, 1, R_COLS), F32), jnp.zeros((bp, R_HEADS, R_HEAD_DIM, R_HEAD_DIM), F32))
        xp, rows, m_new, r_new = _trunk_layer(xp, lw, lp, nsa_p, m_zero, r_zero, PROMPT_ROW_TILE,
                                              MLSTM_CHUNK, PROMPT_ROW_TILE, RWKV_STEPS, final_norm_g, last)
        keep = min(WINDOW, tp)
        states_p.append(tuple(kvshape(r) for r in rows[:4])
                        + (kvshape(rows[4][:, tp - keep:]), kvshape(rows[5][:, tp - keep:])) + m_new + r_new)

        lanes = lambda a: a.reshape(a.shape[0], a.shape[1], A_KV_LANES)
        pools = [lanes(c[l]) for c in (cache_cmp_k, cache_cmp_v, cache_slc_k, cache_slc_v)]
        win_k, win_v = lanes(cache_win_k[l]), lanes(cache_win_v[l])

        def nsa_s(a_q, a_gate, rows):
            return nsa_sample(a_q, a_gate, rows, pools, win_k, win_v, page_table, lw['cwk'], lw['cwv'])

        xs, rows, m_new, r_new = _trunk_layer(
            xs, lw, lp, nsa_s,
            (state_mlstm_conv[l], state_mlstm_C[l], state_mlstm_n[l], state_mlstm_m[l]),
            (state_rwkv_shift[l], state_rwkv[l]), bs * ts, SUBLANES, SUBLANES, SUBLANES, final_norm_g, last)
        keep = win_k.shape[1]
        win_new = [jnp.concatenate([w, r], axis=1)[:, -keep:] for w, r in ((win_k, rows[4]), (win_v, rows[5]))]
        states_s.append(tuple(kvshape(r) for r in rows[:4]) + tuple(kvshape(w) for w in win_new) + m_new + r_new)

    stack = lambda states: [jnp.stack(a) for a in zip(*states)]
    return (xp, xs, *stack(states_p), *stack(states_s))
```

```python
import functools
import math

import numpy as np
import jax
import jax.numpy as jnp
from jax import lax
from jax.experimental import pallas as pl
from jax.experimental.pallas import tpu as pltpu

F32 = jnp.float32
BF16 = jnp.bfloat16

D_MODEL = 1024
DEPTH = 2
PAGE_SIZE = 128
M_HEADS = 4
M_HEAD_DIM = 128
M_WIDTH = M_HEADS * M_HEAD_DIM
M_CONV = 4
A_HEADS = 8
A_KV_HEADS = 2
A_HEAD_DIM = 64
A_WIDTH = A_HEADS * A_HEAD_DIM
A_GROUP = A_HEADS // A_KV_HEADS
A_SCALE = A_HEAD_DIM ** -0.5
A_KV_LANES = A_KV_HEADS * A_HEAD_DIM
CMP_STRIDE = 16
CMP_BLOCK = 2 * CMP_STRIDE
CMP_HIDDEN = A_HEAD_DIM
SEL_BLOCK = 64
N_SEL = 16
WINDOW = 512
Q_BLOCK = 128
R_HEADS = 8
R_HEAD_DIM = 64
R_WIDTH = R_HEADS * R_HEAD_DIM
R_DECAY_RANK = 64
R_A_RANK = 64
R_GATE_RANK = 128
R_COLS = 3 * R_WIDTH + R_DECAY_RANK + R_A_RANK + R_GATE_RANK
LN_X_EPS = 64e-5
D_FF = ((-(-8 * D_MODEL // 3)) + 255) // 256 * 256
IN_SPLITS = (2 * M_WIDTH, M_WIDTH, M_WIDTH, 2 * M_HEADS,
             A_WIDTH, 6 * A_KV_HEADS * A_HEAD_DIM, 3 * A_HEADS,
             R_COLS, 3 * D_MODEL)
RMS_EPS = 1e-6
NEG = -1e30
FORCE = 1e9
LOWEST = -3e38

VMEM_LIMIT_BYTES = 56 * 1024 * 1024
SUBLANES = 8
LANES = 128


def _cparams(*sem):
    return pltpu.CompilerParams(dimension_semantics=sem, vmem_limit_bytes=VMEM_LIMIT_BYTES)


def _resident(shape):
    nd = len(shape)
    return pl.BlockSpec(shape, lambda *_: (0,) * nd, pipeline_mode=pl.Buffered(1))


def _dot(a, b):
    return jnp.dot(a.astype(BF16), b.astype(BF16), preferred_element_type=F32)


def _dot_nt(a, b):
    return lax.dot_general(a.astype(BF16), b.astype(BF16), (((1,), (1,)), ((), ())),
                           preferred_element_type=F32)


def _split3(x):
    hi = x.astype(BF16)
    r = x - hi.astype(F32)
    mid = r.astype(BF16)
    lo = (r - mid.astype(F32)).astype(BF16)
    return hi, mid, lo


def _dot_f32_rhs(a_bf16, x):
    hi, mid, lo = _split3(x)
    d = lambda p: jnp.dot(a_bf16, p, preferred_element_type=F32)
    return d(hi) + d(mid) + d(lo)


def _dot_f32_lhs(x, a_bf16):
    hi, mid, lo = _split3(x)
    d = lambda p: jnp.dot(p, a_bf16, preferred_element_type=F32)
    return d(hi) + d(mid) + d(lo)


def _transpose_f32(x, eye_bf16):
    hi, mid, lo = _split3(x)
    d = lambda p: lax.dot_general(eye_bf16, p, (((1,), (1,)), ((), ())), preferred_element_type=F32)
    return d(hi) + d(mid) + d(lo)


def _eye(n, dtype=BF16):
    r = lax.broadcasted_iota(jnp.int32, (n, n), 0)
    c = lax.broadcasted_iota(jnp.int32, (n, n), 1)
    return (r == c).astype(dtype)


def _sigmoid(x):
    return 1.0 / (1.0 + jnp.exp(-x))


def _silu(x):
    return x * _sigmoid(x)


def _softplus(x):
    return jnp.maximum(x, 0.0) + jnp.log(1.0 + jnp.exp(-jnp.abs(x)))


def _rmsnorm(x, g):
    return x * lax.rsqrt(jnp.mean(x * x, axis=-1, keepdims=True) + RMS_EPS) * g


def _norm_proj_kernel(x_ref, g_ref, *refs):
    n = len(refs) // 2
    hb = _rmsnorm(x_ref[...], g_ref[...]).astype(BF16)
    for w_ref, o_ref in zip(refs[:n], refs[n:]):
        o_ref[...] = jnp.dot(hb, w_ref[...], preferred_element_type=F32)


def norm_proj(x2d, g, weights, tm):
    m, d = x2d.shape
    assert m % tm == 0
    row = lambda i: (i, 0)
    in_specs = [pl.BlockSpec((tm, d), row), _resident((1, d))]
    in_specs += [_resident(w.shape) for w in weights]
    out_specs = [pl.BlockSpec((tm, w.shape[1]), row) for w in weights]
    out_shape = [jax.ShapeDtypeStruct((m, w.shape[1]), F32) for w in weights]
    return pl.pallas_call(
        _norm_proj_kernel, grid=(m // tm,), in_specs=in_specs, out_specs=out_specs,
        out_shape=out_shape, compiler_params=_cparams("parallel"), name="norm_proj",
    )(x2d, g.reshape(1, d), *weights)


def _merge_ffn_kernel(x_ref, ym_ref, ya_ref, yr_ref, mg_ref, wm_ref, wa_ref, wr_ref, wo_ref,
                      g2_ref, wgu_ref, wd_ref, gf_ref, o_ref, *, final_norm):
    mg = mg_ref[...]
    merged = (_sigmoid(mg[:, :D_MODEL]) * _dot(ym_ref[...], wm_ref[...])
              + _sigmoid(mg[:, D_MODEL:2 * D_MODEL]) * _dot(ya_ref[...], wa_ref[...])
              + _sigmoid(mg[:, 2 * D_MODEL:]) * _dot(yr_ref[...], wr_ref[...]))
    x1 = x_ref[...] + _dot(merged, wo_ref[...])
    gu = _dot(_rmsnorm(x1, g2_ref[...]), wgu_ref[...])
    x2 = x1 + _dot(_silu(gu[:, :D_FF]) * gu[:, D_FF:], wd_ref[...])
    o_ref[...] = _rmsnorm(x2, gf_ref[...]) if final_norm else x2


def merge_ffn(x2d, ym, ya, yr, mg, wm, wa, wr, wo, g2, wgu, wd, gf, tm, final_norm):
    m, d = x2d.shape
    row = lambda i: (i, 0)
    acts = [x2d, ym, ya, yr, mg]
    consts = [wm, wa, wr, wo, g2.reshape(1, d), wgu, wd, gf.reshape(1, d)]
    in_specs = [pl.BlockSpec((tm, a.shape[1]), row) for a in acts]
    in_specs += [_resident(c.shape) for c in consts]
    return pl.pallas_call(
        functools.partial(_merge_ffn_kernel, final_norm=final_norm),
        grid=(m // tm,), in_specs=in_specs, out_specs=pl.BlockSpec((tm, d), row),
        out_shape=jax.ShapeDtypeStruct((m, d), F32), compiler_params=_cparams("parallel"),
        name="merge_ffn",
    )(*acts, *consts)


def _mlstm_kernel(qk_ref, v_ref, o_ref, if_ref, conv0_ref, c0_ref, n0_ref, m0_ref,
                  cw_ref, cb_ref, ifb_ref,
                  y_ref, conv_out_ref, c_out_ref, n_out_ref, m_out_ref,
                  cbuf, c_scr, n_scr, m_scr, *, chunk, n_last):
    c = pl.program_id(1)
    nc = pl.num_programs(1)
    L = chunk
    pad = SUBLANES

    @pl.when(c == 0)
    def _():
        cbuf[0:pad, :] = conv0_ref[0]
        c_scr[...] = c0_ref[0]
        n_scr[...] = n0_ref[0]
        m_scr[...] = m0_ref[0]

    cbuf[pad:pad + L, :] = qk_ref[0]
    acc = cb_ref[...]
    for j in range(M_CONV):
        acc = acc + cbuf[pad - (M_CONV - 1) + j: pad - (M_CONV - 1) + j + L, :] * cw_ref[j:j + 1, :]
    qk = _silu(acc)

    @pl.when(c == nc - 1)
    def _():
        conv_out_ref[0] = cbuf[n_last:n_last + pad, :]

    cbuf[0:pad, :] = cbuf[L:L + pad, :]

    gates = if_ref[0] + ifb_ref[...]
    lane8 = lax.broadcasted_iota(jnp.int32, (L, 2 * M_HEADS), 1)
    lf_all = -_softplus(-gates)
    lif = jnp.where(lane8 < M_HEADS, gates, lf_all)
    row_id = lax.broadcasted_iota(jnp.int32, (L, 2 * M_HEADS), 0)
    is_last_chunk = c == nc - 1
    if n_last < L:
        dead = jnp.logical_and(is_last_chunk, row_id >= n_last)
        lif = jnp.where(dead, jnp.where(lane8 < M_HEADS, NEG, 0.0), lif)
    t_i = lax.broadcasted_iota(jnp.int32, (L, L), 0)
    s_i = lax.broadcasted_iota(jnp.int32, (L, L), 1)
    causal = s_i <= t_i
    tri = causal.astype(BF16)
    csum = _dot_f32_rhs(tri, lif)
    lib = jnp.where(lane8 < M_HEADS, lif, csum)
    lib_t = _transpose_f32(lib, _eye(2 * M_HEADS))
    eye_h = _eye(M_HEAD_DIM)
    last = n_last - 1

    for h in range(M_HEADS):
        sl = slice(h * M_HEAD_DIM, (h + 1) * M_HEAD_DIM)
        q = qk[:, sl]
        k = qk[:, M_WIDTH + h * M_HEAD_DIM: M_WIDTH + (h + 1) * M_HEAD_DIM] * (M_HEAD_DIM ** -0.5)
        v = v_ref[0, :, sl]
        li_col = lib[:, h:h + 1]
        b_col = lib[:, M_HEADS + h:M_HEADS + h + 1]
        li_row = lib_t[h:h + 1, :]
        b_row = lib_t[M_HEADS + h:M_HEADS + h + 1, :]
        m_prev = m_scr[0:1, h:h + 1]
        dmat = jnp.where(causal, b_col - b_row + li_row, NEG)
        inter = b_col + m_prev
        m_t = jnp.maximum(inter, jnp.max(dmat, axis=1, keepdims=True))
        w_intra = jnp.exp(dmat - m_t)
        w_inter = jnp.exp(inter - m_t)
        att = w_intra * _dot_nt(q, k)
        c_h = c_scr[h]
        n_h = n_scr[h:h + 1, :]
        num = _dot(att, v) + w_inter * _dot(q, c_h)
        den = jnp.sum(att, axis=1, keepdims=True) + w_inter * jnp.sum(q * n_h, axis=1, keepdims=True)
        hh = num / jnp.maximum(jnp.abs(den), jnp.exp(-m_t))
        y_ref[0, :, sl] = _sigmoid(o_ref[0, :, sl]) * hh

        m_new = m_t[last:last + 1, :]
        b_last = b_col[last:last + 1, :]
        scale = jnp.exp(b_last + m_prev - m_new)
        w_end = jnp.exp(b_last - b_col + li_col - m_new)
        kw = k * w_end
        kw_t = lax.dot_general(eye_h, kw.astype(BF16), (((1,), (1,)), ((), ())),
                               preferred_element_type=F32)
        c_scr[h] = scale * c_h + _dot(kw_t, v)
        n_scr[h:h + 1, :] = scale * n_h + jnp.sum(kw, axis=0, keepdims=True)
        m_scr[0:1, h:h + 1] = m_new

    @pl.when(is_last_chunk)
    def _():
        c_out_ref[0] = c_scr[...]
        n_out_ref[0] = n_scr[...]
        m_out_ref[0] = m_scr[...]


def mlstm(m_qk, m_v, m_o, m_if, conv0, c0, n0, m0, conv_w, conv_b, if_bias, chunk):
    B, T, _ = m_qk.shape
    nc = -(-T // chunk)
    tp = nc * chunk
    n_last = T - (nc - 1) * chunk
    if tp != T:
        padt = lambda a: jnp.pad(a, ((0, 0), (0, tp - T), (0, 0)))
        m_qk, m_v, m_o, m_if = padt(m_qk), padt(m_v), padt(m_o), padt(m_if)
    conv0p = jnp.pad(conv0, ((0, 0), (SUBLANES - (M_CONV - 1), 0), (0, 0)))
    n0p = jnp.pad(n0, ((0, 0), (0, SUBLANES - M_HEADS), (0, 0)))
    m0p = jnp.pad(m0, ((0, 0), (0, LANES - M_HEADS))).reshape(B, 1, LANES)
    cw = jnp.pad(conv_w, ((0, SUBLANES - M_CONV), (0, 0)))
    tile = lambda w: pl.BlockSpec((1, chunk, w), lambda b, c: (b, c, 0))
    perb = lambda *s: pl.BlockSpec((1,) + s, lambda b, c: (b,) + (0,) * len(s))
    outs = pl.pallas_call(
        functools.partial(_mlstm_kernel, chunk=chunk, n_last=n_last),
        grid=(B, nc),
        in_specs=[tile(2 * M_WIDTH), tile(M_WIDTH), tile(M_WIDTH), tile(2 * M_HEADS),
                  perb(SUBLANES, 2 * M_WIDTH), perb(M_HEADS, M_HEAD_DIM, M_HEAD_DIM),
                  perb(SUBLANES, M_HEAD_DIM), perb(1, LANES),
                  _resident((SUBLANES, 2 * M_WIDTH)), _resident((1, 2 * M_WIDTH)),
                  _resident((1, 2 * M_HEADS))],
        out_specs=[tile(M_WIDTH), perb(SUBLANES, 2 * M_WIDTH),
                   perb(M_HEADS, M_HEAD_DIM, M_HEAD_DIM), perb(SUBLANES, M_HEAD_DIM), perb(1, LANES)],
        out_shape=[jax.ShapeDtypeStruct((B, tp, M_WIDTH), F32),
                   jax.ShapeDtypeStruct((B, SUBLANES, 2 * M_WIDTH), F32),
                   jax.ShapeDtypeStruct((B, M_HEADS, M_HEAD_DIM, M_HEAD_DIM), F32),
                   jax.ShapeDtypeStruct((B, SUBLANES, M_HEAD_DIM), F32),
                   jax.ShapeDtypeStruct((B, 1, LANES), F32)],
        scratch_shapes=[pltpu.VMEM((chunk + SUBLANES, 2 * M_WIDTH), F32),
                        pltpu.VMEM((M_HEADS, M_HEAD_DIM, M_HEAD_DIM), F32),
                        pltpu.VMEM((SUBLANES, M_HEAD_DIM), F32),
                        pltpu.VMEM((1, LANES), F32)],
        compiler_params=_cparams("parallel", "arbitrary"), name="mlstm",
    )(m_qk, m_v, m_o, m_if, conv0p, c0, n0p, m0p, cw, conv_b.reshape(1, -1), if_bias.reshape(1, -1))
    y, conv_o, c_o, n_o, m_o_ = outs
    return (y[:, :T], conv_o[:, SUBLANES - (M_CONV - 1):], c_o, n_o[:, :M_HEADS], m_o_[:, 0, :M_HEADS])


def _rwkv_prep_kernel(cols_ref, shift0_ref, mu_ref, w0_ref, w2_ref, a0_ref, a2_ref, g2_ref,
                      kk_ref, ka_ref, rk_ref,
                      r_o, w_o, k_o, v_o, a_o, b_o, g_o, bonus_o, carry):
    t = pl.program_id(1)

    @pl.when(t == 0)
    def _():
        carry[...] = shift0_ref[0]

    cols = cols_ref[0]
    tm = cols.shape[0]
    row = lax.broadcasted_iota(jnp.int32, cols.shape, 0)
    prev = jnp.where(row == 0, carry[...], pltpu.roll(cols, 1, 0))
    carry[...] = cols[tm - 1:tm, :]
    mixed = cols + (prev - cols) * mu_ref[...]
    W = R_WIDTH
    r = mixed[:, :W]
    k = mixed[:, W:2 * W]
    v = mixed[:, 2 * W:3 * W]
    o = 3 * W
    wd = mixed[:, o:o + R_DECAY_RANK]
    ad = mixed[:, o + R_DECAY_RANK:o + R_DECAY_RANK + R_A_RANK]
    gd = mixed[:, o + R_DECAY_RANK + R_A_RANK:]
    w = -_softplus(-(w0_ref[...] + _dot(jnp.tanh(wd), w2_ref[...]))) - 0.5
    decay = jnp.exp(-jnp.exp(w))
    a = _sigmoid(a0_ref[...] + _dot(ad, a2_ref[...]))
    g_o[0] = _dot(_sigmoid(gd), g2_ref[...])
    kk = k * kk_ref[...]
    k2 = k * (1.0 + (a - 1.0) * ka_ref[...])
    rkk = r * k2 * rk_ref[...]
    for h in range(R_HEADS):
        sl = slice(h * R_HEAD_DIM, (h + 1) * R_HEAD_DIM)
        kkh = kk[:, sl]
        kkh = kkh * lax.rsqrt(jnp.maximum(jnp.sum(kkh * kkh, axis=1, keepdims=True), 1e-24))
        r_o[0, h] = r[:, sl]
        w_o[0, h] = decay[:, sl]
        k_o[0, h] = k2[:, sl]
        v_o[0, h] = v[:, sl]
        a_o[0, h] = -kkh
        b_o[0, h] = kkh * a[:, sl]
        bonus_o[0, :, sl] = jnp.sum(rkk[:, sl], axis=1, keepdims=True) * v[:, sl]


def _rwkv_scan_kernel(r_ref, w_ref, k_ref, v_ref, a_ref, b_ref, s0_ref, y_ref, s_out_ref, s_scr,
                      *, steps, n_valid):
    t = pl.program_id(1)
    nt = pl.num_programs(1)
    G = SUBLANES
    N = R_HEAD_DIM

    @pl.when(t == 0)
    def _():
        s_scr[...] = s0_ref[0]

    H = R_HEADS
    eye_n = _eye(N)
    eye_g = _eye(G)
    lane_g = lax.broadcasted_iota(jnp.int32, (H, N, G), 2)
    row_g = lax.broadcasted_iota(jnp.int32, (H, G, N), 1)

    def group(gi, carry):
        off = pl.multiple_of(gi * G, G)
        ld = lambda ref: ref[0, :, pl.ds(off, G), :]
        rb, wb, kb, vb, ab, bb = ld(r_ref), ld(w_ref), ld(k_ref), ld(v_ref), ld(a_ref), ld(b_ref)
        if n_valid % G:
            dead = (t * steps + off + row_g) >= n_valid
            wb = jnp.where(dead, 1.0, wb)
            kb = jnp.where(dead, 0.0, kb)
            vb = jnp.where(dead, 0.0, vb)
            ab = jnp.where(dead, 0.0, ab)
            bb = jnp.where(dead, 0.0, bb)
        v_t = jnp.stack([_transpose_f32(vb[h], eye_n) for h in range(H)])
        s = s_scr[...]
        yc = jnp.zeros((H, N, G), F32)
        for u in range(G):
            sa = jnp.sum(s * ab[:, u:u + 1, :], axis=2, keepdims=True)
            s = s * wb[:, u:u + 1, :] + sa * bb[:, u:u + 1, :] + v_t[:, :, u:u + 1] * kb[:, u:u + 1, :]
            y = jnp.sum(s * rb[:, u:u + 1, :], axis=2, keepdims=True)
            yc = jnp.where(lane_g == u, y, yc)
        s_scr[...] = s
        for h in range(H):
            y_ref[0, h, pl.ds(off, G), :] = _transpose_f32(yc[h], eye_g)
        return carry

    lax.fori_loop(0, steps // G, group, 0)

    @pl.when(t == nt - 1)
    def _():
        s_out_ref[0] = s_scr[...]


def _rwkv_post_kernel(y_ref, bonus_ref, g_ref, lng_ref, lnb_ref, o_ref):
    for h in range(R_HEADS):
        sl = slice(h * R_HEAD_DIM, (h + 1) * R_HEAD_DIM)
        y = y_ref[0, h]
        mean = jnp.mean(y, axis=1, keepdims=True)
        yc = y - mean
        var = jnp.mean(yc * yc, axis=1, keepdims=True)
        yn = yc * lax.rsqrt(var + LN_X_EPS) * lng_ref[:, sl] + lnb_ref[:, sl]
        o_ref[0, :, sl] = (yn + bonus_ref[0, :, sl]) * g_ref[0, :, sl]


def rwkv(cols, shift0, s0, p, tm, steps):
    B, T, _ = cols.shape
    tp = -(-T // tm) * tm
    assert tm % steps == 0 and steps % SUBLANES == 0
    colsp = jnp.pad(cols, ((0, 0), (0, tp - T), (0, 0))) if tp != T else cols
    nt = tp // tm
    vec = lambda a: a.reshape(1, -1)
    consts = [vec(p['rwkv_mu']), vec(p['rwkv_w0']), p['rwkv_w2'].astype(BF16), vec(p['rwkv_a0']),
              p['rwkv_a2'].astype(BF16), p['rwkv_g2'].astype(BF16), vec(p['rwkv_k_k']),
              vec(p['rwkv_k_a']), vec(p['rwkv_r_k'])]
    hm = jax.ShapeDtypeStruct((B, R_HEADS, tp, R_HEAD_DIM), F32)
    fl = jax.ShapeDtypeStruct((B, tp, R_WIDTH), F32)
    hm_spec = lambda n: pl.BlockSpec((1, R_HEADS, n, R_HEAD_DIM), lambda b, t: (b, 0, t, 0))
    fl_spec = lambda n, w: pl.BlockSpec((1, n, w), lambda b, t: (b, t, 0))
    r, w, k, v, a, bb, g, bonus = pl.pallas_call(
        _rwkv_prep_kernel, grid=(B, nt),
        in_specs=[fl_spec(tm, R_COLS), pl.BlockSpec((1, 1, R_COLS), lambda b, t: (b, 0, 0))]
                 + [_resident(c.shape) for c in consts],
        out_specs=[hm_spec(tm)] * 6 + [fl_spec(tm, R_WIDTH)] * 2,
        out_shape=[hm] * 6 + [fl] * 2,
        scratch_shapes=[pltpu.VMEM((1, R_COLS), F32)],
        compiler_params=_cparams("parallel", "arbitrary"), name="rwkv_prep",
    )(colsp, shift0, *consts)
    st_spec = pl.BlockSpec((1, R_HEADS, R_HEAD_DIM, R_HEAD_DIM), lambda b, t: (b, 0, 0, 0))
    y, s_new = pl.pallas_call(
        functools.partial(_rwkv_scan_kernel, steps=steps, n_valid=T),
        grid=(B, tp // steps),
        in_specs=[hm_spec(steps)] * 6 + [st_spec],
        out_specs=[hm_spec(steps), st_spec],
        out_shape=[hm, jax.ShapeDtypeStruct(s0.shape, F32)],
        scratch_shapes=[pltpu.VMEM((R_HEADS, R_HEAD_DIM, R_HEAD_DIM), F32)],
        compiler_params=_cparams("parallel", "arbitrary"), name="rwkv_scan",
    )(r, w, k, v, a, bb, s0)
    out = pl.pallas_call(
        _rwkv_post_kernel, grid=(B, nt),
        in_specs=[hm_spec(tm), fl_spec(tm, R_WIDTH), fl_spec(tm, R_WIDTH),
                  _resident((1, R_WIDTH)), _resident((1, R_WIDTH))],
        out_specs=fl_spec(tm, R_WIDTH), out_shape=fl,
        compiler_params=_cparams("parallel", "parallel"), name="rwkv_post",
    )(y, bonus, g, vec(p['rwkv_ln_g']), vec(p['rwkv_ln_b']))
    return out[:, :T], s_new


def compress_weights(pe, w1, w2):
    d, hdim, G = A_HEAD_DIM, CMP_HIDDEN, A_KV_HEADS
    w1r = w1.reshape(2, CMP_STRIDE, d, hdim)
    wl = jnp.zeros((CMP_STRIDE, G, d, 2, G, hdim), F32)
    for g in range(G):
        wl = wl.at[:, g, :, :, g, :].set(jnp.transpose(w1r, (1, 2, 0, 3)))
    wl = wl.reshape(CMP_STRIDE, G * d, 2 * G * hdim).astype(BF16)
    wpe = jnp.transpose(w1r, (1, 0, 2, 3)).reshape(CMP_STRIDE, 2 * d, hdim)
    wpe = jnp.concatenate([wpe] * G, axis=-1).astype(BF16)
    pe_l = jnp.transpose(pe.reshape(2, CMP_STRIDE, d), (1, 0, 2)).reshape(CMP_STRIDE, 1, 2 * d)
    pe_l = jnp.broadcast_to(pe_l, (CMP_STRIDE, SUBLANES, 2 * d))
    w2bd = jnp.zeros((G, hdim, G, d), F32)
    for g in range(G):
        w2bd = w2bd.at[g, :, g, :].set(w2)
    return wl, wpe, pe_l, w2bd.reshape(G * hdim, G * d).astype(BF16)


def _compress_halves(strided_rows, wl_ref, n):
    acc = None
    for l in range(CMP_STRIDE):
        part = _dot(strided_rows(l), wl_ref[l])
        acc = part if acc is None else acc + part
    return acc


def _compress_finish(halves, wpe_ref, pe_ref, w2_ref):
    n = halves.shape[0]
    gh = A_KV_HEADS * CMP_HIDDEN
    pe_term = None
    for l in range(CMP_STRIDE):
        part = _dot(pe_ref[l], wpe_ref[l])
        pe_term = part if pe_term is None else pe_term + part
    second = pltpu.roll(halves[:, gh:], n - 1, 0)
    hid = _silu(halves[:, :gh] + second + pe_term[0:1, :])
    return _dot(hid, w2_ref[...])


def _compress_kernel(x_ref, wl_ref, wpe_ref, pe_ref, w2_ref, o_ref):
    n = o_ref.shape[1]
    rows = lambda l: x_ref[0, pl.ds(l, n, stride=CMP_STRIDE), :]
    o_ref[0] = _compress_finish(_compress_halves(rows, wl_ref, n), wpe_ref, pe_ref, w2_ref)


def compress_rows(rows, cw):
    B, T, W = rows.shape
    n = T // CMP_STRIDE
    return pl.pallas_call(
        _compress_kernel, grid=(B,),
        in_specs=[pl.BlockSpec((1, T, W), lambda b: (b, 0, 0))] + [_resident(c.shape) for c in cw],
        out_specs=pl.BlockSpec((1, n, W), lambda b: (b, 0, 0)),
        out_shape=jax.ShapeDtypeStruct((B, n, W), F32),
        compiler_params=_cparams("parallel"), name="nsa_compress",
    )(rows, *cw)


def _topk_mask(rank, k):
    lane = lax.broadcasted_iota(jnp.int32, rank.shape, 1)
    sel = jnp.zeros(rank.shape, F32)
    for _ in range(k):
        hit = lane == jnp.argmax(rank, axis=1, keepdims=True).astype(jnp.int32)
        sel = jnp.where(hit, 1.0, sel)
        rank = jnp.where(hit, LOWEST, rank)
    return sel


def _flash_update(state, s, bias, v):
    m_old, l_old, acc = state
    s = s + bias
    m_new = jnp.maximum(m_old, jnp.max(s, axis=1, keepdims=True))
    p = jnp.exp(s - m_new)
    alpha = jnp.exp(m_old - m_new)
    return (m_new, alpha * l_old + jnp.sum(p, axis=1, keepdims=True), alpha * acc + _dot(p, v))


def _nsa_prompt_kernel(q_ref, gate_ref, kc_ref, vc_ref, ks_ref, vs_ref, kw_ref, vw_ref, cov_ref,
                       o_ref, *, seq, tk):
    i = pl.program_id(1)
    QB, d, J = Q_BLOCK, A_HEAD_DIM, A_GROUP
    nc = kc_ref.shape[1]
    nb = cov_ref.shape[1]
    st = i * QB
    q = q_ref[0] * A_SCALE
    gate = _sigmoid(gate_ref[0])
    rows = J * QB
    t_q = st + lax.broadcasted_iota(jnp.int32, (QB, 1), 0)
    t_rows = jnp.concatenate([t_q] * J, axis=0)
    blk_end = lax.broadcasted_iota(jnp.int32, (rows, nc), 1) * CMP_STRIDE + (CMP_BLOCK - 1)
    cmp_mask = blk_end <= t_rows
    jj = lax.broadcasted_iota(jnp.int32, (QB, nb), 1)
    cur = t_q // SEL_BLOCK
    forced = (jj == 0) | (jj == cur) | (jj == cur - 1)
    zeros = lambda w: jnp.zeros((rows, w), F32)

    qgs, o_cmps, ranks = [], [], []
    for g in range(A_KV_HEADS):
        gl = slice(g * d, (g + 1) * d)
        qg = jnp.concatenate([q[:, (g * J + j) * d:(g * J + j + 1) * d] for j in range(J)],
                             axis=0).astype(BF16)
        s = jnp.where(cmp_mask, _dot_nt(qg, kc_ref[0, :, gl]), NEG)
        e = jnp.exp(s - jnp.max(s, axis=1, keepdims=True))
        p = jnp.where(cmp_mask, e / jnp.sum(e, axis=1, keepdims=True), 0.0)
        p_grp = p[0:QB]
        for j in range(1, J):
            p_grp = p_grp + p[j * QB:(j + 1) * QB]
        score = _dot_f32_lhs(p_grp, cov_ref[...])
        qgs.append(qg)
        o_cmps.append(_dot(p, vc_ref[0, :, gl]))
        ranks.append(jnp.where(jj <= cur, jnp.where(forced, FORCE, score), NEG))
    sel_all = _topk_mask(jnp.concatenate(ranks, axis=0), min(N_SEL, nb)).astype(BF16)

    for g in range(A_KV_HEADS):
        gl = slice(g * d, (g + 1) * d)
        qg, o_cmp = qgs[g], o_cmps[g]
        sel = sel_all[g * QB:(g + 1) * QB]

        def sel_tile(kt, state):
            k0 = pl.multiple_of(kt * tk, tk)
            s_pos = k0 + lax.broadcasted_iota(jnp.int32, (1, tk), 1)
            blk_of = k0 // SEL_BLOCK + lax.broadcasted_iota(jnp.int32, (nb, tk), 1) // SEL_BLOCK
            expand = (lax.broadcasted_iota(jnp.int32, (nb, tk), 0) == blk_of).astype(BF16)
            chosen = jnp.dot(sel, expand, preferred_element_type=F32) > 0.5
            bias = jnp.where(chosen & (s_pos <= t_q), 0.0, NEG)
            sc = _dot_nt(qg, ks_ref[0, pl.ds(k0, tk), gl])
            return _flash_update(state, sc, jnp.concatenate([bias] * J, axis=0), vs_ref[0, pl.ds(k0, tk), gl])

        init = (jnp.full((rows, 1), NEG, F32), jnp.zeros((rows, 1), F32), zeros(d))
        _, l_sel, acc_sel = lax.fori_loop(0, (st + QB - 1) // tk + 1, sel_tile, init)
        o_sel = acc_sel / l_sel

        def win_tile(kt, state):
            k0 = pl.multiple_of(kt * QB, QB)
            rel = t_q - (k0 + lax.broadcasted_iota(jnp.int32, (1, QB), 1))
            bias = jnp.where((rel >= 0) & (rel < WINDOW), 0.0, NEG)
            sc = _dot_nt(qg, kw_ref[0, pl.ds(k0, QB), gl])
            return _flash_update(state, sc, jnp.concatenate([bias] * J, axis=0), vw_ref[0, pl.ds(k0, QB), gl])

        _, l_win, acc_win = lax.fori_loop(jnp.maximum(i - WINDOW // QB, 0), i + 1, win_tile, init)
        o_win = acc_win / l_win

        for j in range(J):
            h = g * J + j
            rs = slice(j * QB, (j + 1) * QB)
            o_ref[0, :, h * d:(h + 1) * d] = (gate[:, h:h + 1] * o_cmp[rs]
                                              + gate[:, A_HEADS + h:A_HEADS + h + 1] * o_sel[rs]
                                              + gate[:, 2 * A_HEADS + h:2 * A_HEADS + h + 1] * o_win[rs])


def _covers(n_cmp, n_blocks):
    i = np.arange(n_cmp)[:, None]
    j = np.arange(n_blocks)[None, :]
    return (i * CMP_STRIDE < (j + 1) * SEL_BLOCK) & (i * CMP_STRIDE + CMP_BLOCK > j * SEL_BLOCK)


def nsa_prompt_attend(a_q, a_gate, kc, vc, ks, vs, kw, vw):
    B, T, _ = a_q.shape
    nc = kc.shape[1]
    nb = -(-T // SEL_BLOCK)
    tk = min(4 * Q_BLOCK, T)
    cov = jnp.asarray(_covers(nc, nb), BF16)
    tile = lambda w: pl.BlockSpec((1, Q_BLOCK, w), lambda b, i: (b, i, 0))
    full = lambda n, w: pl.BlockSpec((1, n, w), lambda b, i: (b, 0, 0))
    W = A_KV_LANES
    return pl.pallas_call(
        functools.partial(_nsa_prompt_kernel, seq=T, tk=tk),
        grid=(B, T // Q_BLOCK),
        in_specs=[tile(A_WIDTH), tile(3 * A_HEADS), full(nc, W), full(nc, W),
                  full(T, W), full(T, W), full(T, W), full(T, W), _resident(cov.shape)],
        out_specs=tile(A_WIDTH),
        out_shape=jax.ShapeDtypeStruct((B, T, A_WIDTH), F32),
        compiler_params=_cparams("parallel", "arbitrary"), name="nsa_prompt",
    )(a_q, a_gate, kc, vc, ks, vs, kw, vw, cov)


POOL_TILE_PAGES = 64
GATHER_PAGES = 16
CHUNKS_PER_PAGE = PAGE_SIZE // CMP_STRIDE
HEAD_ROWS = SUBLANES


def channel_major(cache):
    L, n, tok, G, d = cache.shape
    return jnp.transpose(cache, (0, 1, 3, 4, 2)).reshape(L, n, G * d, tok)


def _pool_halves_kernel(x_ref, wl_ref, o_ref, xs):
    pages, _, page = x_ref.shape[1:]

    def to_token_major(p, carry):
        xs[pl.ds(pl.multiple_of(p * page, page), page), :] = x_ref[0, p].T
        return carry

    lax.fori_loop(0, pages, to_token_major, 0)
    n = o_ref.shape[1]
    o_ref[0] = _compress_halves(lambda l: xs[pl.ds(l, n, stride=CMP_STRIDE), :], wl_ref, n)


def pool_halves(pool_cm, layer, wl):
    _, n_pool, W, page = pool_cm.shape
    pt = POOL_TILE_PAGES
    assert n_pool % pt == 0
    nt = n_pool // pt
    hw = wl.shape[2]
    out = pl.pallas_call(
        _pool_halves_kernel, grid=(nt,),
        in_specs=[pl.BlockSpec((1, pt, W, page), lambda i: (layer, i, 0, 0)), _resident(wl.shape)],
        out_specs=pl.BlockSpec((1, pt * CHUNKS_PER_PAGE, hw), lambda i: (i, 0, 0)),
        out_shape=jax.ShapeDtypeStruct((nt, pt * CHUNKS_PER_PAGE, hw), F32),
        scratch_shapes=[pltpu.VMEM((pt * page, W), F32)],
        compiler_params=_cparams("parallel"), name="nsa_pool_halves",
    )(pool_cm, wl)
    return out.reshape(n_pool, CHUNKS_PER_PAGE, hw)


def _nsa_sample_cmp_kernel(pt_ref, *refs, past_len):
    del pt_ref
    ng = GATHER_PAGES
    hk_refs, hv_refs = refs[:ng], refs[ng:2 * ng]
    (q_ref, wink_ref, winv_ref, kwn_ref, vwn_ref, wpek_ref, pek_ref, w2k_ref,
     wpev_ref, pev_ref, w2v_ref, cov_ref, ocmp_ref, owin_ref, idx_ref, hk_scr, hv_scr) = refs[2 * ng:]
    c = pl.program_id(1)
    rows_per_step = ng * CHUNKS_PER_PAGE
    base = pl.multiple_of(c * rows_per_step, rows_per_step)
    for i in range(ng):
        hk_scr[pl.ds(base + i * CHUNKS_PER_PAGE, CHUNKS_PER_PAGE), :] = hk_refs[i][0]
        hv_scr[pl.ds(base + i * CHUNKS_PER_PAGE, CHUNKS_PER_PAGE), :] = hv_refs[i][0]

    @pl.when(c == pl.num_programs(1) - 1)
    def _():
        d, R8 = A_HEAD_DIM, HEAD_ROWS
        kc = _compress_finish(hk_scr[...], wpek_ref, pek_ref, w2k_ref)
        vc = _compress_finish(hv_scr[...], wpev_ref, pev_ref, w2v_ref)
        nc = kc.shape[0]
        nb = cov_ref.shape[1]
        keep = wink_ref.shape[3]
        q_pos = past_len
        blk_end = lax.broadcasted_iota(jnp.int32, (R8, nc), 1) * CMP_STRIDE + (CMP_BLOCK - 1)
        cmp_mask = (blk_end <= q_pos) & (lax.broadcasted_iota(jnp.int32, (R8, nc), 1) < nc - 1)
        real_head = lax.broadcasted_iota(jnp.int32, (R8, nc), 0) < A_GROUP
        row8 = lax.broadcasted_iota(jnp.int32, (R8, nb), 0)
        jj = lax.broadcasted_iota(jnp.int32, (R8, nb), 1)
        cur = q_pos // SEL_BLOCK
        forced = (jj == 0) | (jj == cur) | (jj == cur - 1)
        k_idx = lax.broadcasted_iota(jnp.int32, (R8, keep), 1)
        rel = q_pos - (past_len - keep + k_idx)
        win_mask = (rel >= 0) & (rel < WINDOW)
        rank = None
        for g in range(A_KV_HEADS):
            gl = slice(g * d, (g + 1) * d)
            qg = q_ref[0, g] * A_SCALE
            s = jnp.where(cmp_mask, _dot_nt(qg, kc[:, gl]), NEG)
            e = jnp.exp(s - jnp.max(s, axis=1, keepdims=True))
            p = jnp.where(cmp_mask, e / jnp.sum(e, axis=1, keepdims=True), 0.0)
            ocmp_ref[0, g] = _dot(p, vc[:, gl])
            p_grp = jnp.sum(jnp.where(real_head, p, 0.0), axis=0, keepdims=True)
            score = _dot_f32_lhs(jnp.broadcast_to(p_grp, (R8, nc)), cov_ref[...])
            rank_g = jnp.where(jj <= cur, jnp.where(forced, FORCE, score), NEG)
            rank = rank_g if rank is None else jnp.where(row8 == g, rank_g, rank)
            sw = jnp.where(win_mask, _dot(qg, wink_ref[0, 0, gl, :]), NEG)
            s_new = jnp.sum(qg * kwn_ref[0, :, gl], axis=1, keepdims=True)
            m = jnp.maximum(jnp.max(sw, axis=1, keepdims=True), s_new)
            ew = jnp.where(win_mask, jnp.exp(sw - m), 0.0)
            e_new = jnp.exp(s_new - m)
            owin_ref[0, g] = ((_dot_nt(ew, winv_ref[0, 0, gl, :]) + e_new * vwn_ref[0, :, gl])
                              / (jnp.sum(ew, axis=1, keepdims=True) + e_new))
        out_lane = lax.broadcasted_iota(jnp.int32, (R8, LANES), 1)
        picks = jnp.zeros((R8, LANES), jnp.int32)
        for it in range(N_SEL - 1):
            idx = jnp.argmax(rank, axis=1, keepdims=True).astype(jnp.int32)
            picks = jnp.where(out_lane == it, idx, picks)
            rank = jnp.where(jj == idx, LOWEST, rank)
        idx_ref[0] = picks


def nsa_sample_cmp(page_table, hk_pool, hv_pool, q_hm, win_k, win_v, layer, kw_new, vw_new, cwk, cwv, past_len):
    B, n_pages = page_table.shape
    ng = GATHER_PAGES
    assert n_pages % ng == 0
    nc = n_pages * CHUNKS_PER_PAGE
    hw = hk_pool.shape[2]
    nb = past_len // SEL_BLOCK
    cov = jnp.asarray(_covers(nc, nb), BF16)
    keep = win_k.shape[3]
    W = A_KV_LANES
    page_spec = lambda i: pl.BlockSpec((1, CHUNKS_PER_PAGE, hw), lambda b, c, pt: (pt[b, c * ng + i], 0, 0))
    perb = lambda *s: pl.BlockSpec((1,) + s, lambda b, c, pt: (b,) + (0,) * len(s))
    win_spec = pl.BlockSpec((1, 1, W, keep), lambda b, c, pt: (layer, b, 0, 0))
    const = lambda a: pl.BlockSpec(a.shape, lambda b, c, pt: (0,) * a.ndim)
    _, wpek, pek, w2k = cwk
    _, wpev, pev, w2v = cwv
    consts = [wpek, pek, w2k, wpev, pev, w2v, cov]
    hm = jax.ShapeDtypeStruct((B, A_KV_HEADS, HEAD_ROWS, A_HEAD_DIM), F32)
    grid_spec = pltpu.PrefetchScalarGridSpec(
        num_scalar_prefetch=1, grid=(B, n_pages // ng),
        in_specs=[page_spec(i) for i in range(ng)] * 2
                 + [perb(A_KV_HEADS, HEAD_ROWS, A_HEAD_DIM), win_spec, win_spec, perb(1, W), perb(1, W)]
                 + [const(a) for a in consts],
        out_specs=[perb(A_KV_HEADS, HEAD_ROWS, A_HEAD_DIM), perb(A_KV_HEADS, HEAD_ROWS, A_HEAD_DIM),
                   perb(HEAD_ROWS, LANES)],
        scratch_shapes=[pltpu.VMEM((nc, hw), F32), pltpu.VMEM((nc, hw), F32)])
    return pl.pallas_call(
        functools.partial(_nsa_sample_cmp_kernel, past_len=past_len), grid_spec=grid_spec,
        out_shape=[hm, hm, jax.ShapeDtypeStruct((B, HEAD_ROWS, LANES), jnp.int32)],
        compiler_params=_cparams("parallel", "arbitrary"), name="nsa_sample_cmp",
    )(page_table, *([hk_pool] * ng), *([hv_pool] * ng), q_hm, win_k, win_v, kw_new, vw_new, *consts)


def _nsa_sample_sel_kernel(pt_ref, pk_ref, k0_ref, k1_ref, v0_ref, v1_ref, q_ref, gate_ref,
                           ksn_ref, vsn_ref, ocmp_ref, owin_ref, o_ref, m_scr, l_scr, acc_scr):
    del pt_ref
    b = pl.program_id(0)
    k = pl.program_id(1)
    npk = pl.num_programs(1)
    d = A_HEAD_DIM
    k_refs, v_refs = (k0_ref, k1_ref), (v0_ref, v1_ref)
    page = k0_ref.shape[3]
    tok_blk = lax.broadcasted_iota(jnp.int32, (HEAD_ROWS, page), 1) // SEL_BLOCK
    for g in range(A_KV_HEADS):
        gl = slice(g * d, (g + 1) * d)
        qg = q_ref[0, g] * A_SCALE

        @pl.when(k == 0)
        def _():
            m_scr[g] = jnp.sum(qg * ksn_ref[0, :, gl], axis=1, keepdims=True)
            l_scr[g] = jnp.ones((HEAD_ROWS, 1), F32)
            acc_scr[g] = jnp.broadcast_to(vsn_ref[0, :, gl], (HEAD_ROWS, d))

        chosen = tok_blk == pk_ref[b, g * npk + k] % (page // SEL_BLOCK)
        s = jnp.where(chosen, _dot(qg, k_refs[g][0, 0, gl, :]), NEG)
        m_old = m_scr[g]
        m_new = jnp.maximum(m_old, jnp.max(s, axis=1, keepdims=True))
        p = jnp.where(chosen, jnp.exp(s - m_new), 0.0)
        alpha = jnp.exp(m_old - m_new)
        m_scr[g] = m_new
        l_scr[g] = alpha * l_scr[g] + jnp.sum(p, axis=1, keepdims=True)
        acc_scr[g] = alpha * acc_scr[g] + _dot_nt(p, v_refs[g][0, 0, gl, :])

        @pl.when(k == pl.num_programs(1) - 1)
        def _():
            gate = _sigmoid(gate_ref[0, g])
            o_ref[0, g] = (gate[:, 0:1] * ocmp_ref[0, g] + gate[:, 1:2] * (acc_scr[g] / l_scr[g])
                           + gate[:, 2:3] * owin_ref[0, g])


def nsa_sample_sel(page_table, picks, pool_k, pool_v, layer, q_hm, gate_hm, ks_new, vs_new, o_cmp, o_win):
    B = page_table.shape[0]
    _, n_pool, W, page = pool_k.shape
    per_page = page // SEL_BLOCK
    npk = picks.shape[2]
    picks = picks.reshape(B, A_KV_HEADS * npk)

    def blk(g):
        def index(b, k, pt, pk):
            return (layer, pt[b, pk[b, g * npk + k] // per_page], 0, 0)
        return pl.BlockSpec((1, 1, W, page), index)

    perb = lambda *s: pl.BlockSpec((1,) + s, lambda b, k, pt, pk: (b,) + (0,) * len(s))
    hm_spec = perb(A_KV_HEADS, HEAD_ROWS, A_HEAD_DIM)
    grid_spec = pltpu.PrefetchScalarGridSpec(
        num_scalar_prefetch=2, grid=(B, npk),
        in_specs=[blk(0), blk(1), blk(0), blk(1), hm_spec, perb(A_KV_HEADS, HEAD_ROWS, 3),
                  perb(1, W), perb(1, W), hm_spec, hm_spec],
        out_specs=hm_spec,
        scratch_shapes=[pltpu.VMEM((A_KV_HEADS, HEAD_ROWS, 1), F32), pltpu.VMEM((A_KV_HEADS, HEAD_ROWS, 1), F32),
                        pltpu.VMEM((A_KV_HEADS, HEAD_ROWS, A_HEAD_DIM), F32)])
    return pl.pallas_call(
        _nsa_sample_sel_kernel, grid_spec=grid_spec,
        out_shape=jax.ShapeDtypeStruct((B, A_KV_HEADS, HEAD_ROWS, A_HEAD_DIM), F32),
        compiler_params=_cparams("parallel", "arbitrary"), name="nsa_sample_sel",
    )(page_table, picks, pool_k, pool_k, pool_v, pool_v, q_hm, gate_hm, ks_new, vs_new, o_cmp, o_win)


def nsa_sample(a_q, a_gate, new_rows, pools, win_k, win_v, layer, page_table, cwk, cwv):
    B = a_q.shape[0]
    past_len = page_table.shape[1] * PAGE_SIZE
    _, _, ksn, vsn, kwn, vwn = new_rows
    pool_kc, pool_vc, pool_ks, pool_vs = pools
    padj = ((0, 0), (0, 0), (0, HEAD_ROWS - A_GROUP), (0, 0))
    q_hm = jnp.pad(a_q.reshape(B, A_KV_HEADS, A_GROUP, A_HEAD_DIM), padj)
    gate_hm = jnp.pad(jnp.transpose(a_gate.reshape(B, 3, A_KV_HEADS, A_GROUP), (0, 2, 3, 1)), padj)
    hk = pool_halves(pool_kc, layer, cwk[0])
    hv = pool_halves(pool_vc, layer, cwv[0])
    o_cmp, o_win, picks = nsa_sample_cmp(page_table, hk, hv, q_hm, win_k, win_v, layer, kwn, vwn, cwk, cwv,
                                         past_len)
    picks = picks[:, :A_KV_HEADS, :N_SEL - 1]
    o = nsa_sample_sel(page_table, picks, pool_ks, pool_vs, layer, q_hm, gate_hm, ksn, vsn, o_cmp, o_win)
    return o[:, :, :A_GROUP].reshape(B, 1, A_WIDTH)


PROJ_GROUPS = (2 * M_WIDTH, M_WIDTH, M_WIDTH, 2 * M_HEADS, A_WIDTH) + (A_KV_LANES,) * 6 + (
    3 * A_HEADS, R_COLS, 3 * D_MODEL)
PROMPT_ROW_TILE = 256
MLSTM_CHUNK = 128
RWKV_STEPS = 64


def _layer_params(l, w_in, w_branch_m, w_branch_a, w_branch_r, w_out, w_gate_up, w_down,
                  nsa_cmp_pe, nsa_cmp_w1, nsa_cmp_w2):
    offs = np.cumsum((0,) + PROJ_GROUPS)
    assert offs[-1] == w_in.shape[2]
    wi = w_in[l].astype(BF16)
    return {
        'proj': [wi[:, a:b] for a, b in zip(offs[:-1], offs[1:])],
        'wm': w_branch_m[l].astype(BF16), 'wa': w_branch_a[l].astype(BF16), 'wr': w_branch_r[l].astype(BF16),
        'wo': w_out[l].astype(BF16), 'wgu': w_gate_up[l].astype(BF16), 'wd': w_down[l].astype(BF16),
        'cwk': compress_weights(nsa_cmp_pe[l, 0], nsa_cmp_w1[l, 0], nsa_cmp_w2[l, 0]),
        'cwv': compress_weights(nsa_cmp_pe[l, 1], nsa_cmp_w1[l, 1], nsa_cmp_w2[l, 1]),
    }


def _trunk_layer(x, lw, lp, nsa_fn, m_state, r_state, tm, chunk, rwkv_tm, rwkv_steps, gf, final_norm):
    B, T, D = x.shape
    x2d = x.reshape(B * T, D)
    outs = norm_proj(x2d, lp['norm1_g'], lw['proj'], tm)
    (m_qk, m_v, m_o, m_if, a_q), new_rows, (a_gate, r_cols, merge) = outs[:5], outs[5:11], outs[11:]
    b3 = lambda a: a.reshape(B, T, a.shape[-1])
    new_rows = [b3(r) for r in new_rows]
    y_m, conv_n, c_n, n_n, m_n = mlstm(b3(m_qk), b3(m_v), b3(m_o), b3(m_if), *m_state,
                                        lp['mlstm_conv_w'], lp['mlstm_conv_b'], lp['mlstm_if_bias'], chunk)
    y_a = nsa_fn(b3(a_q), b3(a_gate), new_rows)
    r_cols = b3(r_cols)
    y_r, s_n = rwkv(r_cols, r_state[0], r_state[1], lp, rwkv_tm, rwkv_steps)
    x_new = merge_ffn(x2d, y_m.reshape(B * T, -1), y_a.reshape(B * T, -1), y_r.reshape(B * T, -1), merge,
                      lw['wm'], lw['wa'], lw['wr'], lw['wo'], lp['norm2_g'], lw['wgu'], lw['wd'], gf,
                      tm, final_norm)
    return x_new.reshape(B, T, D), new_rows, (conv_n, c_n, n_n, m_n), (r_cols[:, -1:], s_n)


def kernel(x_prompt, x_sample, cache_cmp_k, cache_cmp_v, cache_slc_k, cache_slc_v, cache_win_k, cache_win_v, state_mlstm_conv, state_mlstm_C, state_mlstm_n, state_mlstm_m, state_rwkv_shift, state_rwkv, page_table, norm1_g, w_in, mlstm_if_bias, mlstm_conv_w, mlstm_conv_b, nsa_cmp_pe, nsa_cmp_w1, nsa_cmp_w2, rwkv_mu, rwkv_w0, rwkv_w2, rwkv_a0, rwkv_a2, rwkv_g2, rwkv_k_k, rwkv_k_a, rwkv_r_k, rwkv_ln_g, rwkv_ln_b, w_branch_m, w_branch_a, w_branch_r, w_out, norm2_g, w_gate_up, w_down, final_norm_g):
    bp, tp, _ = x_prompt.shape
    bs, ts, _ = x_sample.shape
    assert ts == 1
    depth = w_in.shape[0]
    kvshape = lambda a: a.reshape(a.shape[0], a.shape[1], A_KV_HEADS, A_HEAD_DIM)
    xp, xs = x_prompt, x_sample
    pools = [channel_major(c) for c in (cache_cmp_k, cache_cmp_v, cache_slc_k, cache_slc_v)]
    win_k, win_v = channel_major(cache_win_k), channel_major(cache_win_v)
    states_p, states_s = [], []
    for l in range(depth):
        lw = _layer_params(l, w_in, w_branch_m, w_branch_a, w_branch_r, w_out, w_gate_up, w_down,
                           nsa_cmp_pe, nsa_cmp_w1, nsa_cmp_w2)
        lp = {'norm1_g': norm1_g[l], 'mlstm_if_bias': mlstm_if_bias[l], 'mlstm_conv_w': mlstm_conv_w[l],
              'mlstm_conv_b': mlstm_conv_b[l], 'rwkv_mu': rwkv_mu[l], 'rwkv_w0': rwkv_w0[l],
              'rwkv_w2': rwkv_w2[l], 'rwkv_a0': rwkv_a0[l], 'rwkv_a2': rwkv_a2[l], 'rwkv_g2': rwkv_g2[l],
              'rwkv_k_k': rwkv_k_k[l], 'rwkv_k_a': rwkv_k_a[l], 'rwkv_r_k': rwkv_r_k[l],
              'rwkv_ln_g': rwkv_ln_g[l], 'rwkv_ln_b': rwkv_ln_b[l], 'norm2_g': norm2_g[l]}
        last = l == depth - 1

        def nsa_p(a_q, a_gate, rows):
            kc = compress_rows(rows[0], lw['cwk'])
            vc = compress_rows(rows[1], lw['cwv'])
            return nsa_prompt_attend(a_q, a_gate, kc, vc, *rows[2:])

        m_zero = (jnp.zeros((bp, M_CONV - 1, 2 * M_WIDTH), F32),
                  jnp.zeros((bp, M_HEADS, M_HEAD_DIM, M_HEAD_DIM), F32),
                  jnp.zeros((bp, M_HEADS, M_HEAD_DIM), F32), jnp.zeros((bp, M_HEADS), F32))
        r_zero = (jnp.zeros((bp, 1, R_COLS), F32), jnp.zeros((bp, R_HEADS, R_HEAD_DIM, R_HEAD_DIM), F32))
        xp, rows, m_new, r_new = _trunk_layer(xp, lw, lp, nsa_p, m_zero, r_zero, PROMPT_ROW_TILE,
                                              MLSTM_CHUNK, PROMPT_ROW_TILE, RWKV_STEPS, final_norm_g, last)
        keep = min(WINDOW, tp)
        states_p.append(tuple(kvshape(r) for r in rows[:4])
                        + (kvshape(rows[4][:, tp - keep:]), kvshape(rows[5][:, tp - keep:])) + m_new + r_new)

        def nsa_s(a_q, a_gate, rows, l=l):
            return nsa_sample(a_q, a_gate, rows, pools, win_k, win_v, l, page_table, lw['cwk'], lw['cwv'])

        xs, rows, m_new, r_new = _trunk_layer(
            xs, lw, lp, nsa_s,
            (state_mlstm_conv[l], state_mlstm_C[l], state_mlstm_n[l], state_mlstm_m[l]),
            (state_rwkv_shift[l], state_rwkv[l]), bs * ts, SUBLANES, SUBLANES, SUBLANES, final_norm_g, last)
        keep = win_k.shape[3]

        def win_new(win, new_row):
            w = jnp.concatenate([win[l], jnp.swapaxes(new_row, 1, 2)], axis=2)[:, :, -keep:]
            return jnp.transpose(w.reshape(bs, A_KV_HEADS, A_HEAD_DIM, keep), (0, 3, 1, 2))

        states_s.append(tuple(kvshape(r) for r in rows[:4])
                        + (win_new(win_k, rows[4]), win_new(win_v, rows[5])) + m_new + r_new)

    stack = lambda states: [jnp.stack(a) for a in zip(*states)]
    return (xp, xs, *stack(states_p), *stack(states_s))
```

```python
import functools
import math

import numpy as np
import jax
import jax.numpy as jnp
from jax import lax
from jax.experimental import pallas as pl
from jax.experimental.pallas import tpu as pltpu

F32 = jnp.float32
BF16 = jnp.bfloat16

D_MODEL = 1024
DEPTH = 2
PAGE_SIZE = 128
M_HEADS = 4
M_HEAD_DIM = 128
M_WIDTH = M_HEADS * M_HEAD_DIM
M_CONV = 4
A_HEADS = 8
A_KV_HEADS = 2
A_HEAD_DIM = 64
A_WIDTH = A_HEADS * A_HEAD_DIM
A_GROUP = A_HEADS // A_KV_HEADS
A_SCALE = A_HEAD_DIM ** -0.5
A_KV_LANES = A_KV_HEADS * A_HEAD_DIM
CMP_STRIDE = 16
CMP_BLOCK = 2 * CMP_STRIDE
CMP_HIDDEN = A_HEAD_DIM
SEL_BLOCK = 64
N_SEL = 16
WINDOW = 512
Q_BLOCK = 128
R_HEADS = 8
R_HEAD_DIM = 64
R_WIDTH = R_HEADS * R_HEAD_DIM
R_DECAY_RANK = 64
R_A_RANK = 64
R_GATE_RANK = 128
R_COLS = 3 * R_WIDTH + R_DECAY_RANK + R_A_RANK + R_GATE_RANK
LN_X_EPS = 64e-5
D_FF = ((-(-8 * D_MODEL // 3)) + 255) // 256 * 256
IN_SPLITS = (2 * M_WIDTH, M_WIDTH, M_WIDTH, 2 * M_HEADS,
             A_WIDTH, 6 * A_KV_HEADS * A_HEAD_DIM, 3 * A_HEADS,
             R_COLS, 3 * D_MODEL)
RMS_EPS = 1e-6
NEG = -1e30
FORCE = 1e9
LOWEST = -3e38

VMEM_LIMIT_BYTES = 56 * 1024 * 1024
SUBLANES = 8
LANES = 128


def _cparams(*sem):
    return pltpu.CompilerParams(dimension_semantics=sem, vmem_limit_bytes=VMEM_LIMIT_BYTES)


def _resident(shape):
    nd = len(shape)
    return pl.BlockSpec(shape, lambda *_: (0,) * nd, pipeline_mode=pl.Buffered(1))


def _dot(a, b):
    return jnp.dot(a.astype(BF16), b.astype(BF16), preferred_element_type=F32)


def _dot_nt(a, b):
    return lax.dot_general(a.astype(BF16), b.astype(BF16), (((1,), (1,)), ((), ())),
                           preferred_element_type=F32)


def _split3(x):
    hi = x.astype(BF16)
    r = x - hi.astype(F32)
    mid = r.astype(BF16)
    lo = (r - mid.astype(F32)).astype(BF16)
    return hi, mid, lo


def _dot_f32_rhs(a_bf16, x):
    hi, mid, lo = _split3(x)
    d = lambda p: jnp.dot(a_bf16, p, preferred_element_type=F32)
    return d(hi) + d(mid) + d(lo)


def _dot_f32_lhs(x, a_bf16):
    hi, mid, lo = _split3(x)
    d = lambda p: jnp.dot(p, a_bf16, preferred_element_type=F32)
    return d(hi) + d(mid) + d(lo)


def _dot_f32_rhs_nt(a_bf16, x):
    hi, mid, lo = _split3(x)
    d = lambda p: lax.dot_general(a_bf16, p, (((1,), (1,)), ((), ())), preferred_element_type=F32)
    return d(hi) + d(mid) + d(lo)


def _transpose_f32(x, eye_bf16):
    hi, mid, lo = _split3(x)
    d = lambda p: lax.dot_general(eye_bf16, p, (((1,), (1,)), ((), ())), preferred_element_type=F32)
    return d(hi) + d(mid) + d(lo)


def _eye(n, dtype=BF16):
    r = lax.broadcasted_iota(jnp.int32, (n, n), 0)
    c = lax.broadcasted_iota(jnp.int32, (n, n), 1)
    return (r == c).astype(dtype)


def _sigmoid(x):
    return 1.0 / (1.0 + jnp.exp(-x))


def _silu(x):
    return x * _sigmoid(x)


def _softplus(x):
    return jnp.maximum(x, 0.0) + jnp.log(1.0 + jnp.exp(-jnp.abs(x)))


def _rmsnorm(x, g):
    return x * lax.rsqrt(jnp.mean(x * x, axis=-1, keepdims=True) + RMS_EPS) * g


def _norm_proj_kernel(x_ref, g_ref, *refs):
    n = len(refs) // 2
    hb = _rmsnorm(x_ref[...], g_ref[...]).astype(BF16)
    for w_ref, o_ref in zip(refs[:n], refs[n:]):
        o_ref[...] = jnp.dot(hb, w_ref[...], preferred_element_type=F32)


def norm_proj(x2d, g, weights, tm):
    m, d = x2d.shape
    assert m % tm == 0
    row = lambda i: (i, 0)
    in_specs = [pl.BlockSpec((tm, d), row), _resident((1, d))]
    in_specs += [_resident(w.shape) for w in weights]
    out_specs = [pl.BlockSpec((tm, w.shape[1]), row) for w in weights]
    out_shape = [jax.ShapeDtypeStruct((m, w.shape[1]), F32) for w in weights]
    return pl.pallas_call(
        _norm_proj_kernel, grid=(m // tm,), in_specs=in_specs, out_specs=out_specs,
        out_shape=out_shape, compiler_params=_cparams("parallel"), name="norm_proj",
    )(x2d, g.reshape(1, d), *weights)


def _merge_ffn_kernel(x_ref, ym_ref, ya_ref, yr_ref, mg_ref, wm_ref, wa_ref, wr_ref, wo_ref,
                      g2_ref, wgu_ref, wd_ref, gf_ref, o_ref, *, final_norm):
    mg = mg_ref[...]
    merged = (_sigmoid(mg[:, :D_MODEL]) * _dot(ym_ref[...], wm_ref[...])
              + _sigmoid(mg[:, D_MODEL:2 * D_MODEL]) * _dot(ya_ref[...], wa_ref[...])
              + _sigmoid(mg[:, 2 * D_MODEL:]) * _dot(yr_ref[...], wr_ref[...]))
    x1 = x_ref[...] + _dot(merged, wo_ref[...])
    gu = _dot(_rmsnorm(x1, g2_ref[...]), wgu_ref[...])
    x2 = x1 + _dot(_silu(gu[:, :D_FF]) * gu[:, D_FF:], wd_ref[...])
    o_ref[...] = _rmsnorm(x2, gf_ref[...]) if final_norm else x2


def merge_ffn(x2d, ym, ya, yr, mg, wm, wa, wr, wo, g2, wgu, wd, gf, tm, final_norm):
    m, d = x2d.shape
    row = lambda i: (i, 0)
    acts = [x2d, ym, ya, yr, mg]
    consts = [wm, wa, wr, wo, g2.reshape(1, d), wgu, wd, gf.reshape(1, d)]
    in_specs = [pl.BlockSpec((tm, a.shape[1]), row) for a in acts]
    in_specs += [_resident(c.shape) for c in consts]
    return pl.pallas_call(
        functools.partial(_merge_ffn_kernel, final_norm=final_norm),
        grid=(m // tm,), in_specs=in_specs, out_specs=pl.BlockSpec((tm, d), row),
        out_shape=jax.ShapeDtypeStruct((m, d), F32), compiler_params=_cparams("parallel"),
        name="merge_ffn",
    )(*acts, *consts)


def _mlstm_kernel(qk_ref, v_ref, o_ref, if_ref, conv0_ref, c0_ref, n0_ref, m0_ref,
                  cw_ref, cb_ref, ifb_ref,
                  y_ref, conv_out_ref, c_out_ref, n_out_ref, m_out_ref,
                  cbuf, c_scr, n_scr, m_scr, *, chunk, n_last):
    c = pl.program_id(1)
    nc = pl.num_programs(1)
    L = chunk
    pad = SUBLANES

    @pl.when(c == 0)
    def _():
        cbuf[0:pad, :] = conv0_ref[0]
        c_scr[...] = c0_ref[0]
        n_scr[...] = n0_ref[0]
        m_scr[...] = m0_ref[0]

    cbuf[pad:pad + L, :] = qk_ref[0]
    acc = cb_ref[...]
    for j in range(M_CONV):
        acc = acc + cbuf[pad - (M_CONV - 1) + j: pad - (M_CONV - 1) + j + L, :] * cw_ref[j:j + 1, :]
    qk = _silu(acc)

    @pl.when(c == nc - 1)
    def _():
        conv_out_ref[0] = cbuf[n_last:n_last + pad, :]

    cbuf[0:pad, :] = cbuf[L:L + pad, :]

    gates = if_ref[0] + ifb_ref[...]
    lane8 = lax.broadcasted_iota(jnp.int32, (L, 2 * M_HEADS), 1)
    lf_all = -_softplus(-gates)
    lif = jnp.where(lane8 < M_HEADS, gates, lf_all)
    row_id = lax.broadcasted_iota(jnp.int32, (L, 2 * M_HEADS), 0)
    is_last_chunk = c == nc - 1
    if n_last < L:
        dead = jnp.logical_and(is_last_chunk, row_id >= n_last)
        lif = jnp.where(dead, jnp.where(lane8 < M_HEADS, NEG, 0.0), lif)
    t_i = lax.broadcasted_iota(jnp.int32, (L, L), 0)
    s_i = lax.broadcasted_iota(jnp.int32, (L, L), 1)
    causal = s_i <= t_i
    tri = causal.astype(BF16)
    csum = _dot_f32_rhs(tri, lif)
    lib = jnp.where(lane8 < M_HEADS, lif, csum)
    lib_t = _transpose_f32(lib, _eye(2 * M_HEADS))
    eye_h = _eye(M_HEAD_DIM)
    last = n_last - 1

    for h in range(M_HEADS):
        sl = slice(h * M_HEAD_DIM, (h + 1) * M_HEAD_DIM)
        q = qk[:, sl]
        k = qk[:, M_WIDTH + h * M_HEAD_DIM: M_WIDTH + (h + 1) * M_HEAD_DIM] * (M_HEAD_DIM ** -0.5)
        v = v_ref[0, :, sl]
        li_col = lib[:, h:h + 1]
        b_col = lib[:, M_HEADS + h:M_HEADS + h + 1]
        li_row = lib_t[h:h + 1, :]
        b_row = lib_t[M_HEADS + h:M_HEADS + h + 1, :]
        m_prev = m_scr[0:1, h:h + 1]
        dmat = jnp.where(causal, b_col - b_row + li_row, NEG)
        inter = b_col + m_prev
        m_t = jnp.maximum(inter, jnp.max(dmat, axis=1, keepdims=True))
        w_intra = jnp.exp(dmat - m_t)
        w_inter = jnp.exp(inter - m_t)
        att = w_intra * _dot_nt(q, k)
        c_h = c_scr[h]
        n_h = n_scr[h:h + 1, :]
        num = _dot(att, v) + w_inter * _dot(q, c_h)
        den = jnp.sum(att, axis=1, keepdims=True) + w_inter * jnp.sum(q * n_h, axis=1, keepdims=True)
        hh = num / jnp.maximum(jnp.abs(den), jnp.exp(-m_t))
        y_ref[0, :, sl] = _sigmoid(o_ref[0, :, sl]) * hh

        m_new = m_t[last:last + 1, :]
        b_last = b_col[last:last + 1, :]
        scale = jnp.exp(b_last + m_prev - m_new)
        w_end = jnp.exp(b_last - b_col + li_col - m_new)
        kw = k * w_end
        kw_t = lax.dot_general(eye_h, kw.astype(BF16), (((1,), (1,)), ((), ())),
                               preferred_element_type=F32)
        c_scr[h] = scale * c_h + _dot(kw_t, v)
        n_scr[h:h + 1, :] = scale * n_h + jnp.sum(kw, axis=0, keepdims=True)
        m_scr[0:1, h:h + 1] = m_new

    @pl.when(is_last_chunk)
    def _():
        c_out_ref[0] = c_scr[...]
        n_out_ref[0] = n_scr[...]
        m_out_ref[0] = m_scr[...]


def mlstm(m_qk, m_v, m_o, m_if, conv0, c0, n0, m0, conv_w, conv_b, if_bias, chunk):
    B, T, _ = m_qk.shape
    nc = -(-T // chunk)
    tp = nc * chunk
    n_last = T - (nc - 1) * chunk
    if tp != T:
        padt = lambda a: jnp.pad(a, ((0, 0), (0, tp - T), (0, 0)))
        m_qk, m_v, m_o, m_if = padt(m_qk), padt(m_v), padt(m_o), padt(m_if)
    conv0p = jnp.pad(conv0, ((0, 0), (SUBLANES - (M_CONV - 1), 0), (0, 0)))
    n0p = jnp.pad(n0, ((0, 0), (0, SUBLANES - M_HEADS), (0, 0)))
    m0p = jnp.pad(m0, ((0, 0), (0, LANES - M_HEADS))).reshape(B, 1, LANES)
    cw = jnp.pad(conv_w, ((0, SUBLANES - M_CONV), (0, 0)))
    tile = lambda w: pl.BlockSpec((1, chunk, w), lambda b, c: (b, c, 0))
    perb = lambda *s: pl.BlockSpec((1,) + s, lambda b, c: (b,) + (0,) * len(s))
    outs = pl.pallas_call(
        functools.partial(_mlstm_kernel, chunk=chunk, n_last=n_last),
        grid=(B, nc),
        in_specs=[tile(2 * M_WIDTH), tile(M_WIDTH), tile(M_WIDTH), tile(2 * M_HEADS),
                  perb(SUBLANES, 2 * M_WIDTH), perb(M_HEADS, M_HEAD_DIM, M_HEAD_DIM),
                  perb(SUBLANES, M_HEAD_DIM), perb(1, LANES),
                  _resident((SUBLANES, 2 * M_WIDTH)), _resident((1, 2 * M_WIDTH)),
                  _resident((1, 2 * M_HEADS))],
        out_specs=[tile(M_WIDTH), perb(SUBLANES, 2 * M_WIDTH),
                   perb(M_HEADS, M_HEAD_DIM, M_HEAD_DIM), perb(SUBLANES, M_HEAD_DIM), perb(1, LANES)],
        out_shape=[jax.ShapeDtypeStruct((B, tp, M_WIDTH), F32),
                   jax.ShapeDtypeStruct((B, SUBLANES, 2 * M_WIDTH), F32),
                   jax.ShapeDtypeStruct((B, M_HEADS, M_HEAD_DIM, M_HEAD_DIM), F32),
                   jax.ShapeDtypeStruct((B, SUBLANES, M_HEAD_DIM), F32),
                   jax.ShapeDtypeStruct((B, 1, LANES), F32)],
        scratch_shapes=[pltpu.VMEM((chunk + SUBLANES, 2 * M_WIDTH), F32),
                        pltpu.VMEM((M_HEADS, M_HEAD_DIM, M_HEAD_DIM), F32),
                        pltpu.VMEM((SUBLANES, M_HEAD_DIM), F32),
                        pltpu.VMEM((1, LANES), F32)],
        compiler_params=_cparams("parallel", "arbitrary"), name="mlstm",
    )(m_qk, m_v, m_o, m_if, conv0p, c0, n0p, m0p, cw, conv_b.reshape(1, -1), if_bias.reshape(1, -1))
    y, conv_o, c_o, n_o, m_o_ = outs
    return (y[:, :T], conv_o[:, SUBLANES - (M_CONV - 1):], c_o, n_o[:, :M_HEADS], m_o_[:, 0, :M_HEADS])


def _rwkv_prep_kernel(cols_ref, shift0_ref, mu_ref, w0_ref, w2_ref, a0_ref, a2_ref, g2_ref,
                      kk_ref, ka_ref, rk_ref,
                      r_o, w_o, k_o, v_o, a_o, b_o, g_o, bonus_o, carry):
    t = pl.program_id(1)

    @pl.when(t == 0)
    def _():
        carry[...] = shift0_ref[0]

    cols = cols_ref[0]
    tm = cols.shape[0]
    row = lax.broadcasted_iota(jnp.int32, cols.shape, 0)
    prev = jnp.where(row == 0, carry[...], pltpu.roll(cols, 1, 0))
    carry[...] = cols[tm - 1:tm, :]
    mixed = cols + (prev - cols) * mu_ref[...]
    W = R_WIDTH
    r = mixed[:, :W]
    k = mixed[:, W:2 * W]
    v = mixed[:, 2 * W:3 * W]
    o = 3 * W
    wd = mixed[:, o:o + R_DECAY_RANK]
    ad = mixed[:, o + R_DECAY_RANK:o + R_DECAY_RANK + R_A_RANK]
    gd = mixed[:, o + R_DECAY_RANK + R_A_RANK:]
    w = -_softplus(-(w0_ref[...] + _dot(jnp.tanh(wd), w2_ref[...]))) - 0.5
    a = _sigmoid(a0_ref[...] + _dot(ad, a2_ref[...]))
    g_o[0] = _dot(_sigmoid(gd), g2_ref[...])
    kk = k * kk_ref[...]
    k2 = k * (1.0 + (a - 1.0) * ka_ref[...])
    rkk = r * k2 * rk_ref[...]
    r_o[0] = r
    w_o[0] = -jnp.exp(w)
    k_o[0] = k2
    v_o[0] = v
    for h in range(R_HEADS):
        sl = slice(h * R_HEAD_DIM, (h + 1) * R_HEAD_DIM)
        kkh = kk[:, sl]
        kkh = kkh * lax.rsqrt(jnp.maximum(jnp.sum(kkh * kkh, axis=1, keepdims=True), 1e-24))
        a_o[0, :, sl] = -kkh
        b_o[0, :, sl] = kkh * a[:, sl]
        bonus_o[0, :, sl] = jnp.sum(rkk[:, sl], axis=1, keepdims=True) * v[:, sl]


DOT_NN = (((1,), (0,)), ((), ()))
DOT_NT = (((1,), (1,)), ((), ()))
DOT_TN = (((0,), (0,)), ((), ()))


def _dot_x3(x, y, dims):
    xh = x.astype(BF16)
    xl = (x - xh.astype(F32)).astype(BF16)
    yh = y.astype(BF16)
    yl = (y - yh.astype(F32)).astype(BF16)
    d = lambda p, q: lax.dot_general(p, q, dims, preferred_element_type=F32)
    return d(xh, yh) + d(xh, yl) + d(xl, yh)


def _dot_tn(a, b):
    return lax.dot_general(a.astype(BF16), b.astype(BF16), DOT_TN, preferred_element_type=F32)


def _rwkv_chunk_kernel(r_ref, lw_ref, k_ref, v_ref, a_ref, b_ref, bonus_ref, g_ref, lng_ref, lnb_ref,
                       s0_ref, o_ref, s_out_ref, s_scr, *, chunk, n_valid):
    t = pl.program_id(1)
    nt = pl.num_programs(1)
    L, N = chunk, R_HEAD_DIM

    @pl.when(t == 0)
    def _():
        s_scr[...] = s0_ref[0]

    r, lw, k, v, a, b = r_ref[0], lw_ref[0], k_ref[0], v_ref[0], a_ref[0], b_ref[0]
    if n_valid % L:
        dead = (t * L + lax.broadcasted_iota(jnp.int32, (L, R_WIDTH), 0)) >= n_valid
        lw, k, v, a, b = (jnp.where(dead, 0.0, z) for z in (lw, k, v, a, b))
    ti = lax.broadcasted_iota(jnp.int32, (L, L), 0)
    si = lax.broadcasted_iota(jnp.int32, (L, L), 1)
    incl = si <= ti
    strict = si < ti
    eye = (si == ti).astype(F32)
    c = _dot_f32_rhs(incl.astype(BF16), lw)
    c_end = c[L - 1:L, :]
    a_hat = a * jnp.exp(c - lw)
    r_hat = r * jnp.exp(c)
    e_neg = jnp.exp(-c)
    b_til = b * e_neg
    k_til = k * e_neg
    e_end = jnp.exp(c_end - c)
    b_bar = b * e_end
    k_bar = k * e_end
    gam = jnp.exp(c_end)

    heads = range(R_HEADS)
    sl = [slice(h * N, (h + 1) * N) for h in heads]
    ah = [a_hat[:, sl[h]] for h in heads]
    rh = [r_hat[:, sl[h]] for h in heads]
    bt = [b_til[:, sl[h]] for h in heads]
    kt = [k_til[:, sl[h]] for h in heads]
    vh = [v[:, sl[h]] for h in heads]
    a_ab = [jnp.where(strict, _dot_x3(ah[h], bt[h], DOT_NT), 0.0) for h in heads]
    a_ak = [jnp.where(strict, _dot_nt(ah[h], kt[h]), 0.0) for h in heads]
    a_rb = [jnp.where(incl, _dot_nt(rh[h], bt[h]), 0.0) for h in heads]
    a_rk = [jnp.where(incl, _dot_nt(rh[h], kt[h]), 0.0) for h in heads]
    inv = [eye + a_ab[h] for h in heads]
    pw = a_ab
    span = 2
    while span < L:
        pw = [_dot_x3(pw[h], pw[h], DOT_NN) for h in heads]
        inv = [inv[h] + _dot_x3(inv[h], pw[h], DOT_NN) for h in heads]
        span *= 2
    s = [s_scr[h] for h in heads]
    rhs = [_dot_nt(ah[h], s[h]) + _dot(a_ak[h], vh[h]) for h in heads]
    u = [_dot_x3(inv[h], rhs[h], DOT_NN) for h in heads]
    y = [_dot_nt(rh[h], s[h]) + _dot(a_rb[h], u[h]) + _dot(a_rk[h], vh[h]) for h in heads]
    for h in heads:
        s_scr[h] = s[h] * gam[:, sl[h]] + _dot_tn(u[h], b_bar[:, sl[h]]) + _dot_tn(vh[h], k_bar[:, sl[h]])
    for h in heads:
        mean = jnp.mean(y[h], axis=1, keepdims=True)
        yc = y[h] - mean
        var = jnp.mean(yc * yc, axis=1, keepdims=True)
        yn = yc * lax.rsqrt(var + LN_X_EPS) * lng_ref[:, sl[h]] + lnb_ref[:, sl[h]]
        o_ref[0, :, sl[h]] = (yn + bonus_ref[0, :, sl[h]]) * g_ref[0, :, sl[h]]

    @pl.when(t == nt - 1)
    def _():
        s_out_ref[0] = s_scr[...]


def rwkv(cols, shift0, s0, p, tm, chunk):
    B, T, _ = cols.shape
    tp = -(-T // tm) * tm
    assert tm % chunk == 0 and chunk % SUBLANES == 0 and chunk & (chunk - 1) == 0
    colsp = jnp.pad(cols, ((0, 0), (0, tp - T), (0, 0))) if tp != T else cols
    nt = tp // tm
    vec = lambda a: a.reshape(1, -1)
    consts = [vec(p['rwkv_mu']), vec(p['rwkv_w0']), p['rwkv_w2'].astype(BF16), vec(p['rwkv_a0']),
              p['rwkv_a2'].astype(BF16), p['rwkv_g2'].astype(BF16), vec(p['rwkv_k_k']),
              vec(p['rwkv_k_a']), vec(p['rwkv_r_k'])]
    fl = jax.ShapeDtypeStruct((B, tp, R_WIDTH), F32)
    fl_spec = lambda n, w: pl.BlockSpec((1, n, w), lambda b, t: (b, t, 0))
    r, lw, k, v, a, bb, g, bonus = pl.pallas_call(
        _rwkv_prep_kernel, grid=(B, nt),
        in_specs=[fl_spec(tm, R_COLS), pl.BlockSpec((1, 1, R_COLS), lambda b, t: (b, 0, 0))]
                 + [_resident(c.shape) for c in consts],
        out_specs=[fl_spec(tm, R_WIDTH)] * 8, out_shape=[fl] * 8,
        scratch_shapes=[pltpu.VMEM((1, R_COLS), F32)],
        compiler_params=_cparams("parallel", "arbitrary"), name="rwkv_prep",
    )(colsp, shift0, *consts)
    st_spec = pl.BlockSpec((1, R_HEADS, R_HEAD_DIM, R_HEAD_DIM), lambda b, t: (b, 0, 0, 0))
    out, s_new = pl.pallas_call(
        functools.partial(_rwkv_chunk_kernel, chunk=chunk, n_valid=T),
        grid=(B, tp // chunk),
        in_specs=[fl_spec(chunk, R_WIDTH)] * 8 + [_resident((1, R_WIDTH))] * 2 + [st_spec],
        out_specs=[fl_spec(chunk, R_WIDTH), st_spec],
        out_shape=[fl, jax.ShapeDtypeStruct(s0.shape, F32)],
        scratch_shapes=[pltpu.VMEM((R_HEADS, R_HEAD_DIM, R_HEAD_DIM), F32)],
        compiler_params=_cparams("parallel", "arbitrary"), name="rwkv_chunk",
    )(r, lw, k, v, a, bb, bonus, g, vec(p['rwkv_ln_g']), vec(p['rwkv_ln_b']), s0)
    return out[:, :T], s_new


def compress_weights(pe, w1, w2):
    d, hdim, G = A_HEAD_DIM, CMP_HIDDEN, A_KV_HEADS
    w1r = w1.reshape(2, CMP_STRIDE, d, hdim)
    wl = jnp.zeros((CMP_STRIDE, G, d, 2, G, hdim), F32)
    for g in range(G):
        wl = wl.at[:, g, :, :, g, :].set(jnp.transpose(w1r, (1, 2, 0, 3)))
    wl = wl.reshape(CMP_STRIDE, G * d, 2 * G * hdim).astype(BF16)
    wpe = jnp.transpose(w1r, (1, 0, 2, 3)).reshape(CMP_STRIDE, 2 * d, hdim)
    wpe = jnp.concatenate([wpe] * G, axis=-1).astype(BF16)
    pe_l = jnp.transpose(pe.reshape(2, CMP_STRIDE, d), (1, 0, 2)).reshape(CMP_STRIDE, 1, 2 * d)
    pe_l = jnp.broadcast_to(pe_l, (CMP_STRIDE, SUBLANES, 2 * d))
    w2bd = jnp.zeros((G, hdim, G, d), F32)
    for g in range(G):
        w2bd = w2bd.at[g, :, g, :].set(w2)
    return wl, wpe, pe_l, w2bd.reshape(G * hdim, G * d).astype(BF16)


def _compress_halves(strided_rows, wl_ref, n):
    acc = None
    for l in range(CMP_STRIDE):
        part = _dot(strided_rows(l), wl_ref[l])
        acc = part if acc is None else acc + part
    return acc


def _compress_finish(halves, wpe_ref, pe_ref, w2_ref):
    n = halves.shape[0]
    gh = A_KV_HEADS * CMP_HIDDEN
    pe_term = None
    for l in range(CMP_STRIDE):
        part = _dot(pe_ref[l], wpe_ref[l])
        pe_term = part if pe_term is None else pe_term + part
    second = pltpu.roll(halves[:, gh:], n - 1, 0)
    hid = _silu(halves[:, :gh] + second + pe_term[0:1, :])
    return _dot(hid, w2_ref[...])


def _compress_kernel(x_ref, wl_ref, wpe_ref, pe_ref, w2_ref, o_ref):
    n = o_ref.shape[1]
    rows = lambda l: x_ref[0, pl.ds(l, n, stride=CMP_STRIDE), :]
    o_ref[0] = _compress_finish(_compress_halves(rows, wl_ref, n), wpe_ref, pe_ref, w2_ref)


def compress_rows(rows, cw):
    B, T, W = rows.shape
    n = T // CMP_STRIDE
    return pl.pallas_call(
        _compress_kernel, grid=(B,),
        in_specs=[pl.BlockSpec((1, T, W), lambda b: (b, 0, 0))] + [_resident(c.shape) for c in cw],
        out_specs=pl.BlockSpec((1, n, W), lambda b: (b, 0, 0)),
        out_shape=jax.ShapeDtypeStruct((B, n, W), F32),
        compiler_params=_cparams("parallel"), name="nsa_compress",
    )(rows, *cw)


def _topk_mask_columns(rank, k):
    n_entries = rank.shape[0]
    row = lax.broadcasted_iota(jnp.int32, rank.shape, 0).astype(F32)
    sel = jnp.zeros(rank.shape, F32)
    for _ in range(k):
        m = jnp.max(rank, axis=0, keepdims=True)
        first = jnp.min(jnp.where(rank == m, row, float(n_entries)), axis=0, keepdims=True)
        hit = row == first
        sel = jnp.where(hit, 1.0, sel)
        rank = jnp.where(hit, LOWEST, rank)
    return sel


def _first_max(rank):
    lane = lax.broadcasted_iota(jnp.int32, rank.shape, 1).astype(F32)
    m = jnp.max(rank, axis=1, keepdims=True)
    first = jnp.min(jnp.where(rank == m, lane, float(rank.shape[1])), axis=1, keepdims=True)
    return lane == first, first


def _flash_update(state, s, bias, v):
    m_old, l_old, acc = state
    s = s + bias
    m_new = jnp.maximum(m_old, jnp.max(s, axis=1, keepdims=True))
    p = jnp.exp(s - m_new)
    alpha = jnp.exp(m_old - m_new)
    return (m_new, alpha * l_old + jnp.sum(p, axis=1, keepdims=True), alpha * acc + _dot(p, v))


def _nsa_prompt_kernel(q_ref, gate_ref, kc_ref, vc_ref, ks_ref, vs_ref, kw_ref, vw_ref, covt_ref,
                       o_ref, *, seq, tk, wk):
    i = pl.program_id(1)
    QB, d, J = Q_BLOCK, A_HEAD_DIM, A_GROUP
    nc = kc_ref.shape[1]
    nb = covt_ref.shape[0]
    st = i * QB
    q = q_ref[0] * A_SCALE
    gate = _sigmoid(gate_ref[0])
    rows = J * QB
    t_q = st + lax.broadcasted_iota(jnp.int32, (QB, 1), 0)
    t_rows = jnp.concatenate([t_q] * J, axis=0)
    blk_end = lax.broadcasted_iota(jnp.int32, (rows, nc), 1) * CMP_STRIDE + (CMP_BLOCK - 1)
    cmp_mask = blk_end <= t_rows
    jj = lax.broadcasted_iota(jnp.int32, (nb, QB), 0)
    cur = (st + lax.broadcasted_iota(jnp.int32, (1, QB), 1)) // SEL_BLOCK
    forced = (jj == 0) | (jj == cur) | (jj == cur - 1)
    zeros = lambda w: jnp.zeros((rows, w), F32)

    qgs, o_cmps, ranks = [], [], []
    for g in range(A_KV_HEADS):
        gl = slice(g * d, (g + 1) * d)
        qg = jnp.concatenate([q[:, (g * J + j) * d:(g * J + j + 1) * d] for j in range(J)],
                             axis=0).astype(BF16)
        s = jnp.where(cmp_mask, _dot_nt(qg, kc_ref[0, :, gl]), NEG)
        e = jnp.exp(s - jnp.max(s, axis=1, keepdims=True))
        p = jnp.where(cmp_mask, e / jnp.sum(e, axis=1, keepdims=True), 0.0)
        p_grp = p[0:QB]
        for j in range(1, J):
            p_grp = p_grp + p[j * QB:(j + 1) * QB]
        score = _dot_f32_rhs_nt(covt_ref[...], p_grp)
        qgs.append(qg)
        o_cmps.append(_dot(p, vc_ref[0, :, gl]))
        ranks.append(jnp.where(jj <= cur, jnp.where(forced, FORCE, score), NEG))
    sel_all = _topk_mask_columns(jnp.concatenate(ranks, axis=1), min(N_SEL, nb)).astype(BF16)

    k0w = pl.multiple_of(jnp.clip(st + QB - wk, 0, seq - wk), QB)
    rel = t_q - (k0w + lax.broadcasted_iota(jnp.int32, (1, wk), 1))
    win_bias = jnp.concatenate([jnp.where((rel >= 0) & (rel < WINDOW), 0.0, NEG)] * J, axis=0)

    for g in range(A_KV_HEADS):
        gl = slice(g * d, (g + 1) * d)
        qg, o_cmp = qgs[g], o_cmps[g]
        sel = sel_all[:, g * QB:(g + 1) * QB]

        def sel_tile(kt, state):
            k0 = pl.multiple_of(kt * tk, tk)
            s_pos = k0 + lax.broadcasted_iota(jnp.int32, (1, tk), 1)
            blk_of = k0 // SEL_BLOCK + lax.broadcasted_iota(jnp.int32, (nb, tk), 1) // SEL_BLOCK
            expand = (lax.broadcasted_iota(jnp.int32, (nb, tk), 0) == blk_of).astype(BF16)
            chosen = lax.dot_general(sel, expand, DOT_TN, preferred_element_type=F32) > 0.5
            bias = jnp.where(chosen & (s_pos <= t_q), 0.0, NEG)
            sc = _dot_nt(qg, ks_ref[0, pl.ds(k0, tk), gl])
            return _flash_update(state, sc, jnp.concatenate([bias] * J, axis=0), vs_ref[0, pl.ds(k0, tk), gl])

        init = (jnp.full((rows, 1), NEG, F32), jnp.zeros((rows, 1), F32), zeros(d))
        _, l_sel, acc_sel = lax.fori_loop(0, (st + QB - 1) // tk + 1, sel_tile, init)
        o_sel = acc_sel / l_sel

        sw = _dot_nt(qg, kw_ref[0, pl.ds(k0w, wk), gl]) + win_bias
        pw = jnp.exp(sw - jnp.max(sw, axis=1, keepdims=True))
        o_win = _dot(pw, vw_ref[0, pl.ds(k0w, wk), gl]) / jnp.sum(pw, axis=1, keepdims=True)

        for j in range(J):
            h = g * J + j
            rs = slice(j * QB, (j + 1) * QB)
            o_ref[0, :, h * d:(h + 1) * d] = (gate[:, h:h + 1] * o_cmp[rs]
                                              + gate[:, A_HEADS + h:A_HEADS + h + 1] * o_sel[rs]
                                              + gate[:, 2 * A_HEADS + h:2 * A_HEADS + h + 1] * o_win[rs])


def _covers(n_cmp, n_blocks):
    i = np.arange(n_cmp)[:, None]
    j = np.arange(n_blocks)[None, :]
    return (i * CMP_STRIDE < (j + 1) * SEL_BLOCK) & (i * CMP_STRIDE + CMP_BLOCK > j * SEL_BLOCK)


def nsa_prompt_attend(a_q, a_gate, kc, vc, ks, vs, kw, vw):
    B, T, _ = a_q.shape
    nc = kc.shape[1]
    nb = -(-T // SEL_BLOCK)
    tk = min(4 * Q_BLOCK, T)
    wk = min(WINDOW + Q_BLOCK, T)
    cov = jnp.asarray(_covers(nc, nb).T, BF16)
    tile = lambda w: pl.BlockSpec((1, Q_BLOCK, w), lambda b, i: (b, i, 0))
    full = lambda n, w: pl.BlockSpec((1, n, w), lambda b, i: (b, 0, 0))
    W = A_KV_LANES
    return pl.pallas_call(
        functools.partial(_nsa_prompt_kernel, seq=T, tk=tk, wk=wk),
        grid=(B, T // Q_BLOCK),
        in_specs=[tile(A_WIDTH), tile(3 * A_HEADS), full(nc, W), full(nc, W),
                  full(T, W), full(T, W), full(T, W), full(T, W), _resident(cov.shape)],
        out_specs=tile(A_WIDTH),
        out_shape=jax.ShapeDtypeStruct((B, T, A_WIDTH), F32),
        compiler_params=_cparams("parallel", "arbitrary"), name="nsa_prompt",
    )(a_q, a_gate, kc, vc, ks, vs, kw, vw, cov)


POOL_TILE_PAGES = 64
GATHER_PAGES = 16
CHUNKS_PER_PAGE = PAGE_SIZE // CMP_STRIDE
HEAD_ROWS = SUBLANES


def channel_major(cache):
    L, n, tok, G, d = cache.shape
    return jnp.transpose(cache, (0, 1, 3, 4, 2)).reshape(L, n, G * d, tok)


def _pool_halves_kernel(x_ref, wl_ref, o_ref, xs):
    pages, _, page = x_ref.shape[1:]
    eye = _eye(page)

    def to_token_major(p, carry):
        xs[pl.ds(pl.multiple_of(p * page, page), page), :] = _dot_nt(eye, x_ref[0, p])
        return carry

    lax.fori_loop(0, pages, to_token_major, 0)
    n = o_ref.shape[1]
    o_ref[0] = _compress_halves(lambda l: xs[pl.ds(l, n, stride=CMP_STRIDE), :], wl_ref, n)


def pool_halves(pool_cm, layer, wl):
    _, n_pool, W, page = pool_cm.shape
    pt = POOL_TILE_PAGES
    assert n_pool % pt == 0
    nt = n_pool // pt
    hw = wl.shape[2]
    out = pl.pallas_call(
        _pool_halves_kernel, grid=(nt,),
        in_specs=[pl.BlockSpec((1, pt, W, page), lambda i: (layer, i, 0, 0)), _resident(wl.shape)],
        out_specs=pl.BlockSpec((1, pt * CHUNKS_PER_PAGE, hw), lambda i: (i, 0, 0)),
        out_shape=jax.ShapeDtypeStruct((nt, pt * CHUNKS_PER_PAGE, hw), F32),
        scratch_shapes=[pltpu.VMEM((pt * page, W), F32)],
        compiler_params=_cparams("parallel"), name="nsa_pool_halves",
    )(pool_cm, wl)
    return out.reshape(n_pool, CHUNKS_PER_PAGE, hw)


def _nsa_sample_cmp_kernel(pt_ref, *refs, past_len):
    del pt_ref
    ng = GATHER_PAGES
    hk_refs, hv_refs = refs[:ng], refs[ng:2 * ng]
    (q_ref, wink_ref, winv_ref, kwn_ref, vwn_ref, wpek_ref, pek_ref, w2k_ref,
     wpev_ref, pev_ref, w2v_ref, cov_ref, ocmp_ref, owin_ref, idx_ref, hk_scr, hv_scr) = refs[2 * ng:]
    c = pl.program_id(1)
    rows_per_step = ng * CHUNKS_PER_PAGE
    base = pl.multiple_of(c * rows_per_step, rows_per_step)
    for i in range(ng):
        hk_scr[pl.ds(base + i * CHUNKS_PER_PAGE, CHUNKS_PER_PAGE), :] = hk_refs[i][0]
        hv_scr[pl.ds(base + i * CHUNKS_PER_PAGE, CHUNKS_PER_PAGE), :] = hv_refs[i][0]

    @pl.when(c == pl.num_programs(1) - 1)
    def _():
        d, R8 = A_HEAD_DIM, HEAD_ROWS
        kc = _compress_finish(hk_scr[...], wpek_ref, pek_ref, w2k_ref)
        vc = _compress_finish(hv_scr[...], wpev_ref, pev_ref, w2v_ref)
        nc = kc.shape[0]
        nb = cov_ref.shape[1]
        keep = wink_ref.shape[3]
        q_pos = past_len
        blk_end = lax.broadcasted_iota(jnp.int32, (R8, nc), 1) * CMP_STRIDE + (CMP_BLOCK - 1)
        cmp_mask = (blk_end <= q_pos) & (lax.broadcasted_iota(jnp.int32, (R8, nc), 1) < nc - 1)
        real_head = lax.broadcasted_iota(jnp.int32, (R8, nc), 0) < A_GROUP
        row8 = lax.broadcasted_iota(jnp.int32, (R8, nb), 0)
        jj = lax.broadcasted_iota(jnp.int32, (R8, nb), 1)
        cur = q_pos // SEL_BLOCK
        forced = (jj == 0) | (jj == cur) | (jj == cur - 1)
        k_idx = lax.broadcasted_iota(jnp.int32, (R8, keep), 1)
        rel = q_pos - (past_len - keep + k_idx)
        win_mask = (rel >= 0) & (rel < WINDOW)
        rank = None
        for g in range(A_KV_HEADS):
            gl = slice(g * d, (g + 1) * d)
            qg = q_ref[0, g] * A_SCALE
            s = jnp.where(cmp_mask, _dot_nt(qg, kc[:, gl]), NEG)
            e = jnp.exp(s - jnp.max(s, axis=1, keepdims=True))
            p = jnp.where(cmp_mask, e / jnp.sum(e, axis=1, keepdims=True), 0.0)
            ocmp_ref[0, g] = _dot(p, vc[:, gl])
            p_grp = jnp.sum(jnp.where(real_head, p, 0.0), axis=0, keepdims=True)
            score = _dot_f32_lhs(jnp.broadcast_to(p_grp, (R8, nc)), cov_ref[...])
            rank_g = jnp.where(jj <= cur, jnp.where(forced, FORCE, score), NEG)
            rank = rank_g if rank is None else jnp.where(row8 == g, rank_g, rank)
            sw = jnp.where(win_mask, _dot(qg, wink_ref[0, 0, gl, :]), NEG)
            s_new = jnp.sum(qg * kwn_ref[0, :, gl], axis=1, keepdims=True)
            m = jnp.maximum(jnp.max(sw, axis=1, keepdims=True), s_new)
            ew = jnp.where(win_mask, jnp.exp(sw - m), 0.0)
            e_new = jnp.exp(s_new - m)
            owin_ref[0, g] = ((_dot_nt(ew, winv_ref[0, 0, gl, :]) + e_new * vwn_ref[0, :, gl])
                              / (jnp.sum(ew, axis=1, keepdims=True) + e_new))
        out_lane = lax.broadcasted_iota(jnp.int32, (R8, LANES), 1)
        picks = jnp.zeros((R8, LANES), jnp.int32)
        for it in range(N_SEL - 1):
            hit, idx = _first_max(rank)
            picks = jnp.where(out_lane == it, idx.astype(jnp.int32), picks)
            rank = jnp.where(hit, LOWEST, rank)
        idx_ref[0] = picks


def nsa_sample_cmp(page_table, hk_pool, hv_pool, q_hm, win_k, win_v, layer, kw_new, vw_new, cwk, cwv, past_len):
    B, n_pages = page_table.shape
    ng = GATHER_PAGES
    assert n_pages % ng == 0
    nc = n_pages * CHUNKS_PER_PAGE
    hw = hk_pool.shape[2]
    nb = past_len // SEL_BLOCK
    cov = jnp.asarray(_covers(nc, nb), BF16)
    keep = win_k.shape[3]
    W = A_KV_LANES
    page_spec = lambda i: pl.BlockSpec((1, CHUNKS_PER_PAGE, hw), lambda b, c, pt: (pt[b, c * ng + i], 0, 0))
    perb = lambda *s: pl.BlockSpec((1,) + s, lambda b, c, pt: (b,) + (0,) * len(s))
    win_spec = pl.BlockSpec((1, 1, W, keep), lambda b, c, pt: (layer, b, 0, 0))
    const = lambda a: pl.BlockSpec(a.shape, lambda b, c, pt: (0,) * a.ndim)
    _, wpek, pek, w2k = cwk
    _, wpev, pev, w2v = cwv
    consts = [wpek, pek, w2k, wpev, pev, w2v, cov]
    hm = jax.ShapeDtypeStruct((B, A_KV_HEADS, HEAD_ROWS, A_HEAD_DIM), F32)
    grid_spec = pltpu.PrefetchScalarGridSpec(
        num_scalar_prefetch=1, grid=(B, n_pages // ng),
        in_specs=[page_spec(i) for i in range(ng)] * 2
                 + [perb(A_KV_HEADS, HEAD_ROWS, A_HEAD_DIM), win_spec, win_spec, perb(1, W), perb(1, W)]
                 + [const(a) for a in consts],
        out_specs=[perb(A_KV_HEADS, HEAD_ROWS, A_HEAD_DIM), perb(A_KV_HEADS, HEAD_ROWS, A_HEAD_DIM),
                   perb(HEAD_ROWS, LANES)],
        scratch_shapes=[pltpu.VMEM((nc, hw), F32), pltpu.VMEM((nc, hw), F32)])
    return pl.pallas_call(
        functools.partial(_nsa_sample_cmp_kernel, past_len=past_len), grid_spec=grid_spec,
        out_shape=[hm, hm, jax.ShapeDtypeStruct((B, HEAD_ROWS, LANES), jnp.int32)],
        compiler_params=_cparams("parallel", "arbitrary"), name="nsa_sample_cmp",
    )(page_table, *([hk_pool] * ng), *([hv_pool] * ng), q_hm, win_k, win_v, kw_new, vw_new, *consts)


SEL_PICKS_PER_STEP = 5


def _nsa_sample_sel_kernel(pt_ref, pk_ref, *refs, npk):
    del pt_ref
    P = SEL_PICKS_PER_STEP
    n_pages = A_KV_HEADS * P
    k_refs, v_refs = refs[:n_pages], refs[n_pages:2 * n_pages]
    q_ref, gate_ref, ksn_ref, vsn_ref, ocmp_ref, owin_ref, o_ref, m_scr, l_scr, acc_scr = refs[2 * n_pages:]
    b = pl.program_id(0)
    c = pl.program_id(1)
    d = A_HEAD_DIM
    page = k_refs[0].shape[3]
    tok_blk = lax.broadcasted_iota(jnp.int32, (HEAD_ROWS, page), 1) // SEL_BLOCK
    for g in range(A_KV_HEADS):
        gl = slice(g * d, (g + 1) * d)
        qg = q_ref[0, g] * A_SCALE

        @pl.when(c == 0)
        def _():
            m_scr[g] = jnp.sum(qg * ksn_ref[0, :, gl], axis=1, keepdims=True)
            l_scr[g] = jnp.ones((HEAD_ROWS, 1), F32)
            acc_scr[g] = jnp.broadcast_to(vsn_ref[0, :, gl], (HEAD_ROWS, d))

        chosen = [tok_blk == pk_ref[b, g * npk + c * P + i] % (page // SEL_BLOCK) for i in range(P)]
        s = [jnp.where(chosen[i], _dot(qg, k_refs[g * P + i][0, 0, gl, :]), NEG) for i in range(P)]
        m_old = m_scr[g]
        m = m_old
        for i in range(P):
            m = jnp.maximum(m, jnp.max(s[i], axis=1, keepdims=True))
        alpha = jnp.exp(m_old - m)
        l = alpha * l_scr[g]
        acc = alpha * acc_scr[g]
        for i in range(P):
            p = jnp.where(chosen[i], jnp.exp(s[i] - m), 0.0)
            l = l + jnp.sum(p, axis=1, keepdims=True)
            acc = acc + _dot_nt(p, v_refs[g * P + i][0, 0, gl, :])
        m_scr[g] = m
        l_scr[g] = l
        acc_scr[g] = acc

        @pl.when(c == pl.num_programs(1) - 1)
        def _():
            gate = _sigmoid(gate_ref[0, g])
            o_ref[0, g] = (gate[:, 0:1] * ocmp_ref[0, g] + gate[:, 1:2] * (acc / l)
                           + gate[:, 2:3] * owin_ref[0, g])


def nsa_sample_sel(page_table, picks, pool_k, pool_v, layer, q_hm, gate_hm, ks_new, vs_new, o_cmp, o_win):
    B = page_table.shape[0]
    _, n_pool, W, page = pool_k.shape
    per_page = page // SEL_BLOCK
    npk = picks.shape[2]
    P = SEL_PICKS_PER_STEP
    assert npk % P == 0
    picks = picks.reshape(B, A_KV_HEADS * npk)

    def blk(g, i):
        def index(b, c, pt, pk):
            return (layer, pt[b, pk[b, g * npk + c * P + i] // per_page], 0, 0)
        return pl.BlockSpec((1, 1, W, page), index)

    pages = [blk(g, i) for g in range(A_KV_HEADS) for i in range(P)]
    perb = lambda *s: pl.BlockSpec((1,) + s, lambda b, c, pt, pk: (b,) + (0,) * len(s))
    hm_spec = perb(A_KV_HEADS, HEAD_ROWS, A_HEAD_DIM)
    grid_spec = pltpu.PrefetchScalarGridSpec(
        num_scalar_prefetch=2, grid=(B, npk // P),
        in_specs=pages * 2 + [hm_spec, perb(A_KV_HEADS, HEAD_ROWS, 3), perb(1, W), perb(1, W), hm_spec, hm_spec],
        out_specs=hm_spec,
        scratch_shapes=[pltpu.VMEM((A_KV_HEADS, HEAD_ROWS, 1), F32), pltpu.VMEM((A_KV_HEADS, HEAD_ROWS, 1), F32),
                        pltpu.VMEM((A_KV_HEADS, HEAD_ROWS, A_HEAD_DIM), F32)])
    return pl.pallas_call(
        functools.partial(_nsa_sample_sel_kernel, npk=npk), grid_spec=grid_spec,
        out_shape=jax.ShapeDtypeStruct((B, A_KV_HEADS, HEAD_ROWS, A_HEAD_DIM), F32),
        compiler_params=_cparams("parallel", "arbitrary"), name="nsa_sample_sel",
    )(page_table, picks, *([pool_k] * len(pages)), *([pool_v] * len(pages)), q_hm, gate_hm, ks_new, vs_new,
      o_cmp, o_win)


def nsa_sample(a_q, a_gate, new_rows, pools, win_k, win_v, layer, page_table, cwk, cwv):
    B = a_q.shape[0]
    past_len = page_table.shape[1] * PAGE_SIZE
    _, _, ksn, vsn, kwn, vwn = new_rows
    pool_kc, pool_vc, pool_ks, pool_vs = pools
    padj = ((0, 0), (0, 0), (0, HEAD_ROWS - A_GROUP), (0, 0))
    q_hm = jnp.pad(a_q.reshape(B, A_KV_HEADS, A_GROUP, A_HEAD_DIM), padj)
    gate_hm = jnp.pad(jnp.transpose(a_gate.reshape(B, 3, A_KV_HEADS, A_GROUP), (0, 2, 3, 1)), padj)
    hk = pool_halves(pool_kc, layer, cwk[0])
    hv = pool_halves(pool_vc, layer, cwv[0])
    o_cmp, o_win, picks = nsa_sample_cmp(page_table, hk, hv, q_hm, win_k, win_v, layer, kwn, vwn, cwk, cwv,
                                         past_len)
    picks = picks[:, :A_KV_HEADS, :N_SEL - 1]
    o = nsa_sample_sel(page_table, picks, pool_ks, pool_vs, layer, q_hm, gate_hm, ksn, vsn, o_cmp, o_win)
    return o[:, :, :A_GROUP].reshape(B, 1, A_WIDTH)


PROJ_GROUPS = (2 * M_WIDTH, M_WIDTH, M_WIDTH, 2 * M_HEADS, A_WIDTH) + (A_KV_LANES,) * 6 + (
    3 * A_HEADS, R_COLS, 3 * D_MODEL)
PROMPT_ROW_TILE = 256
MLSTM_CHUNK = 128
RWKV_STEPS = 64


def _layer_params(l, w_in, w_branch_m, w_branch_a, w_branch_r, w_out, w_gate_up, w_down,
                  nsa_cmp_pe, nsa_cmp_w1, nsa_cmp_w2):
    offs = np.cumsum((0,) + PROJ_GROUPS)
    assert offs[-1] == w_in.shape[2]
    wi = w_in[l].astype(BF16)
    return {
        'proj': [wi[:, a:b] for a, b in zip(offs[:-1], offs[1:])],
        'wm': w_branch_m[l].astype(BF16), 'wa': w_branch_a[l].astype(BF16), 'wr': w_branch_r[l].astype(BF16),
        'wo': w_out[l].astype(BF16), 'wgu': w_gate_up[l].astype(BF16), 'wd': w_down[l].astype(BF16),
        'cwk': compress_weights(nsa_cmp_pe[l, 0], nsa_cmp_w1[l, 0], nsa_cmp_w2[l, 0]),
        'cwv': compress_weights(nsa_cmp_pe[l, 1], nsa_cmp_w1[l, 1], nsa_cmp_w2[l, 1]),
    }


def _trunk_layer(x, lw, lp, nsa_fn, m_state, r_state, tm, chunk, rwkv_tm, rwkv_steps, gf, final_norm):
    B, T, D = x.shape
    x2d = x.reshape(B * T, D)
    outs = norm_proj(x2d, lp['norm1_g'], lw['proj'], tm)
    (m_qk, m_v, m_o, m_if, a_q), new_rows, (a_gate, r_cols, merge) = outs[:5], outs[5:11], outs[11:]
    b3 = lambda a: a.reshape(B, T, a.shape[-1])
    new_rows = [b3(r) for r in new_rows]
    y_m, conv_n, c_n, n_n, m_n = mlstm(b3(m_qk), b3(m_v), b3(m_o), b3(m_if), *m_state,
                                        lp['mlstm_conv_w'], lp['mlstm_conv_b'], lp['mlstm_if_bias'], chunk)
    y_a = nsa_fn(b3(a_q), b3(a_gate), new_rows)
    r_cols = b3(r_cols)
    y_r, s_n = rwkv(r_cols, r_state[0], r_state[1], lp, rwkv_tm, rwkv_steps)
    x_new = merge_ffn(x2d, y_m.reshape(B * T, -1), y_a.reshape(B * T, -1), y_r.reshape(B * T, -1), merge,
                      lw['wm'], lw['wa'], lw['wr'], lw['wo'], lp['norm2_g'], lw['wgu'], lw['wd'], gf,
                      tm, final_norm)
    return x_new.reshape(B, T, D), new_rows, (conv_n, c_n, n_n, m_n), (r_cols[:, -1:], s_n)


def kernel(x_prompt, x_sample, cache_cmp_k, cache_cmp_v, cache_slc_k, cache_slc_v, cache_win_k, cache_win_v, state_mlstm_conv, state_mlstm_C, state_mlstm_n, state_mlstm_m, state_rwkv_shift, state_rwkv, page_table, norm1_g, w_in, mlstm_if_bias, mlstm_conv_w, mlstm_conv_b, nsa_cmp_pe, nsa_cmp_w1, nsa_cmp_w2, rwkv_mu, rwkv_w0, rwkv_w2, rwkv_a0, rwkv_a2, rwkv_g2, rwkv_k_k, rwkv_k_a, rwkv_r_k, rwkv_ln_g, rwkv_ln_b, w_branch_m, w_branch_a, w_branch_r, w_out, norm2_g, w_gate_up, w_down, final_norm_g):
    bp, tp, _ = x_prompt.shape
    bs, ts, _ = x_sample.shape
    assert ts == 1
    depth = w_in.shape[0]
    kvshape = lambda a: a.reshape(a.shape[0], a.shape[1], A_KV_HEADS, A_HEAD_DIM)
    xp, xs = x_prompt, x_sample
    pools = [channel_major(c) for c in (cache_cmp_k, cache_cmp_v, cache_slc_k, cache_slc_v)]
    win_k, win_v = channel_major(cache_win_k), channel_major(cache_win_v)
    states_p, states_s = [], []
    for l in range(depth):
        lw = _layer_params(l, w_in, w_branch_m, w_branch_a, w_branch_r, w_out, w_gate_up, w_down,
                           nsa_cmp_pe, nsa_cmp_w1, nsa_cmp_w2)
        lp = {'norm1_g': norm1_g[l], 'mlstm_if_bias': mlstm_if_bias[l], 'mlstm_conv_w': mlstm_conv_w[l],
              'mlstm_conv_b': mlstm_conv_b[l], 'rwkv_mu': rwkv_mu[l], 'rwkv_w0': rwkv_w0[l],
              'rwkv_w2': rwkv_w2[l], 'rwkv_a0': rwkv_a0[l], 'rwkv_a2': rwkv_a2[l], 'rwkv_g2': rwkv_g2[l],
              'rwkv_k_k': rwkv_k_k[l], 'rwkv_k_a': rwkv_k_a[l], 'rwkv_r_k': rwkv_r_k[l],
              'rwkv_ln_g': rwkv_ln_g[l], 'rwkv_ln_b': rwkv_ln_b[l], 'norm2_g': norm2_g[l]}
        last = l == depth - 1

        def nsa_p(a_q, a_gate, rows):
            kc = compress_rows(rows[0], lw['cwk'])
            vc = compress_rows(rows[1], lw['cwv'])
            return nsa_prompt_attend(a_q, a_gate, kc, vc, *rows[2:])

        m_zero = (jnp.zeros((bp, M_CONV - 1, 2 * M_WIDTH), F32),
                  jnp.zeros((bp, M_HEADS, M_HEAD_DIM, M_HEAD_DIM), F32),
                  jnp.zeros((bp, M_HEADS, M_HEAD_DIM), F32), jnp.zeros((bp, M_HEADS), F32))
        r_zero = (jnp.zeros((bp, 1, R_COLS), F32), jnp.zeros((bp, R_HEADS, R_HEAD_DIM, R_HEAD_DIM), F32))
        xp, rows, m_new, r_new = _trunk_layer(xp, lw, lp, nsa_p, m_zero, r_zero, PROMPT_ROW_TILE,
                                              MLSTM_CHUNK, PROMPT_ROW_TILE, RWKV_STEPS, final_norm_g, last)
        keep = min(WINDOW, tp)
        states_p.append(tuple(kvshape(r) for r in rows[:4])
                        + (kvshape(rows[4][:, tp - keep:]), kvshape(rows[5][:, tp - keep:])) + m_new + r_new)

        def nsa_s(a_q, a_gate, rows, l=l):
            return nsa_sample(a_q, a_gate, rows, pools, win_k, win_v, l, page_table, lw['cwk'], lw['cwv'])

        xs, rows, m_new, r_new = _trunk_layer(
            xs, lw, lp, nsa_s,
            (state_mlstm_conv[l], state_mlstm_C[l], state_mlstm_n[l], state_mlstm_m[l]),
            (state_rwkv_shift[l], state_rwkv[l]), bs * ts, SUBLANES, SUBLANES, SUBLANES, final_norm_g, last)
        keep = win_k.shape[3]

        def win_new(win, new_row):
            w = jnp.concatenate([win[l], jnp.swapaxes(new_row, 1, 2)], axis=2)[:, :, -keep:]
            return jnp.transpose(w.reshape(bs, A_KV_HEADS, A_HEAD_DIM, keep), (0, 3, 1, 2))

        states_s.append(tuple(kvshape(r) for r in rows[:4])
                        + (win_new(win_k, rows[4]), win_new(win_v, rows[5])) + m_new + r_new)

    stack = lambda states: [jnp.stack(a) for a in zip(*states)]
    return (xp, xs, *stack(states_p), *stack(states_s))
```

```python
import functools
import math

import numpy as np
import jax
import jax.numpy as jnp
from jax import lax
from jax.experimental import pallas as pl
from jax.experimental.pallas import tpu as pltpu

F32 = jnp.float32
BF16 = jnp.bfloat16

D_MODEL = 1024
DEPTH = 2
PAGE_SIZE = 128
M_HEADS = 4
M_HEAD_DIM = 128
M_WIDTH = M_HEADS * M_HEAD_DIM
M_CONV = 4
A_HEADS = 8
A_KV_HEADS = 2
A_HEAD_DIM = 64
A_WIDTH = A_HEADS * A_HEAD_DIM
A_GROUP = A_HEADS // A_KV_HEADS
A_SCALE = A_HEAD_DIM ** -0.5
A_KV_LANES = A_KV_HEADS * A_HEAD_DIM
CMP_STRIDE = 16
CMP_BLOCK = 2 * CMP_STRIDE
CMP_HIDDEN = A_HEAD_DIM
SEL_BLOCK = 64
N_SEL = 16
WINDOW = 512
Q_BLOCK = 128
R_HEADS = 8
R_HEAD_DIM = 64
R_WIDTH = R_HEADS * R_HEAD_DIM
R_DECAY_RANK = 64
R_A_RANK = 64
R_GATE_RANK = 128
R_COLS = 3 * R_WIDTH + R_DECAY_RANK + R_A_RANK + R_GATE_RANK
LN_X_EPS = 64e-5
D_FF = ((-(-8 * D_MODEL // 3)) + 255) // 256 * 256
IN_SPLITS = (2 * M_WIDTH, M_WIDTH, M_WIDTH, 2 * M_HEADS,
             A_WIDTH, 6 * A_KV_HEADS * A_HEAD_DIM, 3 * A_HEADS,
             R_COLS, 3 * D_MODEL)
RMS_EPS = 1e-6
NEG = -1e30
FORCE = 1e9
LOWEST = -3e38

VMEM_LIMIT_BYTES = 56 * 1024 * 1024
SUBLANES = 8
LANES = 128


def _cparams(*sem):
    return pltpu.CompilerParams(dimension_semantics=sem, vmem_limit_bytes=VMEM_LIMIT_BYTES)


def _resident(shape):
    nd = len(shape)
    return pl.BlockSpec(shape, lambda *_: (0,) * nd, pipeline_mode=pl.Buffered(1))


def _dot(a, b):
    return jnp.dot(a.astype(BF16), b.astype(BF16), preferred_element_type=F32)


def _dot_nt(a, b):
    return lax.dot_general(a.astype(BF16), b.astype(BF16), (((1,), (1,)), ((), ())),
                           preferred_element_type=F32)


def _split3(x):
    hi = x.astype(BF16)
    r = x - hi.astype(F32)
    mid = r.astype(BF16)
    lo = (r - mid.astype(F32)).astype(BF16)
    return hi, mid, lo


def _dot_f32_rhs(a_bf16, x):
    hi, mid, lo = _split3(x)
    d = lambda p: jnp.dot(a_bf16, p, preferred_element_type=F32)
    return d(hi) + d(mid) + d(lo)


def _dot_f32_lhs(x, a_bf16):
    hi, mid, lo = _split3(x)
    d = lambda p: jnp.dot(p, a_bf16, preferred_element_type=F32)
    return d(hi) + d(mid) + d(lo)


def _dot_f32_rhs_nt(a_bf16, x):
    hi, mid, lo = _split3(x)
    d = lambda p: lax.dot_general(a_bf16, p, (((1,), (1,)), ((), ())), preferred_element_type=F32)
    return d(hi) + d(mid) + d(lo)


def _transpose_f32(x, eye_bf16):
    hi, mid, lo = _split3(x)
    d = lambda p: lax.dot_general(eye_bf16, p, (((1,), (1,)), ((), ())), preferred_element_type=F32)
    return d(hi) + d(mid) + d(lo)


def _eye(n, dtype=BF16):
    r = lax.broadcasted_iota(jnp.int32, (n, n), 0)
    c = lax.broadcasted_iota(jnp.int32, (n, n), 1)
    return (r == c).astype(dtype)


def _sigmoid(x):
    return 1.0 / (1.0 + jnp.exp(-x))


def _silu(x):
    return x * _sigmoid(x)


def _softplus(x):
    return jnp.maximum(x, 0.0) + jnp.log(1.0 + jnp.exp(-jnp.abs(x)))


def _rmsnorm(x, g):
    return x * lax.rsqrt(jnp.mean(x * x, axis=-1, keepdims=True) + RMS_EPS) * g


def _norm_proj_kernel(x_ref, g_ref, *refs):
    n = len(refs) // 2
    hb = _rmsnorm(x_ref[...], g_ref[...]).astype(BF16)
    for w_ref, o_ref in zip(refs[:n], refs[n:]):
        o_ref[...] = jnp.dot(hb, w_ref[...], preferred_element_type=F32)


def norm_proj(x2d, g, weights, tm):
    m, d = x2d.shape
    assert m % tm == 0
    row = lambda i: (i, 0)
    in_specs = [pl.BlockSpec((tm, d), row), _resident((1, d))]
    in_specs += [_resident(w.shape) for w in weights]
    out_specs = [pl.BlockSpec((tm, w.shape[1]), row) for w in weights]
    out_shape = [jax.ShapeDtypeStruct((m, w.shape[1]), F32) for w in weights]
    return pl.pallas_call(
        _norm_proj_kernel, grid=(m // tm,), in_specs=in_specs, out_specs=out_specs,
        out_shape=out_shape, compiler_params=_cparams("parallel"), name="norm_proj",
    )(x2d, g.reshape(1, d), *weights)


def _merge_ffn_kernel(x_ref, ym_ref, ya_ref, yr_ref, mg_ref, wm_ref, wa_ref, wr_ref, wo_ref,
                      g2_ref, wgu_ref, wd_ref, gf_ref, o_ref, *, final_norm):
    mg = mg_ref[...]
    merged = (_sigmoid(mg[:, :D_MODEL]) * _dot(ym_ref[...], wm_ref[...])
              + _sigmoid(mg[:, D_MODEL:2 * D_MODEL]) * _dot(ya_ref[...], wa_ref[...])
              + _sigmoid(mg[:, 2 * D_MODEL:]) * _dot(yr_ref[...], wr_ref[...]))
    x1 = x_ref[...] + _dot(merged, wo_ref[...])
    gu = _dot(_rmsnorm(x1, g2_ref[...]), wgu_ref[...])
    x2 = x1 + _dot(_silu(gu[:, :D_FF]) * gu[:, D_FF:], wd_ref[...])
    o_ref[...] = _rmsnorm(x2, gf_ref[...]) if final_norm else x2


def merge_ffn(x2d, ym, ya, yr, mg, wm, wa, wr, wo, g2, wgu, wd, gf, tm, final_norm):
    m, d = x2d.shape
    row = lambda i: (i, 0)
    acts = [x2d, ym, ya, yr, mg]
    consts = [wm, wa, wr, wo, g2.reshape(1, d), wgu, wd, gf.reshape(1, d)]
    in_specs = [pl.BlockSpec((tm, a.shape[1]), row) for a in acts]
    in_specs += [_resident(c.shape) for c in consts]
    return pl.pallas_call(
        functools.partial(_merge_ffn_kernel, final_norm=final_norm),
        grid=(m // tm,), in_specs=in_specs, out_specs=pl.BlockSpec((tm, d), row),
        out_shape=jax.ShapeDtypeStruct((m, d), F32), compiler_params=_cparams("parallel"),
        name="merge_ffn",
    )(*acts, *consts)


def _mlstm_kernel(qk_ref, v_ref, o_ref, if_ref, conv0_ref, c0_ref, n0_ref, m0_ref,
                  cw_ref, cb_ref, ifb_ref,
                  y_ref, conv_out_ref, c_out_ref, n_out_ref, m_out_ref,
                  cbuf, c_scr, n_scr, m_scr, *, chunk, n_last):
    c = pl.program_id(1)
    nc = pl.num_programs(1)
    L = chunk
    pad = SUBLANES

    @pl.when(c == 0)
    def _():
        cbuf[0:pad, :] = conv0_ref[0]
        c_scr[...] = c0_ref[0]
        n_scr[...] = n0_ref[0]
        m_scr[...] = m0_ref[0]

    cbuf[pad:pad + L, :] = qk_ref[0]
    acc = cb_ref[...]
    for j in range(M_CONV):
        acc = acc + cbuf[pad - (M_CONV - 1) + j: pad - (M_CONV - 1) + j + L, :] * cw_ref[j:j + 1, :]
    qk = _silu(acc)

    @pl.when(c == nc - 1)
    def _():
        conv_out_ref[0] = cbuf[n_last:n_last + pad, :]

    cbuf[0:pad, :] = cbuf[L:L + pad, :]

    gates = if_ref[0] + ifb_ref[...]
    lane8 = lax.broadcasted_iota(jnp.int32, (L, 2 * M_HEADS), 1)
    lf_all = -_softplus(-gates)
    lif = jnp.where(lane8 < M_HEADS, gates, lf_all)
    row_id = lax.broadcasted_iota(jnp.int32, (L, 2 * M_HEADS), 0)
    is_last_chunk = c == nc - 1
    if n_last < L:
        dead = jnp.logical_and(is_last_chunk, row_id >= n_last)
        lif = jnp.where(dead, jnp.where(lane8 < M_HEADS, NEG, 0.0), lif)
    t_i = lax.broadcasted_iota(jnp.int32, (L, L), 0)
    s_i = lax.broadcasted_iota(jnp.int32, (L, L), 1)
    causal = s_i <= t_i
    tri = causal.astype(BF16)
    csum = _dot_f32_rhs(tri, lif)
    lib = jnp.where(lane8 < M_HEADS, lif, csum)
    lib_t = _transpose_f32(lib, _eye(2 * M_HEADS))
    eye_h = _eye(M_HEAD_DIM)
    last = n_last - 1

    heads = range(M_HEADS)
    sl = [slice(h * M_HEAD_DIM, (h + 1) * M_HEAD_DIM) for h in heads]
    q = [qk[:, sl[h]] for h in heads]
    k = [qk[:, M_WIDTH + h * M_HEAD_DIM: M_WIDTH + (h + 1) * M_HEAD_DIM] * (M_HEAD_DIM ** -0.5) for h in heads]
    v = [v_ref[0, :, sl[h]] for h in heads]
    li_col = [lib[:, h:h + 1] for h in heads]
    b_col = [lib[:, M_HEADS + h:M_HEADS + h + 1] for h in heads]
    m_prev = [m_scr[0:1, h:h + 1] for h in heads]
    c_old = [c_scr[h] for h in heads]
    n_old = [n_scr[h:h + 1, :] for h in heads]
    s_qk = [_dot_nt(q[h], k[h]) for h in heads]
    q_c = [_dot(q[h], c_old[h]) for h in heads]
    dmat = [jnp.where(causal, b_col[h] - lib_t[M_HEADS + h:M_HEADS + h + 1, :] + lib_t[h:h + 1, :], NEG)
            for h in heads]
    inter = [b_col[h] + m_prev[h] for h in heads]
    m_t = [jnp.maximum(inter[h], jnp.max(dmat[h], axis=1, keepdims=True)) for h in heads]
    att = [jnp.exp(dmat[h] - m_t[h]) * s_qk[h] for h in heads]
    w_inter = [jnp.exp(inter[h] - m_t[h]) for h in heads]
    num = [_dot(att[h], v[h]) + w_inter[h] * q_c[h] for h in heads]
    den = [jnp.sum(att[h], axis=1, keepdims=True) + w_inter[h] * jnp.sum(q[h] * n_old[h], axis=1, keepdims=True)
           for h in heads]
    for h in heads:
        hh = num[h] / jnp.maximum(jnp.abs(den[h]), jnp.exp(-m_t[h]))
        y_ref[0, :, sl[h]] = _sigmoid(o_ref[0, :, sl[h]]) * hh

    m_new = [m_t[h][last:last + 1, :] for h in heads]
    b_last = [b_col[h][last:last + 1, :] for h in heads]
    scale = [jnp.exp(b_last[h] + m_prev[h] - m_new[h]) for h in heads]
    kw = [k[h] * jnp.exp(b_last[h] - b_col[h] + li_col[h] - m_new[h]) for h in heads]
    kw_t = [lax.dot_general(eye_h, kw[h].astype(BF16), DOT_NT, preferred_element_type=F32)
            for h in heads]
    for h in heads:
        c_scr[h] = scale[h] * c_old[h] + _dot(kw_t[h], v[h])
        n_scr[h:h + 1, :] = scale[h] * n_old[h] + jnp.sum(kw[h], axis=0, keepdims=True)
        m_scr[0:1, h:h + 1] = m_new[h]

    @pl.when(is_last_chunk)
    def _():
        c_out_ref[0] = c_scr[...]
        n_out_ref[0] = n_scr[...]
        m_out_ref[0] = m_scr[...]


def mlstm(m_qk, m_v, m_o, m_if, conv0, c0, n0, m0, conv_w, conv_b, if_bias, chunk):
    B, T, _ = m_qk.shape
    nc = -(-T // chunk)
    tp = nc * chunk
    n_last = T - (nc - 1) * chunk
    if tp != T:
        padt = lambda a: jnp.pad(a, ((0, 0), (0, tp - T), (0, 0)))
        m_qk, m_v, m_o, m_if = padt(m_qk), padt(m_v), padt(m_o), padt(m_if)
    conv0p = jnp.pad(conv0, ((0, 0), (SUBLANES - (M_CONV - 1), 0), (0, 0)))
    n0p = jnp.pad(n0, ((0, 0), (0, SUBLANES - M_HEADS), (0, 0)))
    m0p = jnp.pad(m0, ((0, 0), (0, LANES - M_HEADS))).reshape(B, 1, LANES)
    cw = jnp.pad(conv_w, ((0, SUBLANES - M_CONV), (0, 0)))
    tile = lambda w: pl.BlockSpec((1, chunk, w), lambda b, c: (b, c, 0))
    perb = lambda *s: pl.BlockSpec((1,) + s, lambda b, c: (b,) + (0,) * len(s))
    outs = pl.pallas_call(
        functools.partial(_mlstm_kernel, chunk=chunk, n_last=n_last),
        grid=(B, nc),
        in_specs=[tile(2 * M_WIDTH), tile(M_WIDTH), tile(M_WIDTH), tile(2 * M_HEADS),
                  perb(SUBLANES, 2 * M_WIDTH), perb(M_HEADS, M_HEAD_DIM, M_HEAD_DIM),
                  perb(SUBLANES, M_HEAD_DIM), perb(1, LANES),
                  _resident((SUBLANES, 2 * M_WIDTH)), _resident((1, 2 * M_WIDTH)),
                  _resident((1, 2 * M_HEADS))],
        out_specs=[tile(M_WIDTH), perb(SUBLANES, 2 * M_WIDTH),
                   perb(M_HEADS, M_HEAD_DIM, M_HEAD_DIM), perb(SUBLANES, M_HEAD_DIM), perb(1, LANES)],
        out_shape=[jax.ShapeDtypeStruct((B, tp, M_WIDTH), F32),
                   jax.ShapeDtypeStruct((B, SUBLANES, 2 * M_WIDTH), F32),
                   jax.ShapeDtypeStruct((B, M_HEADS, M_HEAD_DIM, M_HEAD_DIM), F32),
                   jax.ShapeDtypeStruct((B, SUBLANES, M_HEAD_DIM), F32),
                   jax.ShapeDtypeStruct((B, 1, LANES), F32)],
        scratch_shapes=[pltpu.VMEM((chunk + SUBLANES, 2 * M_WIDTH), F32),
                        pltpu.VMEM((M_HEADS, M_HEAD_DIM, M_HEAD_DIM), F32),
                        pltpu.VMEM((SUBLANES, M_HEAD_DIM), F32),
                        pltpu.VMEM((1, LANES), F32)],
        compiler_params=_cparams("parallel", "arbitrary"), name="mlstm",
    )(m_qk, m_v, m_o, m_if, conv0p, c0, n0p, m0p, cw, conv_b.reshape(1, -1), if_bias.reshape(1, -1))
    y, conv_o, c_o, n_o, m_o_ = outs
    return (y[:, :T], conv_o[:, SUBLANES - (M_CONV - 1):], c_o, n_o[:, :M_HEADS], m_o_[:, 0, :M_HEADS])


def _rwkv_prep_kernel(cols_ref, shift0_ref, mu_ref, w0_ref, w2_ref, a0_ref, a2_ref, g2_ref,
                      kk_ref, ka_ref, rk_ref,
                      r_o, w_o, k_o, v_o, a_o, b_o, g_o, bonus_o, carry):
    t = pl.program_id(1)

    @pl.when(t == 0)
    def _():
        carry[...] = shift0_ref[0]

    cols = cols_ref[0]
    tm = cols.shape[0]
    row = lax.broadcasted_iota(jnp.int32, cols.shape, 0)
    prev = jnp.where(row == 0, carry[...], pltpu.roll(cols, 1, 0))
    carry[...] = cols[tm - 1:tm, :]
    mixed = cols + (prev - cols) * mu_ref[...]
    W = R_WIDTH
    r = mixed[:, :W]
    k = mixed[:, W:2 * W]
    v = mixed[:, 2 * W:3 * W]
    o = 3 * W
    wd = mixed[:, o:o + R_DECAY_RANK]
    ad = mixed[:, o + R_DECAY_RANK:o + R_DECAY_RANK + R_A_RANK]
    gd = mixed[:, o + R_DECAY_RANK + R_A_RANK:]
    w = -_softplus(-(w0_ref[...] + _dot(jnp.tanh(wd), w2_ref[...]))) - 0.5
    a = _sigmoid(a0_ref[...] + _dot(ad, a2_ref[...]))
    g_o[0] = _dot(_sigmoid(gd), g2_ref[...])
    kk = k * kk_ref[...]
    k2 = k * (1.0 + (a - 1.0) * ka_ref[...])
    rkk = r * k2 * rk_ref[...]
    r_o[0] = r
    w_o[0] = -jnp.exp(w)
    k_o[0] = k2
    v_o[0] = v
    for h in range(R_HEADS):
        sl = slice(h * R_HEAD_DIM, (h + 1) * R_HEAD_DIM)
        kkh = kk[:, sl]
        kkh = kkh * lax.rsqrt(jnp.maximum(jnp.sum(kkh * kkh, axis=1, keepdims=True), 1e-24))
        a_o[0, :, sl] = -kkh
        b_o[0, :, sl] = kkh * a[:, sl]
        bonus_o[0, :, sl] = jnp.sum(rkk[:, sl], axis=1, keepdims=True) * v[:, sl]


DOT_NN = (((1,), (0,)), ((), ()))
DOT_NT = (((1,), (1,)), ((), ()))
DOT_TN = (((0,), (0,)), ((), ()))


def _dot_x3(x, y, dims):
    xh = x.astype(BF16)
    xl = (x - xh.astype(F32)).astype(BF16)
    yh = y.astype(BF16)
    yl = (y - yh.astype(F32)).astype(BF16)
    d = lambda p, q: lax.dot_general(p, q, dims, preferred_element_type=F32)
    return d(xh, yh) + d(xh, yl) + d(xl, yh)


def _dot_tn(a, b):
    return lax.dot_general(a.astype(BF16), b.astype(BF16), DOT_TN, preferred_element_type=F32)


def _rwkv_chunk_kernel(r_ref, lw_ref, k_ref, v_ref, a_ref, b_ref, bonus_ref, g_ref, lng_ref, lnb_ref,
                       s0_ref, o_ref, s_out_ref, s_scr, *, chunk, n_valid):
    t = pl.program_id(1)
    nt = pl.num_programs(1)
    L, N = chunk, R_HEAD_DIM

    @pl.when(t == 0)
    def _():
        s_scr[...] = s0_ref[0]

    r, lw, k, v, a, b = r_ref[0], lw_ref[0], k_ref[0], v_ref[0], a_ref[0], b_ref[0]
    if n_valid % L:
        dead = (t * L + lax.broadcasted_iota(jnp.int32, (L, R_WIDTH), 0)) >= n_valid
        lw, k, v, a, b = (jnp.where(dead, 0.0, z) for z in (lw, k, v, a, b))
    ti = lax.broadcasted_iota(jnp.int32, (L, L), 0)
    si = lax.broadcasted_iota(jnp.int32, (L, L), 1)
    incl = si <= ti
    strict = si < ti
    eye = (si == ti).astype(F32)
    c = _dot_f32_rhs(incl.astype(BF16), lw)
    c_end = c[L - 1:L, :]
    a_hat = a * jnp.exp(c - lw)
    r_hat = r * jnp.exp(c)
    e_neg = jnp.exp(-c)
    b_til = b * e_neg
    k_til = k * e_neg
    e_end = jnp.exp(c_end - c)
    b_bar = b * e_end
    k_bar = k * e_end
    gam = jnp.exp(c_end)

    heads = range(R_HEADS)
    sl = [slice(h * N, (h + 1) * N) for h in heads]
    ah = [a_hat[:, sl[h]] for h in heads]
    rh = [r_hat[:, sl[h]] for h in heads]
    bt = [b_til[:, sl[h]] for h in heads]
    kt = [k_til[:, sl[h]] for h in heads]
    vh = [v[:, sl[h]] for h in heads]
    a_ab = [jnp.where(strict, _dot_x3(ah[h], bt[h], DOT_NT), 0.0) for h in heads]
    a_ak = [jnp.where(strict, _dot_nt(ah[h], kt[h]), 0.0) for h in heads]
    a_rb = [jnp.where(incl, _dot_nt(rh[h], bt[h]), 0.0) for h in heads]
    a_rk = [jnp.where(incl, _dot_nt(rh[h], kt[h]), 0.0) for h in heads]
    inv = [eye + a_ab[h] for h in heads]
    pw = a_ab
    span = 2
    while span < L:
        pw = [_dot_x3(pw[h], pw[h], DOT_NN) for h in heads]
        inv = [inv[h] + _dot_x3(inv[h], pw[h], DOT_NN) for h in heads]
        span *= 2
    s = [s_scr[h] for h in heads]
    rhs = [_dot_nt(ah[h], s[h]) + _dot(a_ak[h], vh[h]) for h in heads]
    u = [_dot_x3(inv[h], rhs[h], DOT_NN) for h in heads]
    y = [_dot_nt(rh[h], s[h]) + _dot(a_rb[h], u[h]) + _dot(a_rk[h], vh[h]) for h in heads]
    for h in heads:
        s_scr[h] = s[h] * gam[:, sl[h]] + _dot_tn(u[h], b_bar[:, sl[h]]) + _dot_tn(vh[h], k_bar[:, sl[h]])
    for h in heads:
        mean = jnp.mean(y[h], axis=1, keepdims=True)
        yc = y[h] - mean
        var = jnp.mean(yc * yc, axis=1, keepdims=True)
        yn = yc * lax.rsqrt(var + LN_X_EPS) * lng_ref[:, sl[h]] + lnb_ref[:, sl[h]]
        o_ref[0, :, sl[h]] = (yn + bonus_ref[0, :, sl[h]]) * g_ref[0, :, sl[h]]

    @pl.when(t == nt - 1)
    def _():
        s_out_ref[0] = s_scr[...]


def rwkv(cols, shift0, s0, p, tm, chunk):
    B, T, _ = cols.shape
    tp = -(-T // tm) * tm
    assert tm % chunk == 0 and chunk % SUBLANES == 0 and chunk & (chunk - 1) == 0
    colsp = jnp.pad(cols, ((0, 0), (0, tp - T), (0, 0))) if tp != T else cols
    nt = tp // tm
    vec = lambda a: a.reshape(1, -1)
    consts = [vec(p['rwkv_mu']), vec(p['rwkv_w0']), p['rwkv_w2'].astype(BF16), vec(p['rwkv_a0']),
              p['rwkv_a2'].astype(BF16), p['rwkv_g2'].astype(BF16), vec(p['rwkv_k_k']),
              vec(p['rwkv_k_a']), vec(p['rwkv_r_k'])]
    fl = jax.ShapeDtypeStruct((B, tp, R_WIDTH), F32)
    fl_spec = lambda n, w: pl.BlockSpec((1, n, w), lambda b, t: (b, t, 0))
    r, lw, k, v, a, bb, g, bonus = pl.pallas_call(
        _rwkv_prep_kernel, grid=(B, nt),
        in_specs=[fl_spec(tm, R_COLS), pl.BlockSpec((1, 1, R_COLS), lambda b, t: (b, 0, 0))]
                 + [_resident(c.shape) for c in consts],
        out_specs=[fl_spec(tm, R_WIDTH)] * 8, out_shape=[fl] * 8,
        scratch_shapes=[pltpu.VMEM((1, R_COLS), F32)],
        compiler_params=_cparams("parallel", "arbitrary"), name="rwkv_prep",
    )(colsp, shift0, *consts)
    st_spec = pl.BlockSpec((1, R_HEADS, R_HEAD_DIM, R_HEAD_DIM), lambda b, t: (b, 0, 0, 0))
    out, s_new = pl.pallas_call(
        functools.partial(_rwkv_chunk_kernel, chunk=chunk, n_valid=T),
        grid=(B, tp // chunk),
        in_specs=[fl_spec(chunk, R_WIDTH)] * 8 + [_resident((1, R_WIDTH))] * 2 + [st_spec],
        out_specs=[fl_spec(chunk, R_WIDTH), st_spec],
        out_shape=[fl, jax.ShapeDtypeStruct(s0.shape, F32)],
        scratch_shapes=[pltpu.VMEM((R_HEADS, R_HEAD_DIM, R_HEAD_DIM), F32)],
        compiler_params=_cparams("parallel", "arbitrary"), name="rwkv_chunk",
    )(r, lw, k, v, a, bb, bonus, g, vec(p['rwkv_ln_g']), vec(p['rwkv_ln_b']), s0)
    return out[:, :T], s_new


def compress_weights(pe, w1, w2):
    d, hdim, G = A_HEAD_DIM, CMP_HIDDEN, A_KV_HEADS
    w1r = w1.reshape(2, CMP_STRIDE, d, hdim)
    wl = jnp.zeros((CMP_STRIDE, G, d, 2, G, hdim), F32)
    for g in range(G):
        wl = wl.at[:, g, :, :, g, :].set(jnp.transpose(w1r, (1, 2, 0, 3)))
    wl = wl.reshape(CMP_STRIDE, G * d, 2 * G * hdim).astype(BF16)
    wpe = jnp.transpose(w1r, (1, 0, 2, 3)).reshape(CMP_STRIDE, 2 * d, hdim)
    wpe = jnp.concatenate([wpe] * G, axis=-1).astype(BF16)
    pe_l = jnp.transpose(pe.reshape(2, CMP_STRIDE, d), (1, 0, 2)).reshape(CMP_STRIDE, 1, 2 * d)
    pe_l = jnp.broadcast_to(pe_l, (CMP_STRIDE, SUBLANES, 2 * d))
    w2bd = jnp.zeros((G, hdim, G, d), F32)
    for g in range(G):
        w2bd = w2bd.at[g, :, g, :].set(w2)
    return wl, wpe, pe_l, w2bd.reshape(G * hdim, G * d).astype(BF16)


def _compress_halves(strided_rows, wl_ref, n):
    acc = None
    for l in range(CMP_STRIDE):
        part = _dot(strided_rows(l), wl_ref[l])
        acc = part if acc is None else acc + part
    return acc


def _compress_finish(halves, wpe_ref, pe_ref, w2_ref):
    n = halves.shape[0]
    gh = A_KV_HEADS * CMP_HIDDEN
    pe_term = None
    for l in range(CMP_STRIDE):
        part = _dot(pe_ref[l], wpe_ref[l])
        pe_term = part if pe_term is None else pe_term + part
    second = pltpu.roll(halves[:, gh:], n - 1, 0)
    hid = _silu(halves[:, :gh] + second + pe_term[0:1, :])
    return _dot(hid, w2_ref[...])


def _compress_kernel(x_ref, wl_ref, wpe_ref, pe_ref, w2_ref, o_ref):
    n = o_ref.shape[1]
    rows = lambda l: x_ref[0, pl.ds(l, n, stride=CMP_STRIDE), :]
    o_ref[0] = _compress_finish(_compress_halves(rows, wl_ref, n), wpe_ref, pe_ref, w2_ref)


def compress_rows(rows, cw):
    B, T, W = rows.shape
    n = T // CMP_STRIDE
    return pl.pallas_call(
        _compress_kernel, grid=(B,),
        in_specs=[pl.BlockSpec((1, T, W), lambda b: (b, 0, 0))] + [_resident(c.shape) for c in cw],
        out_specs=pl.BlockSpec((1, n, W), lambda b: (b, 0, 0)),
        out_shape=jax.ShapeDtypeStruct((B, n, W), F32),
        compiler_params=_cparams("parallel"), name="nsa_compress",
    )(rows, *cw)


def _topk_mask_columns(rank, k):
    n_entries = rank.shape[0]
    row = lax.broadcasted_iota(jnp.int32, rank.shape, 0).astype(F32)
    sel = jnp.zeros(rank.shape, F32)
    for _ in range(k):
        m = jnp.max(rank, axis=0, keepdims=True)
        first = jnp.min(jnp.where(rank == m, row, float(n_entries)), axis=0, keepdims=True)
        hit = row == first
        sel = jnp.where(hit, 1.0, sel)
        rank = jnp.where(hit, LOWEST, rank)
    return sel


def _first_max(rank):
    lane = lax.broadcasted_iota(jnp.int32, rank.shape, 1).astype(F32)
    m = jnp.max(rank, axis=1, keepdims=True)
    first = jnp.min(jnp.where(rank == m, lane, float(rank.shape[1])), axis=1, keepdims=True)
    return lane == first, first


def _flash_update(state, s, bias, v):
    m_old, l_old, acc = state
    s = s + bias
    m_new = jnp.maximum(m_old, jnp.max(s, axis=1, keepdims=True))
    p = jnp.exp(s - m_new)
    alpha = jnp.exp(m_old - m_new)
    return (m_new, alpha * l_old + jnp.sum(p, axis=1, keepdims=True), alpha * acc + _dot(p, v))


def _nsa_prompt_kernel(q_ref, gate_ref, kc_ref, vc_ref, ks_ref, vs_ref, kw_ref, vw_ref, covt_ref,
                       o_ref, *, seq, tk, wk):
    i = pl.program_id(1)
    QB, d, J = Q_BLOCK, A_HEAD_DIM, A_GROUP
    nc = kc_ref.shape[1]
    nb = covt_ref.shape[0]
    st = i * QB
    q = q_ref[0] * A_SCALE
    gate = _sigmoid(gate_ref[0])
    rows = J * QB
    t_q = st + lax.broadcasted_iota(jnp.int32, (QB, 1), 0)
    t_rows = jnp.concatenate([t_q] * J, axis=0)
    blk_end = lax.broadcasted_iota(jnp.int32, (rows, nc), 1) * CMP_STRIDE + (CMP_BLOCK - 1)
    cmp_mask = blk_end <= t_rows
    jj = lax.broadcasted_iota(jnp.int32, (nb, QB), 0)
    cur = (st + lax.broadcasted_iota(jnp.int32, (1, QB), 1)) // SEL_BLOCK
    forced = (jj == 0) | (jj == cur) | (jj == cur - 1)
    zeros = lambda w: jnp.zeros((rows, w), F32)

    qgs, o_cmps, ranks = [], [], []
    for g in range(A_KV_HEADS):
        gl = slice(g * d, (g + 1) * d)
        qg = jnp.concatenate([q[:, (g * J + j) * d:(g * J + j + 1) * d] for j in range(J)],
                             axis=0).astype(BF16)
        s = jnp.where(cmp_mask, _dot_nt(qg, kc_ref[0, :, gl]), NEG)
        e = jnp.exp(s - jnp.max(s, axis=1, keepdims=True))
        p = jnp.where(cmp_mask, e / jnp.sum(e, axis=1, keepdims=True), 0.0)
        p_grp = p[0:QB]
        for j in range(1, J):
            p_grp = p_grp + p[j * QB:(j + 1) * QB]
        score = _dot_f32_rhs_nt(covt_ref[...], p_grp)
        qgs.append(qg)
        o_cmps.append(_dot(p, vc_ref[0, :, gl]))
        ranks.append(jnp.where(jj <= cur, jnp.where(forced, FORCE, score), NEG))
    sel_all = _topk_mask_columns(jnp.concatenate(ranks, axis=1), min(N_SEL, nb)).astype(BF16)

    k0w = pl.multiple_of(jnp.clip(st + QB - wk, 0, seq - wk), QB)
    rel = t_q - (k0w + lax.broadcasted_iota(jnp.int32, (1, wk), 1))
    win_bias = jnp.concatenate([jnp.where((rel >= 0) & (rel < WINDOW), 0.0, NEG)] * J, axis=0)

    for g in range(A_KV_HEADS):
        gl = slice(g * d, (g + 1) * d)
        qg, o_cmp = qgs[g], o_cmps[g]
        sel = sel_all[:, g * QB:(g + 1) * QB]

        def sel_tile(kt, state):
            k0 = pl.multiple_of(kt * tk, tk)
            s_pos = k0 + lax.broadcasted_iota(jnp.int32, (1, tk), 1)
            blk_of = k0 // SEL_BLOCK + lax.broadcasted_iota(jnp.int32, (nb, tk), 1) // SEL_BLOCK
            expand = (lax.broadcasted_iota(jnp.int32, (nb, tk), 0) == blk_of).astype(BF16)
            chosen = lax.dot_general(sel, expand, DOT_TN, preferred_element_type=F32) > 0.5
            bias = jnp.where(chosen & (s_pos <= t_q), 0.0, NEG)
            sc = _dot_nt(qg, ks_ref[0, pl.ds(k0, tk), gl])
            return _flash_update(state, sc, jnp.concatenate([bias] * J, axis=0), vs_ref[0, pl.ds(k0, tk), gl])

        init = (jnp.full((rows, 1), NEG, F32), jnp.zeros((rows, 1), F32), zeros(d))
        _, l_sel, acc_sel = lax.fori_loop(0, (st + QB - 1) // tk + 1, sel_tile, init)
        o_sel = acc_sel / l_sel

        sw = _dot_nt(qg, kw_ref[0, pl.ds(k0w, wk), gl]) + win_bias
        pw = jnp.exp(sw - jnp.max(sw, axis=1, keepdims=True))
        o_win = _dot(pw, vw_ref[0, pl.ds(k0w, wk), gl]) / jnp.sum(pw, axis=1, keepdims=True)

        for j in range(J):
            h = g * J + j
            rs = slice(j * QB, (j + 1) * QB)
            o_ref[0, :, h * d:(h + 1) * d] = (gate[:, h:h + 1] * o_cmp[rs]
                                              + gate[:, A_HEADS + h:A_HEADS + h + 1] * o_sel[rs]
                                              + gate[:, 2 * A_HEADS + h:2 * A_HEADS + h + 1] * o_win[rs])


def _covers(n_cmp, n_blocks):
    i = np.arange(n_cmp)[:, None]
    j = np.arange(n_blocks)[None, :]
    return (i * CMP_STRIDE < (j + 1) * SEL_BLOCK) & (i * CMP_STRIDE + CMP_BLOCK > j * SEL_BLOCK)


def nsa_prompt_attend(a_q, a_gate, kc, vc, ks, vs, kw, vw):
    B, T, _ = a_q.shape
    nc = kc.shape[1]
    nb = -(-T // SEL_BLOCK)
    tk = min(4 * Q_BLOCK, T)
    wk = min(WINDOW + Q_BLOCK, T)
    cov = jnp.asarray(_covers(nc, nb).T, BF16)
    tile = lambda w: pl.BlockSpec((1, Q_BLOCK, w), lambda b, i: (b, i, 0))
    full = lambda n, w: pl.BlockSpec((1, n, w), lambda b, i: (b, 0, 0))
    W = A_KV_LANES
    return pl.pallas_call(
        functools.partial(_nsa_prompt_kernel, seq=T, tk=tk, wk=wk),
        grid=(B, T // Q_BLOCK),
        in_specs=[tile(A_WIDTH), tile(3 * A_HEADS), full(nc, W), full(nc, W),
                  full(T, W), full(T, W), full(T, W), full(T, W), _resident(cov.shape)],
        out_specs=tile(A_WIDTH),
        out_shape=jax.ShapeDtypeStruct((B, T, A_WIDTH), F32),
        compiler_params=_cparams("parallel", "arbitrary"), name="nsa_prompt",
    )(a_q, a_gate, kc, vc, ks, vs, kw, vw, cov)


GATHER_PAGES = 16
CHUNKS_PER_PAGE = PAGE_SIZE // CMP_STRIDE
HEAD_ROWS = SUBLANES


def channel_major(cache):
    L, n, tok, G, d = cache.shape
    return jnp.transpose(cache, (0, 1, 3, 4, 2)).reshape(L, n, G * d, tok)


def _nsa_sample_cmp_kernel(pt_ref, *refs, past_len):
    del pt_ref
    ng = GATHER_PAGES
    pk_refs, pv_refs = refs[:ng], refs[ng:2 * ng]
    (q_ref, wink_ref, winv_ref, kwn_ref, vwn_ref, wlk_ref, wpek_ref, pek_ref, w2k_ref,
     wlv_ref, wpev_ref, pev_ref, w2v_ref, cov_ref, ocmp_ref, owin_ref, idx_ref,
     hk_scr, hv_scr, xk_scr, xv_scr) = refs[2 * ng:]
    c = pl.program_id(1)
    page = pk_refs[0].shape[3]
    rows_per_step = ng * CHUNKS_PER_PAGE
    base = pl.multiple_of(c * rows_per_step, rows_per_step)
    for page_refs, xs, wl_ref, h_scr in ((pk_refs, xk_scr, wlk_ref, hk_scr), (pv_refs, xv_scr, wlv_ref, hv_scr)):
        for i in range(ng):
            xs[i * page:(i + 1) * page, :] = page_refs[i][0, 0].T
        rows = lambda l, xs=xs: xs[pl.ds(l, rows_per_step, stride=CMP_STRIDE), :]
        h_scr[pl.ds(base, rows_per_step), :] = _compress_halves(rows, wl_ref, rows_per_step)

    @pl.when(c == pl.num_programs(1) - 1)
    def _():
        d, R8 = A_HEAD_DIM, HEAD_ROWS
        kc = _compress_finish(hk_scr[...], wpek_ref, pek_ref, w2k_ref)
        vc = _compress_finish(hv_scr[...], wpev_ref, pev_ref, w2v_ref)
        nc = kc.shape[0]
        nb = cov_ref.shape[1]
        keep = wink_ref.shape[3]
        q_pos = past_len
        blk_end = lax.broadcasted_iota(jnp.int32, (R8, nc), 1) * CMP_STRIDE + (CMP_BLOCK - 1)
        cmp_mask = (blk_end <= q_pos) & (lax.broadcasted_iota(jnp.int32, (R8, nc), 1) < nc - 1)
        real_head = lax.broadcasted_iota(jnp.int32, (R8, nc), 0) < A_GROUP
        row8 = lax.broadcasted_iota(jnp.int32, (R8, nb), 0)
        jj = lax.broadcasted_iota(jnp.int32, (R8, nb), 1)
        cur = q_pos // SEL_BLOCK
        forced = (jj == 0) | (jj == cur) | (jj == cur - 1)
        k_idx = lax.broadcasted_iota(jnp.int32, (R8, keep), 1)
        rel = q_pos - (past_len - keep + k_idx)
        win_mask = (rel >= 0) & (rel < WINDOW)
        rank = None
        for g in range(A_KV_HEADS):
            gl = slice(g * d, (g + 1) * d)
            qg = q_ref[0, g] * A_SCALE
            s = jnp.where(cmp_mask, _dot_nt(qg, kc[:, gl]), NEG)
            e = jnp.exp(s - jnp.max(s, axis=1, keepdims=True))
            p = jnp.where(cmp_mask, e / jnp.sum(e, axis=1, keepdims=True), 0.0)
            ocmp_ref[0, g] = _dot(p, vc[:, gl])
            p_grp = jnp.sum(jnp.where(real_head, p, 0.0), axis=0, keepdims=True)
            score = _dot_f32_lhs(jnp.broadcast_to(p_grp, (R8, nc)), cov_ref[...])
            rank_g = jnp.where(jj <= cur, jnp.where(forced, FORCE, score), NEG)
            rank = rank_g if rank is None else jnp.where(row8 == g, rank_g, rank)
            sw = jnp.where(win_mask, _dot(qg, wink_ref[0, 0, gl, :]), NEG)
            s_new = jnp.sum(qg * kwn_ref[0, :, gl], axis=1, keepdims=True)
            m = jnp.maximum(jnp.max(sw, axis=1, keepdims=True), s_new)
            ew = jnp.where(win_mask, jnp.exp(sw - m), 0.0)
            e_new = jnp.exp(s_new - m)
            owin_ref[0, g] = ((_dot_nt(ew, winv_ref[0, 0, gl, :]) + e_new * vwn_ref[0, :, gl])
                              / (jnp.sum(ew, axis=1, keepdims=True) + e_new))
        out_lane = lax.broadcasted_iota(jnp.int32, (R8, LANES), 1)
        picks = jnp.zeros((R8, LANES), jnp.int32)
        for it in range(N_SEL - 1):
            hit, idx = _first_max(rank)
            picks = jnp.where(out_lane == it, idx.astype(jnp.int32), picks)
            rank = jnp.where(hit, LOWEST, rank)
        idx_ref[0] = picks


def nsa_sample_cmp(page_table, pool_k, pool_v, q_hm, win_k, win_v, layer, kw_new, vw_new, cwk, cwv, past_len):
    B, n_pages = page_table.shape
    _, n_pool, W, page = pool_k.shape
    ng = GATHER_PAGES
    assert n_pages % ng == 0
    nc = n_pages * CHUNKS_PER_PAGE
    nb = past_len // SEL_BLOCK
    cov = jnp.asarray(_covers(nc, nb), BF16)
    keep = win_k.shape[3]
    page_spec = lambda i: pl.BlockSpec((1, 1, W, page), lambda b, c, pt: (layer, pt[b, c * ng + i], 0, 0))
    perb = lambda *s: pl.BlockSpec((1,) + s, lambda b, c, pt: (b,) + (0,) * len(s))
    win_spec = pl.BlockSpec((1, 1, W, keep), lambda b, c, pt: (layer, b, 0, 0))
    const = lambda a: pl.BlockSpec(a.shape, lambda b, c, pt: (0,) * a.ndim)
    consts = [*cwk, *cwv, cov]
    hw = cwk[0].shape[2]
    hm = jax.ShapeDtypeStruct((B, A_KV_HEADS, HEAD_ROWS, A_HEAD_DIM), F32)
    grid_spec = pltpu.PrefetchScalarGridSpec(
        num_scalar_prefetch=1, grid=(B, n_pages // ng),
        in_specs=[page_spec(i) for i in range(ng)] * 2
                 + [perb(A_KV_HEADS, HEAD_ROWS, A_HEAD_DIM), win_spec, win_spec, perb(1, W), perb(1, W)]
                 + [const(a) for a in consts],
        out_specs=[perb(A_KV_HEADS, HEAD_ROWS, A_HEAD_DIM), perb(A_KV_HEADS, HEAD_ROWS, A_HEAD_DIM),
                   perb(HEAD_ROWS, LANES)],
        scratch_shapes=[pltpu.VMEM((nc, hw), F32), pltpu.VMEM((nc, hw), F32),
                        pltpu.VMEM((ng * page, W), F32), pltpu.VMEM((ng * page, W), F32)])
    return pl.pallas_call(
        functools.partial(_nsa_sample_cmp_kernel, past_len=past_len), grid_spec=grid_spec,
        out_shape=[hm, hm, jax.ShapeDtypeStruct((B, HEAD_ROWS, LANES), jnp.int32)],
        compiler_params=_cparams("parallel", "arbitrary"), name="nsa_sample_cmp",
    )(page_table, *([pool_k] * ng), *([pool_v] * ng), q_hm, win_k, win_v, kw_new, vw_new, *consts)


SEL_PICKS_PER_STEP = 5


def _nsa_sample_sel_kernel(pt_ref, pk_ref, *refs, npk):
    del pt_ref
    P = SEL_PICKS_PER_STEP
    n_pages = A_KV_HEADS * P
    k_refs, v_refs = refs[:n_pages], refs[n_pages:2 * n_pages]
    q_ref, gate_ref, ksn_ref, vsn_ref, ocmp_ref, owin_ref, o_ref, m_scr, l_scr, acc_scr = refs[2 * n_pages:]
    b = pl.program_id(0)
    c = pl.program_id(1)
    d = A_HEAD_DIM
    page = k_refs[0].shape[3]
    tok_blk = lax.broadcasted_iota(jnp.int32, (HEAD_ROWS, page), 1) // SEL_BLOCK
    for g in range(A_KV_HEADS):
        gl = slice(g * d, (g + 1) * d)
        qg = q_ref[0, g] * A_SCALE

        @pl.when(c == 0)
        def _():
            m_scr[g] = jnp.sum(qg * ksn_ref[0, :, gl], axis=1, keepdims=True)
            l_scr[g] = jnp.ones((HEAD_ROWS, 1), F32)
            acc_scr[g] = jnp.broadcast_to(vsn_ref[0, :, gl], (HEAD_ROWS, d))

        chosen = [tok_blk == pk_ref[b, g * npk + c * P + i] % (page // SEL_BLOCK) for i in range(P)]
        s = [jnp.where(chosen[i], _dot(qg, k_refs[g * P + i][0, 0, gl, :]), NEG) for i in range(P)]
        m_old = m_scr[g]
        m = m_old
        for i in range(P):
            m = jnp.maximum(m, jnp.max(s[i], axis=1, keepdims=True))
        alpha = jnp.exp(m_old - m)
        l = alpha * l_scr[g]
        acc = alpha * acc_scr[g]
        for i in range(P):
            p = jnp.where(chosen[i], jnp.exp(s[i] - m), 0.0)
            l = l + jnp.sum(p, axis=1, keepdims=True)
            acc = acc + _dot_nt(p, v_refs[g * P + i][0, 0, gl, :])
        m_scr[g] = m
        l_scr[g] = l
        acc_scr[g] = acc

        @pl.when(c == pl.num_programs(1) - 1)
        def _():
            gate = _sigmoid(gate_ref[0, g])
            o_ref[0, g] = (gate[:, 0:1] * ocmp_ref[0, g] + gate[:, 1:2] * (acc / l)
                           + gate[:, 2:3] * owin_ref[0, g])


def nsa_sample_sel(page_table, picks, pool_k, pool_v, layer, q_hm, gate_hm, ks_new, vs_new, o_cmp, o_win):
    B = page_table.shape[0]
    _, n_pool, W, page = pool_k.shape
    per_page = page // SEL_BLOCK
    npk = picks.shape[2]
    P = SEL_PICKS_PER_STEP
    assert npk % P == 0
    picks = picks.reshape(B, A_KV_HEADS * npk)

    def blk(g, i):
        def index(b, c, pt, pk):
            return (layer, pt[b, pk[b, g * npk + c * P + i] // per_page], 0, 0)
        return pl.BlockSpec((1, 1, W, page), index)

    pages = [blk(g, i) for g in range(A_KV_HEADS) for i in range(P)]
    perb = lambda *s: pl.BlockSpec((1,) + s, lambda b, c, pt, pk: (b,) + (0,) * len(s))
    hm_spec = perb(A_KV_HEADS, HEAD_ROWS, A_HEAD_DIM)
    grid_spec = pltpu.PrefetchScalarGridSpec(
        num_scalar_prefetch=2, grid=(B, npk // P),
        in_specs=pages * 2 + [hm_spec, perb(A_KV_HEADS, HEAD_ROWS, 3), perb(1, W), perb(1, W), hm_spec, hm_spec],
        out_specs=hm_spec,
        scratch_shapes=[pltpu.VMEM((A_KV_HEADS, HEAD_ROWS, 1), F32), pltpu.VMEM((A_KV_HEADS, HEAD_ROWS, 1), F32),
                        pltpu.VMEM((A_KV_HEADS, HEAD_ROWS, A_HEAD_DIM), F32)])
    return pl.pallas_call(
        functools.partial(_nsa_sample_sel_kernel, npk=npk), grid_spec=grid_spec,
        out_shape=jax.ShapeDtypeStruct((B, A_KV_HEADS, HEAD_ROWS, A_HEAD_DIM), F32),
        compiler_params=_cparams("parallel", "arbitrary"), name="nsa_sample_sel",
    )(page_table, picks, *([pool_k] * len(pages)), *([pool_v] * len(pages)), q_hm, gate_hm, ks_new, vs_new,
      o_cmp, o_win)


def nsa_sample(a_q, a_gate, new_rows, pools, win_k, win_v, layer, page_table, cwk, cwv):
    B = a_q.shape[0]
    past_len = page_table.shape[1] * PAGE_SIZE
    _, _, ksn, vsn, kwn, vwn = new_rows
    pool_kc, pool_vc, pool_ks, pool_vs = pools
    padj = ((0, 0), (0, 0), (0, HEAD_ROWS - A_GROUP), (0, 0))
    q_hm = jnp.pad(a_q.reshape(B, A_KV_HEADS, A_GROUP, A_HEAD_DIM), padj)
    gate_hm = jnp.pad(jnp.transpose(a_gate.reshape(B, 3, A_KV_HEADS, A_GROUP), (0, 2, 3, 1)), padj)
    o_cmp, o_win, picks = nsa_sample_cmp(page_table, pool_kc, pool_vc, q_hm, win_k, win_v, layer, kwn, vwn,
                                         cwk, cwv, past_len)
    picks = picks[:, :A_KV_HEADS, :N_SEL - 1]
    o = nsa_sample_sel(page_table, picks, pool_ks, pool_vs, layer, q_hm, gate_hm, ksn, vsn, o_cmp, o_win)
    return o[:, :, :A_GROUP].reshape(B, 1, A_WIDTH)


PROJ_GROUPS = (2 * M_WIDTH, M_WIDTH, M_WIDTH, 2 * M_HEADS, A_WIDTH) + (A_KV_LANES,) * 6 + (
    3 * A_HEADS, R_COLS, 3 * D_MODEL)
PROMPT_ROW_TILE = 256
MLSTM_CHUNK = 128
RWKV_STEPS = 64


def _layer_params(l, w_in, w_branch_m, w_branch_a, w_branch_r, w_out, w_gate_up, w_down,
                  nsa_cmp_pe, nsa_cmp_w1, nsa_cmp_w2):
    offs = np.cumsum((0,) + PROJ_GROUPS)
    assert offs[-1] == w_in.shape[2]
    wi = w_in[l].astype(BF16)
    return {
        'proj': [wi[:, a:b] for a, b in zip(offs[:-1], offs[1:])],
        'wm': w_branch_m[l].astype(BF16), 'wa': w_branch_a[l].astype(BF16), 'wr': w_branch_r[l].astype(BF16),
        'wo': w_out[l].astype(BF16), 'wgu': w_gate_up[l].astype(BF16), 'wd': w_down[l].astype(BF16),
        'cwk': compress_weights(nsa_cmp_pe[l, 0], nsa_cmp_w1[l, 0], nsa_cmp_w2[l, 0]),
        'cwv': compress_weights(nsa_cmp_pe[l, 1], nsa_cmp_w1[l, 1], nsa_cmp_w2[l, 1]),
    }


def _trunk_layer(x, lw, lp, nsa_fn, m_state, r_state, tm, chunk, rwkv_tm, rwkv_steps, gf, final_norm):
    B, T, D = x.shape
    x2d = x.reshape(B * T, D)
    outs = norm_proj(x2d, lp['norm1_g'], lw['proj'], tm)
    (m_qk, m_v, m_o, m_if, a_q), new_rows, (a_gate, r_cols, merge) = outs[:5], outs[5:11], outs[11:]
    b3 = lambda a: a.reshape(B, T, a.shape[-1])
    new_rows = [b3(r) for r in new_rows]
    y_m, conv_n, c_n, n_n, m_n = mlstm(b3(m_qk), b3(m_v), b3(m_o), b3(m_if), *m_state,
                                        lp['mlstm_conv_w'], lp['mlstm_conv_b'], lp['mlstm_if_bias'], chunk)
    y_a = nsa_fn(b3(a_q), b3(a_gate), new_rows)
    r_cols = b3(r_cols)
    y_r, s_n = rwkv(r_cols, r_state[0], r_state[1], lp, rwkv_tm, rwkv_steps)
    x_new = merge_ffn(x2d, y_m.reshape(B * T, -1), y_a.reshape(B * T, -1), y_r.reshape(B * T, -1), merge,
                      lw['wm'], lw['wa'], lw['wr'], lw['wo'], lp['norm2_g'], lw['wgu'], lw['wd'], gf,
                      tm, final_norm)
    return x_new.reshape(B, T, D), new_rows, (conv_n, c_n, n_n, m_n), (r_cols[:, -1:], s_n)


def kernel(x_prompt, x_sample, cache_cmp_k, cache_cmp_v, cache_slc_k, cache_slc_v, cache_win_k, cache_win_v, state_mlstm_conv, state_mlstm_C, state_mlstm_n, state_mlstm_m, state_rwkv_shift, state_rwkv, page_table, norm1_g, w_in, mlstm_if_bias, mlstm_conv_w, mlstm_conv_b, nsa_cmp_pe, nsa_cmp_w1, nsa_cmp_w2, rwkv_mu, rwkv_w0, rwkv_w2, rwkv_a0, rwkv_a2, rwkv_g2, rwkv_k_k, rwkv_k_a, rwkv_r_k, rwkv_ln_g, rwkv_ln_b, w_branch_m, w_branch_a, w_branch_r, w_out, norm2_g, w_gate_up, w_down, final_norm_g):
    bp, tp, _ = x_prompt.shape
    bs, ts, _ = x_sample.shape
    assert ts == 1
    depth = w_in.shape[0]
    kvshape = lambda a: a.reshape(a.shape[0], a.shape[1], A_KV_HEADS, A_HEAD_DIM)
    xp, xs = x_prompt, x_sample
    pools = [channel_major(c) for c in (cache_cmp_k, cache_cmp_v, cache_slc_k, cache_slc_v)]
    win_k, win_v = channel_major(cache_win_k), channel_major(cache_win_v)
    states_p, states_s = [], []
    for l in range(depth):
        lw = _layer_params(l, w_in, w_branch_m, w_branch_a, w_branch_r, w_out, w_gate_up, w_down,
                           nsa_cmp_pe, nsa_cmp_w1, nsa_cmp_w2)
        lp = {'norm1_g': norm1_g[l], 'mlstm_if_bias': mlstm_if_bias[l], 'mlstm_conv_w': mlstm_conv_w[l],
              'mlstm_conv_b': mlstm_conv_b[l], 'rwkv_mu': rwkv_mu[l], 'rwkv_w0': rwkv_w0[l],
              'rwkv_w2': rwkv_w2[l], 'rwkv_a0': rwkv_a0[l], 'rwkv_a2': rwkv_a2[l], 'rwkv_g2': rwkv_g2[l],
              'rwkv_k_k': rwkv_k_k[l], 'rwkv_k_a': rwkv_k_a[l], 'rwkv_r_k': rwkv_r_k[l],
              'rwkv_ln_g': rwkv_ln_g[l], 'rwkv_ln_b': rwkv_ln_b[l], 'norm2_g': norm2_g[l]}
        last = l == depth - 1

        def nsa_p(a_q, a_gate, rows):
            kc = compress_rows(rows[0], lw['cwk'])
            vc = compress_rows(rows[1], lw['cwv'])
            return nsa_prompt_attend(a_q, a_gate, kc, vc, *rows[2:])

        m_zero = (jnp.zeros((bp, M_CONV - 1, 2 * M_WIDTH), F32),
                  jnp.zeros((bp, M_HEADS, M_HEAD_DIM, M_HEAD_DIM), F32),
                  jnp.zeros((bp, M_HEADS, M_HEAD_DIM), F32), jnp.zeros((bp, M_HEADS), F32))
        r_zero = (jnp.zeros((bp, 1, R_COLS), F32), jnp.zeros((bp, R_HEADS, R_HEAD_DIM, R_HEAD_DIM), F32))
        xp, rows, m_new, r_new = _trunk_layer(xp, lw, lp, nsa_p, m_zero, r_zero, PROMPT_ROW_TILE,
                                              MLSTM_CHUNK, PROMPT_ROW_TILE, RWKV_STEPS, final_norm_g, last)
        keep = min(WINDOW, tp)
        states_p.append(tuple(kvshape(r) for r in rows[:4])
                        + (kvshape(rows[4][:, tp - keep:]), kvshape(rows[5][:, tp - keep:])) + m_new + r_new)

        def nsa_s(a_q, a_gate, rows, l=l):
            return nsa_sample(a_q, a_gate, rows, pools, win_k, win_v, l, page_table, lw['cwk'], lw['cwv'])

        xs, rows, m_new, r_new = _trunk_layer(
            xs, lw, lp, nsa_s,
            (state_mlstm_conv[l], state_mlstm_C[l], state_mlstm_n[l], state_mlstm_m[l]),
            (state_rwkv_shift[l], state_rwkv[l]), bs * ts, SUBLANES, SUBLANES, SUBLANES, final_norm_g, last)
        keep = win_k.shape[3]

        def win_new(win, new_row):
            w = jnp.concatenate([win[l], jnp.swapaxes(new_row, 1, 2)], axis=2)[:, :, -keep:]
            return jnp.transpose(w.reshape(bs, A_KV_HEADS, A_HEAD_DIM, keep), (0, 3, 1, 2))

        states_s.append(tuple(kvshape(r) for r in rows[:4])
                        + (win_new(win_k, rows[4]), win_new(win_v, rows[5])) + m_new + r_new)

    stack = lambda states: [jnp.stack(a) for a in zip(*states)]
    return (xp, xs, *stack(states_p), *stack(states_s))
```

```python
import functools
import math

import numpy as np
import jax
import jax.numpy as jnp
from jax import lax
from jax.experimental import pallas as pl
from jax.experimental.pallas import tpu as pltpu

F32 = jnp.float32
BF16 = jnp.bfloat16

D_MODEL = 1024
DEPTH = 2
PAGE_SIZE = 128
M_HEADS = 4
M_HEAD_DIM = 128
M_WIDTH = M_HEADS * M_HEAD_DIM
M_CONV = 4
A_HEADS = 8
A_KV_HEADS = 2
A_HEAD_DIM = 64
A_WIDTH = A_HEADS * A_HEAD_DIM
A_GROUP = A_HEADS // A_KV_HEADS
A_SCALE = A_HEAD_DIM ** -0.5
A_KV_LANES = A_KV_HEADS * A_HEAD_DIM
CMP_STRIDE = 16
CMP_BLOCK = 2 * CMP_STRIDE
CMP_HIDDEN = A_HEAD_DIM
SEL_BLOCK = 64
N_SEL = 16
WINDOW = 512
Q_BLOCK = 128
R_HEADS = 8
R_HEAD_DIM = 64
R_WIDTH = R_HEADS * R_HEAD_DIM
R_DECAY_RANK = 64
R_A_RANK = 64
R_GATE_RANK = 128
R_COLS = 3 * R_WIDTH + R_DECAY_RANK + R_A_RANK + R_GATE_RANK
LN_X_EPS = 64e-5
D_FF = ((-(-8 * D_MODEL // 3)) + 255) // 256 * 256
IN_SPLITS = (2 * M_WIDTH, M_WIDTH, M_WIDTH, 2 * M_HEADS,
             A_WIDTH, 6 * A_KV_HEADS * A_HEAD_DIM, 3 * A_HEADS,
             R_COLS, 3 * D_MODEL)
RMS_EPS = 1e-6
NEG = -1e30
FORCE = 1e9
LOWEST = -3e38

VMEM_LIMIT_BYTES = 56 * 1024 * 1024
SUBLANES = 8
LANES = 128


def _cparams(*sem):
    return pltpu.CompilerParams(dimension_semantics=sem, vmem_limit_bytes=VMEM_LIMIT_BYTES)


def _resident(shape):
    nd = len(shape)
    return pl.BlockSpec(shape, lambda *_: (0,) * nd, pipeline_mode=pl.Buffered(1))


def _dot(a, b):
    return jnp.dot(a.astype(BF16), b.astype(BF16), preferred_element_type=F32)


def _dot_nt(a, b):
    return lax.dot_general(a.astype(BF16), b.astype(BF16), (((1,), (1,)), ((), ())),
                           preferred_element_type=F32)


def _split3(x):
    hi = x.astype(BF16)
    r = x - hi.astype(F32)
    mid = r.astype(BF16)
    lo = (r - mid.astype(F32)).astype(BF16)
    return hi, mid, lo


def _dot_f32_rhs(a_bf16, x):
    hi, mid, lo = _split3(x)
    d = lambda p: jnp.dot(a_bf16, p, preferred_element_type=F32)
    return d(hi) + d(mid) + d(lo)


def _dot_f32_rhs_nt(a_bf16, x):
    hi, mid, lo = _split3(x)
    d = lambda p: lax.dot_general(a_bf16, p, (((1,), (1,)), ((), ())), preferred_element_type=F32)
    return d(hi) + d(mid) + d(lo)


def _transpose_f32(x, eye_bf16):
    hi, mid, lo = _split3(x)
    d = lambda p: lax.dot_general(eye_bf16, p, (((1,), (1,)), ((), ())), preferred_element_type=F32)
    return d(hi) + d(mid) + d(lo)


def _eye(n, dtype=BF16):
    r = lax.broadcasted_iota(jnp.int32, (n, n), 0)
    c = lax.broadcasted_iota(jnp.int32, (n, n), 1)
    return (r == c).astype(dtype)


def _sigmoid(x):
    return 1.0 / (1.0 + jnp.exp(-x))


def _silu(x):
    return x * _sigmoid(x)


def _softplus(x):
    return jnp.maximum(x, 0.0) + jnp.log(1.0 + jnp.exp(-jnp.abs(x)))


def _rmsnorm(x, g):
    return x * lax.rsqrt(jnp.mean(x * x, axis=-1, keepdims=True) + RMS_EPS) * g


def _norm_proj_kernel(x_ref, g_ref, *refs):
    n = len(refs) // 2
    hb = _rmsnorm(x_ref[...], g_ref[...]).astype(BF16)
    for w_ref, o_ref in zip(refs[:n], refs[n:]):
        o_ref[...] = jnp.dot(hb, w_ref[...], preferred_element_type=F32)


def norm_proj(x2d, g, weights, tm):
    m, d = x2d.shape
    assert m % tm == 0
    row = lambda i: (i, 0)
    in_specs = [pl.BlockSpec((tm, d), row), _resident((1, d))]
    in_specs += [_resident(w.shape) for w in weights]
    out_specs = [pl.BlockSpec((tm, w.shape[1]), row) for w in weights]
    out_shape = [jax.ShapeDtypeStruct((m, w.shape[1]), F32) for w in weights]
    return pl.pallas_call(
        _norm_proj_kernel, grid=(m // tm,), in_specs=in_specs, out_specs=out_specs,
        out_shape=out_shape, compiler_params=_cparams("parallel"), name="norm_proj",
    )(x2d, g.reshape(1, d), *weights)


def _merge_ffn_kernel(x_ref, ym_ref, ya_ref, yr_ref, mg_ref, wm_ref, wa_ref, wr_ref, wo_ref,
                      g2_ref, wgu_ref, wd_ref, gf_ref, o_ref, *, final_norm):
    mg = mg_ref[...]
    merged = (_sigmoid(mg[:, :D_MODEL]) * _dot(ym_ref[...], wm_ref[...])
              + _sigmoid(mg[:, D_MODEL:2 * D_MODEL]) * _dot(ya_ref[...], wa_ref[...])
              + _sigmoid(mg[:, 2 * D_MODEL:]) * _dot(yr_ref[...], wr_ref[...]))
    x1 = x_ref[...] + _dot(merged, wo_ref[...])
    gu = _dot(_rmsnorm(x1, g2_ref[...]), wgu_ref[...])
    x2 = x1 + _dot(_silu(gu[:, :D_FF]) * gu[:, D_FF:], wd_ref[...])
    o_ref[...] = _rmsnorm(x2, gf_ref[...]) if final_norm else x2


def merge_ffn(x2d, ym, ya, yr, mg, wm, wa, wr, wo, g2, wgu, wd, gf, tm, final_norm):
    m, d = x2d.shape
    row = lambda i: (i, 0)
    acts = [x2d, ym, ya, yr, mg]
    consts = [wm, wa, wr, wo, g2.reshape(1, d), wgu, wd, gf.reshape(1, d)]
    in_specs = [pl.BlockSpec((tm, a.shape[1]), row) for a in acts]
    in_specs += [_resident(c.shape) for c in consts]
    return pl.pallas_call(
        functools.partial(_merge_ffn_kernel, final_norm=final_norm),
        grid=(m // tm,), in_specs=in_specs, out_specs=pl.BlockSpec((tm, d), row),
        out_shape=jax.ShapeDtypeStruct((m, d), F32), compiler_params=_cparams("parallel"),
        name="merge_ffn",
    )(*acts, *consts)


def _mlstm_kernel(qk_ref, v_ref, o_ref, if_ref, conv0_ref, c0_ref, n0_ref, m0_ref,
                  cw_ref, cb_ref, ifb_ref,
                  y_ref, conv_out_ref, c_out_ref, n_out_ref, m_out_ref,
                  cbuf, c_scr, n_scr, m_scr, *, chunk, n_last):
    c = pl.program_id(1)
    nc = pl.num_programs(1)
    L = chunk
    pad = SUBLANES

    @pl.when(c == 0)
    def _():
        cbuf[0:pad, :] = conv0_ref[0]
        c_scr[...] = c0_ref[0]
        n_scr[...] = n0_ref[0]
        m_scr[...] = m0_ref[0]

    cbuf[pad:pad + L, :] = qk_ref[0]
    acc = cb_ref[...]
    for j in range(M_CONV):
        acc = acc + cbuf[pad - (M_CONV - 1) + j: pad - (M_CONV - 1) + j + L, :] * cw_ref[j:j + 1, :]
    qk = _silu(acc)

    @pl.when(c == nc - 1)
    def _():
        conv_out_ref[0] = cbuf[n_last:n_last + pad, :]

    cbuf[0:pad, :] = cbuf[L:L + pad, :]

    gates = if_ref[0] + ifb_ref[...]
    lane8 = lax.broadcasted_iota(jnp.int32, (L, 2 * M_HEADS), 1)
    lf_all = -_softplus(-gates)
    lif = jnp.where(lane8 < M_HEADS, gates, lf_all)
    row_id = lax.broadcasted_iota(jnp.int32, (L, 2 * M_HEADS), 0)
    is_last_chunk = c == nc - 1
    if n_last < L:
        dead = jnp.logical_and(is_last_chunk, row_id >= n_last)
        lif = jnp.where(dead, jnp.where(lane8 < M_HEADS, NEG, 0.0), lif)
    t_i = lax.broadcasted_iota(jnp.int32, (L, L), 0)
    s_i = lax.broadcasted_iota(jnp.int32, (L, L), 1)
    causal = s_i <= t_i
    tri = causal.astype(BF16)
    csum = _dot_f32_rhs(tri, lif)
    lib = jnp.where(lane8 < M_HEADS, lif, csum)
    lib_t = _transpose_f32(lib, _eye(2 * M_HEADS))
    eye_h = _eye(M_HEAD_DIM)
    last = n_last - 1

    heads = range(M_HEADS)
    sl = [slice(h * M_HEAD_DIM, (h + 1) * M_HEAD_DIM) for h in heads]
    q = [qk[:, sl[h]] for h in heads]
    k = [qk[:, M_WIDTH + h * M_HEAD_DIM: M_WIDTH + (h + 1) * M_HEAD_DIM] * (M_HEAD_DIM ** -0.5) for h in heads]
    v = [v_ref[0, :, sl[h]] for h in heads]
    li_col = [lib[:, h:h + 1] for h in heads]
    b_col = [lib[:, M_HEADS + h:M_HEADS + h + 1] for h in heads]
    m_prev = [m_scr[0:1, h:h + 1] for h in heads]
    c_old = [c_scr[h] for h in heads]
    n_old = [n_scr[h:h + 1, :] for h in heads]
    s_qk = [_dot_nt(q[h], k[h]) for h in heads]
    q_c = [_dot(q[h], c_old[h]) for h in heads]
    dmat = [jnp.where(causal, b_col[h] - lib_t[M_HEADS + h:M_HEADS + h + 1, :] + lib_t[h:h + 1, :], NEG)
            for h in heads]
    inter = [b_col[h] + m_prev[h] for h in heads]
    m_t = [jnp.maximum(inter[h], jnp.max(dmat[h], axis=1, keepdims=True)) for h in heads]
    att = [jnp.exp(dmat[h] - m_t[h]) * s_qk[h] for h in heads]
    w_inter = [jnp.exp(inter[h] - m_t[h]) for h in heads]
    num = [_dot(att[h], v[h]) + w_inter[h] * q_c[h] for h in heads]
    den = [jnp.sum(att[h], axis=1, keepdims=True) + w_inter[h] * jnp.sum(q[h] * n_old[h], axis=1, keepdims=True)
           for h in heads]
    for h in heads:
        hh = num[h] / jnp.maximum(jnp.abs(den[h]), jnp.exp(-m_t[h]))
        y_ref[0, :, sl[h]] = _sigmoid(o_ref[0, :, sl[h]]) * hh

    m_new = [m_t[h][last:last + 1, :] for h in heads]
    b_last = [b_col[h][last:last + 1, :] for h in heads]
    scale = [jnp.exp(b_last[h] + m_prev[h] - m_new[h]) for h in heads]
    kw = [k[h] * jnp.exp(b_last[h] - b_col[h] + li_col[h] - m_new[h]) for h in heads]
    kw_t = [lax.dot_general(eye_h, kw[h].astype(BF16), DOT_NT, preferred_element_type=F32)
            for h in heads]
    for h in heads:
        c_scr[h] = scale[h] * c_old[h] + _dot(kw_t[h], v[h])
        n_scr[h:h + 1, :] = scale[h] * n_old[h] + jnp.sum(kw[h], axis=0, keepdims=True)
        m_scr[0:1, h:h + 1] = m_new[h]

    @pl.when(is_last_chunk)
    def _():
        c_out_ref[0] = c_scr[...]
        n_out_ref[0] = n_scr[...]
        m_out_ref[0] = m_scr[...]


def mlstm(m_qk, m_v, m_o, m_if, conv0, c0, n0, m0, conv_w, conv_b, if_bias, chunk):
    B, T, _ = m_qk.shape
    nc = -(-T // chunk)
    tp = nc * chunk
    n_last = T - (nc - 1) * chunk
    if tp != T:
        padt = lambda a: jnp.pad(a, ((0, 0), (0, tp - T), (0, 0)))
        m_qk, m_v, m_o, m_if = padt(m_qk), padt(m_v), padt(m_o), padt(m_if)
    conv0p = jnp.pad(conv0, ((0, 0), (SUBLANES - (M_CONV - 1), 0), (0, 0)))
    n0p = jnp.pad(n0, ((0, 0), (0, SUBLANES - M_HEADS), (0, 0)))
    m0p = jnp.pad(m0, ((0, 0), (0, LANES - M_HEADS))).reshape(B, 1, LANES)
    cw = jnp.pad(conv_w, ((0, SUBLANES - M_CONV), (0, 0)))
    tile = lambda w: pl.BlockSpec((1, chunk, w), lambda b, c: (b, c, 0))
    perb = lambda *s: pl.BlockSpec((1,) + s, lambda b, c: (b,) + (0,) * len(s))
    outs = pl.pallas_call(
        functools.partial(_mlstm_kernel, chunk=chunk, n_last=n_last),
        grid=(B, nc),
        in_specs=[tile(2 * M_WIDTH), tile(M_WIDTH), tile(M_WIDTH), tile(2 * M_HEADS),
                  perb(SUBLANES, 2 * M_WIDTH), perb(M_HEADS, M_HEAD_DIM, M_HEAD_DIM),
                  perb(SUBLANES, M_HEAD_DIM), perb(1, LANES),
                  _resident((SUBLANES, 2 * M_WIDTH)), _resident((1, 2 * M_WIDTH)),
                  _resident((1, 2 * M_HEADS))],
        out_specs=[tile(M_WIDTH), perb(SUBLANES, 2 * M_WIDTH),
                   perb(M_HEADS, M_HEAD_DIM, M_HEAD_DIM), perb(SUBLANES, M_HEAD_DIM), perb(1, LANES)],
        out_shape=[jax.ShapeDtypeStruct((B, tp, M_WIDTH), F32),
                   jax.ShapeDtypeStruct((B, SUBLANES, 2 * M_WIDTH), F32),
                   jax.ShapeDtypeStruct((B, M_HEADS, M_HEAD_DIM, M_HEAD_DIM), F32),
                   jax.ShapeDtypeStruct((B, SUBLANES, M_HEAD_DIM), F32),
                   jax.ShapeDtypeStruct((B, 1, LANES), F32)],
        scratch_shapes=[pltpu.VMEM((chunk + SUBLANES, 2 * M_WIDTH), F32),
                        pltpu.VMEM((M_HEADS, M_HEAD_DIM, M_HEAD_DIM), F32),
                        pltpu.VMEM((SUBLANES, M_HEAD_DIM), F32),
                        pltpu.VMEM((1, LANES), F32)],
        compiler_params=_cparams("parallel", "arbitrary"), name="mlstm",
    )(m_qk, m_v, m_o, m_if, conv0p, c0, n0p, m0p, cw, conv_b.reshape(1, -1), if_bias.reshape(1, -1))
    y, conv_o, c_o, n_o, m_o_ = outs
    return (y[:, :T], conv_o[:, SUBLANES - (M_CONV - 1):], c_o, n_o[:, :M_HEADS], m_o_[:, 0, :M_HEADS])


def _rwkv_prep_kernel(cols_ref, shift0_ref, mu_ref, w0_ref, w2_ref, a0_ref, a2_ref, g2_ref,
                      kk_ref, ka_ref, rk_ref,
                      r_o, w_o, k_o, v_o, a_o, b_o, g_o, bonus_o, carry):
    t = pl.program_id(1)

    @pl.when(t == 0)
    def _():
        carry[...] = shift0_ref[0]

    cols = cols_ref[0]
    tm = cols.shape[0]
    row = lax.broadcasted_iota(jnp.int32, cols.shape, 0)
    prev = jnp.where(row == 0, carry[...], pltpu.roll(cols, 1, 0))
    carry[...] = cols[tm - 1:tm, :]
    mixed = cols + (prev - cols) * mu_ref[...]
    W = R_WIDTH
    r = mixed[:, :W]
    k = mixed[:, W:2 * W]
    v = mixed[:, 2 * W:3 * W]
    o = 3 * W
    wd = mixed[:, o:o + R_DECAY_RANK]
    ad = mixed[:, o + R_DECAY_RANK:o + R_DECAY_RANK + R_A_RANK]
    gd = mixed[:, o + R_DECAY_RANK + R_A_RANK:]
    w = -_softplus(-(w0_ref[...] + _dot(jnp.tanh(wd), w2_ref[...]))) - 0.5
    a = _sigmoid(a0_ref[...] + _dot(ad, a2_ref[...]))
    g_o[0] = _dot(_sigmoid(gd), g2_ref[...])
    kk = k * kk_ref[...]
    k2 = k * (1.0 + (a - 1.0) * ka_ref[...])
    rkk = r * k2 * rk_ref[...]
    r_o[0] = r
    w_o[0] = -jnp.exp(w)
    k_o[0] = k2
    v_o[0] = v
    for h in range(R_HEADS):
        sl = slice(h * R_HEAD_DIM, (h + 1) * R_HEAD_DIM)
        kkh = kk[:, sl]
        kkh = kkh * lax.rsqrt(jnp.maximum(jnp.sum(kkh * kkh, axis=1, keepdims=True), 1e-24))
        a_o[0, :, sl] = -kkh
        b_o[0, :, sl] = kkh * a[:, sl]
        bonus_o[0, :, sl] = jnp.sum(rkk[:, sl], axis=1, keepdims=True) * v[:, sl]


DOT_NN = (((1,), (0,)), ((), ()))
DOT_NT = (((1,), (1,)), ((), ()))
DOT_TN = (((0,), (0,)), ((), ()))


def _dot_x3(x, y, dims):
    xh = x.astype(BF16)
    xl = (x - xh.astype(F32)).astype(BF16)
    yh = y.astype(BF16)
    yl = (y - yh.astype(F32)).astype(BF16)
    d = lambda p, q: lax.dot_general(p, q, dims, preferred_element_type=F32)
    return d(xh, yh) + d(xh, yl) + d(xl, yh)


def _dot_tn(a, b):
    return lax.dot_general(a.astype(BF16), b.astype(BF16), DOT_TN, preferred_element_type=F32)


def _rwkv_chunk_kernel(r_ref, lw_ref, k_ref, v_ref, a_ref, b_ref, bonus_ref, g_ref, lng_ref, lnb_ref,
                       s0_ref, o_ref, s_out_ref, s_scr, *, chunk, n_valid):
    t = pl.program_id(1)
    nt = pl.num_programs(1)
    L, N = chunk, R_HEAD_DIM

    @pl.when(t == 0)
    def _():
        s_scr[...] = s0_ref[0]

    r, lw, k, v, a, b = r_ref[0], lw_ref[0], k_ref[0], v_ref[0], a_ref[0], b_ref[0]
    if n_valid % L:
        dead = (t * L + lax.broadcasted_iota(jnp.int32, (L, R_WIDTH), 0)) >= n_valid
        lw, k, v, a, b = (jnp.where(dead, 0.0, z) for z in (lw, k, v, a, b))
    ti = lax.broadcasted_iota(jnp.int32, (L, L), 0)
    si = lax.broadcasted_iota(jnp.int32, (L, L), 1)
    incl = si <= ti
    strict = si < ti
    eye = (si == ti).astype(F32)
    c = _dot_f32_rhs(incl.astype(BF16), lw)
    c_end = c[L - 1:L, :]
    a_hat = a * jnp.exp(c - lw)
    r_hat = r * jnp.exp(c)
    e_neg = jnp.exp(-c)
    b_til = b * e_neg
    k_til = k * e_neg
    e_end = jnp.exp(c_end - c)
    b_bar = b * e_end
    k_bar = k * e_end
    gam = jnp.exp(c_end)

    heads = range(R_HEADS)
    sl = [slice(h * N, (h + 1) * N) for h in heads]
    ah = [a_hat[:, sl[h]] for h in heads]
    rh = [r_hat[:, sl[h]] for h in heads]
    bt = [b_til[:, sl[h]] for h in heads]
    kt = [k_til[:, sl[h]] for h in heads]
    vh = [v[:, sl[h]] for h in heads]
    a_ab = [jnp.where(strict, _dot_x3(ah[h], bt[h], DOT_NT), 0.0) for h in heads]
    a_ak = [jnp.where(strict, _dot_nt(ah[h], kt[h]), 0.0) for h in heads]
    a_rb = [jnp.where(incl, _dot_nt(rh[h], bt[h]), 0.0) for h in heads]
    a_rk = [jnp.where(incl, _dot_nt(rh[h], kt[h]), 0.0) for h in heads]
    inv = [eye + a_ab[h] for h in heads]
    pw = a_ab
    span = 2
    while span < L:
        pw = [_dot(pw[h], pw[h]) for h in heads]
        inv = [inv[h] + _dot(inv[h], pw[h]) for h in heads]
        span *= 2
    s = [s_scr[h] for h in heads]
    rhs = [_dot_nt(ah[h], s[h]) + _dot(a_ak[h], vh[h]) for h in heads]
    u = [_dot(inv[h], rhs[h]) for h in heads]
    y = [_dot_nt(rh[h], s[h]) + _dot(a_rb[h], u[h]) + _dot(a_rk[h], vh[h]) for h in heads]
    for h in heads:
        s_scr[h] = s[h] * gam[:, sl[h]] + _dot_tn(u[h], b_bar[:, sl[h]]) + _dot_tn(vh[h], k_bar[:, sl[h]])
    for h in heads:
        mean = jnp.mean(y[h], axis=1, keepdims=True)
        yc = y[h] - mean
        var = jnp.mean(yc * yc, axis=1, keepdims=True)
        yn = yc * lax.rsqrt(var + LN_X_EPS) * lng_ref[:, sl[h]] + lnb_ref[:, sl[h]]
        o_ref[0, :, sl[h]] = (yn + bonus_ref[0, :, sl[h]]) * g_ref[0, :, sl[h]]

    @pl.when(t == nt - 1)
    def _():
        s_out_ref[0] = s_scr[...]


def rwkv(cols, shift0, s0, p, tm, chunk):
    B, T, _ = cols.shape
    tp = -(-T // tm) * tm
    assert tm % chunk == 0 and chunk % SUBLANES == 0 and chunk & (chunk - 1) == 0
    colsp = jnp.pad(cols, ((0, 0), (0, tp - T), (0, 0))) if tp != T else cols
    nt = tp // tm
    vec = lambda a: a.reshape(1, -1)
    consts = [vec(p['rwkv_mu']), vec(p['rwkv_w0']), p['rwkv_w2'].astype(BF16), vec(p['rwkv_a0']),
              p['rwkv_a2'].astype(BF16), p['rwkv_g2'].astype(BF16), vec(p['rwkv_k_k']),
              vec(p['rwkv_k_a']), vec(p['rwkv_r_k'])]
    fl = jax.ShapeDtypeStruct((B, tp, R_WIDTH), F32)
    fl_spec = lambda n, w: pl.BlockSpec((1, n, w), lambda b, t: (b, t, 0))
    r, lw, k, v, a, bb, g, bonus = pl.pallas_call(
        _rwkv_prep_kernel, grid=(B, nt),
        in_specs=[fl_spec(tm, R_COLS), pl.BlockSpec((1, 1, R_COLS), lambda b, t: (b, 0, 0))]
                 + [_resident(c.shape) for c in consts],
        out_specs=[fl_spec(tm, R_WIDTH)] * 8, out_shape=[fl] * 8,
        scratch_shapes=[pltpu.VMEM((1, R_COLS), F32)],
        compiler_params=_cparams("parallel", "arbitrary"), name="rwkv_prep",
    )(colsp, shift0, *consts)
    st_spec = pl.BlockSpec((1, R_HEADS, R_HEAD_DIM, R_HEAD_DIM), lambda b, t: (b, 0, 0, 0))
    out, s_new = pl.pallas_call(
        functools.partial(_rwkv_chunk_kernel, chunk=chunk, n_valid=T),
        grid=(B, tp // chunk),
        in_specs=[fl_spec(chunk, R_WIDTH)] * 8 + [_resident((1, R_WIDTH))] * 2 + [st_spec],
        out_specs=[fl_spec(chunk, R_WIDTH), st_spec],
        out_shape=[fl, jax.ShapeDtypeStruct(s0.shape, F32)],
        scratch_shapes=[pltpu.VMEM((R_HEADS, R_HEAD_DIM, R_HEAD_DIM), F32)],
        compiler_params=_cparams("parallel", "arbitrary"), name="rwkv_chunk",
    )(r, lw, k, v, a, bb, bonus, g, vec(p['rwkv_ln_g']), vec(p['rwkv_ln_b']), s0)
    return out[:, :T], s_new


def compress_weights(pe, w1, w2):
    d, hdim, G = A_HEAD_DIM, CMP_HIDDEN, A_KV_HEADS
    w1r = w1.reshape(2, CMP_STRIDE, d, hdim)
    wl = jnp.zeros((CMP_STRIDE, G, d, 2, G, hdim), F32)
    for g in range(G):
        wl = wl.at[:, g, :, :, g, :].set(jnp.transpose(w1r, (1, 2, 0, 3)))
    wl = wl.reshape(CMP_STRIDE * G * d, 2 * G * hdim).astype(BF16)
    wpe = jnp.transpose(w1r, (1, 0, 2, 3)).reshape(CMP_STRIDE, 2 * d, hdim)
    wpe = jnp.concatenate([wpe] * G, axis=-1).astype(BF16)
    pe_l = jnp.transpose(pe.reshape(2, CMP_STRIDE, d), (1, 0, 2)).reshape(CMP_STRIDE, 1, 2 * d)
    pe_l = jnp.broadcast_to(pe_l, (CMP_STRIDE, SUBLANES, 2 * d))
    w2bd = jnp.zeros((G, hdim, G, d), F32)
    for g in range(G):
        w2bd = w2bd.at[g, :, g, :].set(w2)
    return wl, wpe, pe_l, w2bd.reshape(G * hdim, G * d).astype(BF16)


def _compress_halves(strided_rows, wl_ref, n):
    x = jnp.concatenate([strided_rows(l).astype(BF16) for l in range(CMP_STRIDE)], axis=1)
    return jnp.dot(x, wl_ref[...], preferred_element_type=F32)


def _compress_finish(halves, wpe_ref, pe_ref, w2_ref):
    n = halves.shape[0]
    gh = A_KV_HEADS * CMP_HIDDEN
    pe_term = None
    for l in range(CMP_STRIDE):
        part = _dot(pe_ref[l], wpe_ref[l])
        pe_term = part if pe_term is None else pe_term + part
    second = pltpu.roll(halves[:, gh:], n - 1, 0)
    hid = _silu(halves[:, :gh] + second + pe_term[0:1, :])
    return _dot(hid, w2_ref[...])


def _compress_kernel(x_ref, wl_ref, wpe_ref, pe_ref, w2_ref, o_ref):
    n = o_ref.shape[1]
    rows = lambda l: x_ref[0, pl.ds(l, n, stride=CMP_STRIDE), :]
    o_ref[0] = _compress_finish(_compress_halves(rows, wl_ref, n), wpe_ref, pe_ref, w2_ref)


def compress_rows(rows, cw):
    B, T, W = rows.shape
    n = T // CMP_STRIDE
    return pl.pallas_call(
        _compress_kernel, grid=(B,),
        in_specs=[pl.BlockSpec((1, T, W), lambda b: (b, 0, 0))] + [_resident(c.shape) for c in cw],
        out_specs=pl.BlockSpec((1, n, W), lambda b: (b, 0, 0)),
        out_shape=jax.ShapeDtypeStruct((B, n, W), F32),
        compiler_params=_cparams("parallel"), name="nsa_compress",
    )(rows, *cw)


def _topk_mask_columns(rank, k):
    n_entries = rank.shape[0]
    row = lax.broadcasted_iota(jnp.int32, rank.shape, 0).astype(F32)
    sel = jnp.zeros(rank.shape, F32)
    for _ in range(k):
        m = jnp.max(rank, axis=0, keepdims=True)
        first = jnp.min(jnp.where(rank == m, row, float(n_entries)), axis=0, keepdims=True)
        hit = row == first
        sel = jnp.where(hit, 1.0, sel)
        rank = jnp.where(hit, LOWEST, rank)
    return sel


def _flash_update(state, s, bias, v):
    m_old, l_old, acc = state
    s = s + bias
    m_new = jnp.maximum(m_old, jnp.max(s, axis=1, keepdims=True))
    p = jnp.exp(s - m_new)
    alpha = jnp.exp(m_old - m_new)
    return (m_new, alpha * l_old + jnp.sum(p, axis=1, keepdims=True), alpha * acc + _dot(p, v))


def _nsa_prompt_kernel(q_ref, gate_ref, kc_ref, vc_ref, ks_ref, vs_ref, kw_ref, vw_ref, covt_ref,
                       o_ref, *, seq, tk, wk):
    i = pl.program_id(1)
    QB, d, J = Q_BLOCK, A_HEAD_DIM, A_GROUP
    nc = kc_ref.shape[1]
    nb = covt_ref.shape[0]
    st = i * QB
    q = q_ref[0] * A_SCALE
    gate = _sigmoid(gate_ref[0])
    rows = J * QB
    t_q = st + lax.broadcasted_iota(jnp.int32, (QB, 1), 0)
    t_rows = jnp.concatenate([t_q] * J, axis=0)
    blk_end = lax.broadcasted_iota(jnp.int32, (rows, nc), 1) * CMP_STRIDE + (CMP_BLOCK - 1)
    cmp_mask = blk_end <= t_rows
    jj = lax.broadcasted_iota(jnp.int32, (nb, QB), 0)
    cur = (st + lax.broadcasted_iota(jnp.int32, (1, QB), 1)) // SEL_BLOCK
    forced = (jj == 0) | (jj == cur) | (jj == cur - 1)
    zeros = lambda w: jnp.zeros((rows, w), F32)

    qgs, o_cmps, ranks = [], [], []
    for g in range(A_KV_HEADS):
        gl = slice(g * d, (g + 1) * d)
        qg = jnp.concatenate([q[:, (g * J + j) * d:(g * J + j + 1) * d] for j in range(J)],
                             axis=0).astype(BF16)
        s = jnp.where(cmp_mask, _dot_nt(qg, kc_ref[0, :, gl]), NEG)
        e = jnp.exp(s - jnp.max(s, axis=1, keepdims=True))
        p = jnp.where(cmp_mask, e / jnp.sum(e, axis=1, keepdims=True), 0.0)
        p_grp = p[0:QB]
        for j in range(1, J):
            p_grp = p_grp + p[j * QB:(j + 1) * QB]
        score = _dot_f32_rhs_nt(covt_ref[...], p_grp)
        qgs.append(qg)
        o_cmps.append(_dot(p, vc_ref[0, :, gl]))
        ranks.append(jnp.where(jj <= cur, jnp.where(forced, FORCE, score), NEG))
    sel_all = _topk_mask_columns(jnp.concatenate(ranks, axis=1), min(N_SEL, nb)).astype(BF16)

    k0w = pl.multiple_of(jnp.clip(st + QB - wk, 0, seq - wk), QB)
    rel = t_q - (k0w + lax.broadcasted_iota(jnp.int32, (1, wk), 1))
    win_bias = jnp.concatenate([jnp.where((rel >= 0) & (rel < WINDOW), 0.0, NEG)] * J, axis=0)

    for g in range(A_KV_HEADS):
        gl = slice(g * d, (g + 1) * d)
        qg, o_cmp = qgs[g], o_cmps[g]
        sel = sel_all[:, g * QB:(g + 1) * QB]

        def sel_tile(kt, state):
            k0 = pl.multiple_of(kt * tk, tk)
            s_pos = k0 + lax.broadcasted_iota(jnp.int32, (1, tk), 1)
            blk_of = k0 // SEL_BLOCK + lax.broadcasted_iota(jnp.int32, (nb, tk), 1) // SEL_BLOCK
            expand = (lax.broadcasted_iota(jnp.int32, (nb, tk), 0) == blk_of).astype(BF16)
            chosen = lax.dot_general(sel, expand, DOT_TN, preferred_element_type=F32) > 0.5
            bias = jnp.where(chosen & (s_pos <= t_q), 0.0, NEG)
            sc = _dot_nt(qg, ks_ref[0, pl.ds(k0, tk), gl])
            return _flash_update(state, sc, jnp.concatenate([bias] * J, axis=0), vs_ref[0, pl.ds(k0, tk), gl])

        init = (jnp.full((rows, 1), NEG, F32), jnp.zeros((rows, 1), F32), zeros(d))
        _, l_sel, acc_sel = lax.fori_loop(0, (st + QB - 1) // tk + 1, sel_tile, init)
        o_sel = acc_sel / l_sel

        sw = _dot_nt(qg, kw_ref[0, pl.ds(k0w, wk), gl]) + win_bias
        pw = jnp.exp(sw - jnp.max(sw, axis=1, keepdims=True))
        o_win = _dot(pw, vw_ref[0, pl.ds(k0w, wk), gl]) / jnp.sum(pw, axis=1, keepdims=True)

        for j in range(J):
            h = g * J + j
            rs = slice(j * QB, (j + 1) * QB)
            o_ref[0, :, h * d:(h + 1) * d] = (gate[:, h:h + 1] * o_cmp[rs]
                                              + gate[:, A_HEADS + h:A_HEADS + h + 1] * o_sel[rs]
                                              + gate[:, 2 * A_HEADS + h:2 * A_HEADS + h + 1] * o_win[rs])


def _covers(n_cmp, n_blocks):
    i = np.arange(n_cmp)[:, None]
    j = np.arange(n_blocks)[None, :]
    return (i * CMP_STRIDE < (j + 1) * SEL_BLOCK) & (i * CMP_STRIDE + CMP_BLOCK > j * SEL_BLOCK)


def nsa_prompt_attend(a_q, a_gate, kc, vc, ks, vs, kw, vw):
    B, T, _ = a_q.shape
    nc = kc.shape[1]
    nb = -(-T // SEL_BLOCK)
    tk = min(4 * Q_BLOCK, T)
    wk = min(WINDOW + Q_BLOCK, T)
    cov = jnp.asarray(_covers(nc, nb).T, BF16)
    tile = lambda w: pl.BlockSpec((1, Q_BLOCK, w), lambda b, i: (b, i, 0))
    full = lambda n, w: pl.BlockSpec((1, n, w), lambda b, i: (b, 0, 0))
    W = A_KV_LANES
    return pl.pallas_call(
        functools.partial(_nsa_prompt_kernel, seq=T, tk=tk, wk=wk),
        grid=(B, T // Q_BLOCK),
        in_specs=[tile(A_WIDTH), tile(3 * A_HEADS), full(nc, W), full(nc, W),
                  full(T, W), full(T, W), full(T, W), full(T, W), _resident(cov.shape)],
        out_specs=tile(A_WIDTH),
        out_shape=jax.ShapeDtypeStruct((B, T, A_WIDTH), F32),
        compiler_params=_cparams("parallel", "arbitrary"), name="nsa_prompt",
    )(a_q, a_gate, kc, vc, ks, vs, kw, vw, cov)


GATHER_PAGES = 16
CHUNKS_PER_PAGE = PAGE_SIZE // CMP_STRIDE
HEAD_ROWS = SUBLANES
PICK_ROWS = 2 * SUBLANES


def channel_major(cache):
    L, n, tok, G, d = cache.shape
    return jnp.transpose(cache, (0, 1, 3, 4, 2)).reshape(L, n, G * d, tok)


def _nsa_sample_cmp_kernel(pt_ref, *refs, past_len):
    del pt_ref
    ng = GATHER_PAGES
    pk_refs, pv_refs = refs[:ng], refs[ng:2 * ng]
    (q_ref, wink_ref, winv_ref, kwn_ref, vwn_ref, wlk_ref, wpek_ref, pek_ref, w2k_ref,
     wlv_ref, wpev_ref, pev_ref, w2v_ref, cov_ref, ocmp_ref, owin_ref, idx_ref,
     hk_scr, hv_scr, xk_scr, xv_scr) = refs[2 * ng:]
    c = pl.program_id(1)
    page = pk_refs[0].shape[3]
    rows_per_step = ng * CHUNKS_PER_PAGE
    base = pl.multiple_of(c * rows_per_step, rows_per_step)
    for page_refs, xs, wl_ref, h_scr in ((pk_refs, xk_scr, wlk_ref, hk_scr), (pv_refs, xv_scr, wlv_ref, hv_scr)):
        for i in range(ng):
            xs[i * page:(i + 1) * page, :] = page_refs[i][0, 0].T
        rows = lambda l, xs=xs: xs[pl.ds(l, rows_per_step, stride=CMP_STRIDE), :]
        h_scr[pl.ds(base, rows_per_step), :] = _compress_halves(rows, wl_ref, rows_per_step)

    @pl.when(c == pl.num_programs(1) - 1)
    def _():
        d, R8 = A_HEAD_DIM, HEAD_ROWS
        kc = _compress_finish(hk_scr[...], wpek_ref, pek_ref, w2k_ref)
        vc = _compress_finish(hv_scr[...], wpev_ref, pev_ref, w2v_ref)
        nc = kc.shape[0]
        nb = cov_ref.shape[0]
        keep = wink_ref.shape[3]
        q_pos = past_len
        blk_end = lax.broadcasted_iota(jnp.int32, (R8, nc), 1) * CMP_STRIDE + (CMP_BLOCK - 1)
        cmp_mask = (blk_end <= q_pos) & (lax.broadcasted_iota(jnp.int32, (R8, nc), 1) < nc - 1)
        real_head = lax.broadcasted_iota(jnp.int32, (R8, nc), 0) < A_GROUP
        col8 = lax.broadcasted_iota(jnp.int32, (nb, R8), 1)
        jj = lax.broadcasted_iota(jnp.int32, (nb, R8), 0)
        cur = q_pos // SEL_BLOCK
        forced = (jj == 0) | (jj == cur) | (jj == cur - 1)
        k_idx = lax.broadcasted_iota(jnp.int32, (R8, keep), 1)
        rel = q_pos - (past_len - keep + k_idx)
        win_mask = (rel >= 0) & (rel < WINDOW)
        rank = None
        for g in range(A_KV_HEADS):
            gl = slice(g * d, (g + 1) * d)
            qg = q_ref[0, g] * A_SCALE
            s = jnp.where(cmp_mask, _dot_nt(qg, kc[:, gl]), NEG)
            e = jnp.exp(s - jnp.max(s, axis=1, keepdims=True))
            p = jnp.where(cmp_mask, e / jnp.sum(e, axis=1, keepdims=True), 0.0)
            ocmp_ref[0, g] = _dot(p, vc[:, gl])
            p_grp = jnp.sum(jnp.where(real_head, p, 0.0), axis=0, keepdims=True)
            score = _dot_f32_rhs_nt(cov_ref[...], jnp.broadcast_to(p_grp, (R8, nc)))
            rank_g = jnp.where(jj <= cur, jnp.where(forced, FORCE, score), NEG)
            rank = rank_g if rank is None else jnp.where(col8 == g, rank_g, rank)
            sw = jnp.where(win_mask, _dot(qg, wink_ref[0, 0, gl, :]), NEG)
            s_new = jnp.sum(qg * kwn_ref[0, :, gl], axis=1, keepdims=True)
            m = jnp.maximum(jnp.max(sw, axis=1, keepdims=True), s_new)
            ew = jnp.where(win_mask, jnp.exp(sw - m), 0.0)
            e_new = jnp.exp(s_new - m)
            owin_ref[0, g] = ((_dot_nt(ew, winv_ref[0, 0, gl, :]) + e_new * vwn_ref[0, :, gl])
                              / (jnp.sum(ew, axis=1, keepdims=True) + e_new))
        block = jj.astype(F32)
        out_row = lax.broadcasted_iota(jnp.int32, idx_ref.shape[1:], 0)
        picks = jnp.zeros(idx_ref.shape[1:], jnp.int32)
        for it in range(N_SEL - 1):
            m = jnp.max(rank, axis=0, keepdims=True)
            first = jnp.min(jnp.where(rank == m, block, float(nb)), axis=0, keepdims=True)
            picks = jnp.where(out_row == it, first.astype(jnp.int32), picks)
            rank = jnp.where(block == first, LOWEST, rank)
        idx_ref[0] = picks


def nsa_sample_cmp(page_table, pool_k, pool_v, q_hm, win_k, win_v, layer, kw_new, vw_new, cwk, cwv, past_len):
    B, n_pages = page_table.shape
    _, n_pool, W, page = pool_k.shape
    ng = GATHER_PAGES
    assert n_pages % ng == 0
    nc = n_pages * CHUNKS_PER_PAGE
    nb = past_len // SEL_BLOCK
    cov = jnp.asarray(_covers(nc, nb).T, BF16)
    keep = win_k.shape[3]
    page_spec = lambda i: pl.BlockSpec((1, 1, W, page), lambda b, c, pt: (layer, pt[b, c * ng + i], 0, 0))
    perb = lambda *s: pl.BlockSpec((1,) + s, lambda b, c, pt: (b,) + (0,) * len(s))
    win_spec = pl.BlockSpec((1, 1, W, keep), lambda b, c, pt: (layer, b, 0, 0))
    const = lambda a: pl.BlockSpec(a.shape, lambda b, c, pt: (0,) * a.ndim)
    consts = [*cwk, *cwv, cov]
    hw = cwk[0].shape[1]
    hm = jax.ShapeDtypeStruct((B, A_KV_HEADS, HEAD_ROWS, A_HEAD_DIM), F32)
    grid_spec = pltpu.PrefetchScalarGridSpec(
        num_scalar_prefetch=1, grid=(B, n_pages // ng),
        in_specs=[page_spec(i) for i in range(ng)] * 2
                 + [perb(A_KV_HEADS, HEAD_ROWS, A_HEAD_DIM), win_spec, win_spec, perb(1, W), perb(1, W)]
                 + [const(a) for a in consts],
        out_specs=[perb(A_KV_HEADS, HEAD_ROWS, A_HEAD_DIM), perb(A_KV_HEADS, HEAD_ROWS, A_HEAD_DIM),
                   perb(PICK_ROWS, HEAD_ROWS)],
        scratch_shapes=[pltpu.VMEM((nc, hw), F32), pltpu.VMEM((nc, hw), F32),
                        pltpu.VMEM((ng * page, W), F32), pltpu.VMEM((ng * page, W), F32)])
    return pl.pallas_call(
        functools.partial(_nsa_sample_cmp_kernel, past_len=past_len), grid_spec=grid_spec,
        out_shape=[hm, hm, jax.ShapeDtypeStruct((B, PICK_ROWS, HEAD_ROWS), jnp.int32)],
        compiler_params=_cparams("parallel", "arbitrary"), name="nsa_sample_cmp",
    )(page_table, *([pool_k] * ng), *([pool_v] * ng), q_hm, win_k, win_v, kw_new, vw_new, *consts)


SEL_PICKS_PER_STEP = 5


def _nsa_sample_sel_kernel(pt_ref, pk_ref, *refs, npk):
    del pt_ref
    P = SEL_PICKS_PER_STEP
    n_pages = A_KV_HEADS * P
    k_refs, v_refs = refs[:n_pages], refs[n_pages:2 * n_pages]
    q_ref, gate_ref, ksn_ref, vsn_ref, ocmp_ref, owin_ref, o_ref, m_scr, l_scr, acc_scr = refs[2 * n_pages:]
    b = pl.program_id(0)
    c = pl.program_id(1)
    d = A_HEAD_DIM
    page = k_refs[0].shape[3]
    tok_blk = lax.broadcasted_iota(jnp.int32, (HEAD_ROWS, page), 1) // SEL_BLOCK
    for g in range(A_KV_HEADS):
        gl = slice(g * d, (g + 1) * d)
        qg = q_ref[0, g] * A_SCALE

        @pl.when(c == 0)
        def _():
            m_scr[g] = jnp.sum(qg * ksn_ref[0, :, gl], axis=1, keepdims=True)
            l_scr[g] = jnp.ones((HEAD_ROWS, 1), F32)
            acc_scr[g] = jnp.broadcast_to(vsn_ref[0, :, gl], (HEAD_ROWS, d))

        chosen = [tok_blk == pk_ref[b, g * npk + c * P + i] % (page // SEL_BLOCK) for i in range(P)]
        s = [jnp.where(chosen[i], _dot(qg, k_refs[g * P + i][0, 0, gl, :]), NEG) for i in range(P)]
        m_old = m_scr[g]
        m = m_old
        for i in range(P):
            m = jnp.maximum(m, jnp.max(s[i], axis=1, keepdims=True))
        alpha = jnp.exp(m_old - m)
        l = alpha * l_scr[g]
        acc = alpha * acc_scr[g]
        for i in range(P):
            p = jnp.where(chosen[i], jnp.exp(s[i] - m), 0.0)
            l = l + jnp.sum(p, axis=1, keepdims=True)
            acc = acc + _dot_nt(p, v_refs[g * P + i][0, 0, gl, :])
        m_scr[g] = m
        l_scr[g] = l
        acc_scr[g] = acc

        @pl.when(c == pl.num_programs(1) - 1)
        def _():
            gate = _sigmoid(gate_ref[0, g])
            o_ref[0, g] = (gate[:, 0:1] * ocmp_ref[0, g] + gate[:, 1:2] * (acc / l)
                           + gate[:, 2:3] * owin_ref[0, g])


def nsa_sample_sel(page_table, picks, pool_k, pool_v, layer, q_hm, gate_hm, ks_new, vs_new, o_cmp, o_win):
    B = page_table.shape[0]
    _, n_pool, W, page = pool_k.shape
    per_page = page // SEL_BLOCK
    npk = picks.shape[2]
    P = SEL_PICKS_PER_STEP
    assert npk % P == 0
    picks = picks.reshape(B, A_KV_HEADS * npk)

    def blk(g, i):
        def index(b, c, pt, pk):
            return (layer, pt[b, pk[b, g * npk + c * P + i] // per_page], 0, 0)
        return pl.BlockSpec((1, 1, W, page), index)

    pages = [blk(g, i) for g in range(A_KV_HEADS) for i in range(P)]
    perb = lambda *s: pl.BlockSpec((1,) + s, lambda b, c, pt, pk: (b,) + (0,) * len(s))
    hm_spec = perb(A_KV_HEADS, HEAD_ROWS, A_HEAD_DIM)
    grid_spec = pltpu.PrefetchScalarGridSpec(
        num_scalar_prefetch=2, grid=(B, npk // P),
        in_specs=pages * 2 + [hm_spec, perb(A_KV_HEADS, HEAD_ROWS, 3), perb(1, W), perb(1, W), hm_spec, hm_spec],
        out_specs=hm_spec,
        scratch_shapes=[pltpu.VMEM((A_KV_HEADS, HEAD_ROWS, 1), F32), pltpu.VMEM((A_KV_HEADS, HEAD_ROWS, 1), F32),
                        pltpu.VMEM((A_KV_HEADS, HEAD_ROWS, A_HEAD_DIM), F32)])
    return pl.pallas_call(
        functools.partial(_nsa_sample_sel_kernel, npk=npk), grid_spec=grid_spec,
        out_shape=jax.ShapeDtypeStruct((B, A_KV_HEADS, HEAD_ROWS, A_HEAD_DIM), F32),
        compiler_params=_cparams("parallel", "arbitrary"), name="nsa_sample_sel",
    )(page_table, picks, *([pool_k] * len(pages)), *([pool_v] * len(pages)), q_hm, gate_hm, ks_new, vs_new,
      o_cmp, o_win)


def nsa_sample(a_q, a_gate, new_rows, pools, win_k, win_v, layer, page_table, cwk, cwv):
    B = a_q.shape[0]
    past_len = page_table.shape[1] * PAGE_SIZE
    _, _, ksn, vsn, kwn, vwn = new_rows
    pool_kc, pool_vc, pool_ks, pool_vs = pools
    padj = ((0, 0), (0, 0), (0, HEAD_ROWS - A_GROUP), (0, 0))
    q_hm = jnp.pad(a_q.reshape(B, A_KV_HEADS, A_GROUP, A_HEAD_DIM), padj)
    gate_hm = jnp.pad(jnp.transpose(a_gate.reshape(B, 3, A_KV_HEADS, A_GROUP), (0, 2, 3, 1)), padj)
    o_cmp, o_win, picks = nsa_sample_cmp(page_table, pool_kc, pool_vc, q_hm, win_k, win_v, layer, kwn, vwn,
                                         cwk, cwv, past_len)
    picks = jnp.swapaxes(picks, 1, 2)[:, :A_KV_HEADS, :N_SEL - 1]
    o = nsa_sample_sel(page_table, picks, pool_ks, pool_vs, layer, q_hm, gate_hm, ksn, vsn, o_cmp, o_win)
    return o[:, :, :A_GROUP].reshape(B, 1, A_WIDTH)


PROJ_GROUPS = (2 * M_WIDTH, M_WIDTH, M_WIDTH, 2 * M_HEADS, A_WIDTH) + (A_KV_LANES,) * 6 + (
    3 * A_HEADS, R_COLS, 3 * D_MODEL)
PROMPT_ROW_TILE = 256
MLSTM_CHUNK = 128
RWKV_STEPS = 64


def _layer_params(l, w_in, w_branch_m, w_branch_a, w_branch_r, w_out, w_gate_up, w_down,
                  nsa_cmp_pe, nsa_cmp_w1, nsa_cmp_w2):
    offs = np.cumsum((0,) + PROJ_GROUPS)
    assert offs[-1] == w_in.shape[2]
    wi = w_in[l].astype(BF16)
    return {
        'proj': [wi[:, a:b] for a, b in zip(offs[:-1], offs[1:])],
        'wm': w_branch_m[l].astype(BF16), 'wa': w_branch_a[l].astype(BF16), 'wr': w_branch_r[l].astype(BF16),
        'wo': w_out[l].astype(BF16), 'wgu': w_gate_up[l].astype(BF16), 'wd': w_down[l].astype(BF16),
        'cwk': compress_weights(nsa_cmp_pe[l, 0], nsa_cmp_w1[l, 0], nsa_cmp_w2[l, 0]),
        'cwv': compress_weights(nsa_cmp_pe[l, 1], nsa_cmp_w1[l, 1], nsa_cmp_w2[l, 1]),
    }


def _trunk_layer(x, lw, lp, nsa_fn, m_state, r_state, tm, chunk, rwkv_tm, rwkv_steps, gf, final_norm):
    B, T, D = x.shape
    x2d = x.reshape(B * T, D)
    outs = norm_proj(x2d, lp['norm1_g'], lw['proj'], tm)
    (m_qk, m_v, m_o, m_if, a_q), new_rows, (a_gate, r_cols, merge) = outs[:5], outs[5:11], outs[11:]
    b3 = lambda a: a.reshape(B, T, a.shape[-1])
    new_rows = [b3(r) for r in new_rows]
    y_m, conv_n, c_n, n_n, m_n = mlstm(b3(m_qk), b3(m_v), b3(m_o), b3(m_if), *m_state,
                                        lp['mlstm_conv_w'], lp['mlstm_conv_b'], lp['mlstm_if_bias'], chunk)
    y_a = nsa_fn(b3(a_q), b3(a_gate), new_rows)
    r_cols = b3(r_cols)
    y_r, s_n = rwkv(r_cols, r_state[0], r_state[1], lp, rwkv_tm, rwkv_steps)
    x_new = merge_ffn(x2d, y_m.reshape(B * T, -1), y_a.reshape(B * T, -1), y_r.reshape(B * T, -1), merge,
                      lw['wm'], lw['wa'], lw['wr'], lw['wo'], lp['norm2_g'], lw['wgu'], lw['wd'], gf,
                      tm, final_norm)
    return x_new.reshape(B, T, D), new_rows, (conv_n, c_n, n_n, m_n), (r_cols[:, -1:], s_n)


def kernel(x_prompt, x_sample, cache_cmp_k, cache_cmp_v, cache_slc_k, cache_slc_v, cache_win_k, cache_win_v, state_mlstm_conv, state_mlstm_C, state_mlstm_n, state_mlstm_m, state_rwkv_shift, state_rwkv, page_table, norm1_g, w_in, mlstm_if_bias, mlstm_conv_w, mlstm_conv_b, nsa_cmp_pe, nsa_cmp_w1, nsa_cmp_w2, rwkv_mu, rwkv_w0, rwkv_w2, rwkv_a0, rwkv_a2, rwkv_g2, rwkv_k_k, rwkv_k_a, rwkv_r_k, rwkv_ln_g, rwkv_ln_b, w_branch_m, w_branch_a, w_branch_r, w_out, norm2_g, w_gate_up, w_down, final_norm_g):
    bp, tp, _ = x_prompt.shape
    bs, ts, _ = x_sample.shape
    assert ts == 1
    depth = w_in.shape[0]
    kvshape = lambda a: a.reshape(a.shape[0], a.shape[1], A_KV_HEADS, A_HEAD_DIM)
    xp, xs = x_prompt, x_sample
    pools = [channel_major(c) for c in (cache_cmp_k, cache_cmp_v, cache_slc_k, cache_slc_v)]
    win_k, win_v = channel_major(cache_win_k), channel_major(cache_win_v)
    states_p, states_s = [], []
    for l in range(depth):
        lw = _layer_params(l, w_in, w_branch_m, w_branch_a, w_branch_r, w_out, w_gate_up, w_down,
                           nsa_cmp_pe, nsa_cmp_w1, nsa_cmp_w2)
        lp = {'norm1_g': norm1_g[l], 'mlstm_if_bias': mlstm_if_bias[l], 'mlstm_conv_w': mlstm_conv_w[l],
              'mlstm_conv_b': mlstm_conv_b[l], 'rwkv_mu': rwkv_mu[l], 'rwkv_w0': rwkv_w0[l],
              'rwkv_w2': rwkv_w2[l], 'rwkv_a0': rwkv_a0[l], 'rwkv_a2': rwkv_a2[l], 'rwkv_g2': rwkv_g2[l],
              'rwkv_k_k': rwkv_k_k[l], 'rwkv_k_a': rwkv_k_a[l], 'rwkv_r_k': rwkv_r_k[l],
              'rwkv_ln_g': rwkv_ln_g[l], 'rwkv_ln_b': rwkv_ln_b[l], 'norm2_g': norm2_g[l]}
        last = l == depth - 1

        def nsa_p(a_q, a_gate, rows):
            kc = compress_rows(rows[0], lw['cwk'])
            vc = compress_rows(rows[1], lw['cwv'])
            return nsa_prompt_attend(a_q, a_gate, kc, vc, *rows[2:])

        m_zero = (jnp.zeros((bp, M_CONV - 1, 2 * M_WIDTH), F32),
                  jnp.zeros((bp, M_HEADS, M_HEAD_DIM, M_HEAD_DIM), F32),
                  jnp.zeros((bp, M_HEADS, M_HEAD_DIM), F32), jnp.zeros((bp, M_HEADS), F32))
        r_zero = (jnp.zeros((bp, 1, R_COLS), F32), jnp.zeros((bp, R_HEADS, R_HEAD_DIM, R_HEAD_DIM), F32))
        xp, rows, m_new, r_new = _trunk_layer(xp, lw, lp, nsa_p, m_zero, r_zero, PROMPT_ROW_TILE,
                                              MLSTM_CHUNK, PROMPT_ROW_TILE, RWKV_STEPS, final_norm_g, last)
        keep = min(WINDOW, tp)
        states_p.append(tuple(kvshape(r) for r in rows[:4])
                        + (kvshape(rows[4][:, tp - keep:]), kvshape(rows[5][:, tp - keep:])) + m_new + r_new)

        def nsa_s(a_q, a_gate, rows, l=l):
            return nsa_sample(a_q, a_gate, rows, pools, win_k, win_v, l, page_table, lw['cwk'], lw['cwv'])

        xs, rows, m_new, r_new = _trunk_layer(
            xs, lw, lp, nsa_s,
            (state_mlstm_conv[l], state_mlstm_C[l], state_mlstm_n[l], state_mlstm_m[l]),
            (state_rwkv_shift[l], state_rwkv[l]), bs * ts, SUBLANES, SUBLANES, SUBLANES, final_norm_g, last)
        keep = win_k.shape[3]

        def win_new(win, new_row):
            w = jnp.concatenate([win[l], jnp.swapaxes(new_row, 1, 2)], axis=2)[:, :, -keep:]
            return jnp.transpose(w.reshape(bs, A_KV_HEADS, A_HEAD_DIM, keep), (0, 3, 1, 2))

        states_s.append(tuple(kvshape(r) for r in rows[:4])
                        + (win_new(win_k, rows[4]), win_new(win_v, rows[5])) + m_new + r_new)

    stack = lambda states: [jnp.stack(a) for a in zip(*states)]
    return (xp, xs, *stack(states_p), *stack(states_s))
```

```python
import functools
import math

import numpy as np
import jax
import jax.numpy as jnp
from jax import lax
from jax.experimental import pallas as pl
from jax.experimental.pallas import tpu as pltpu

F32 = jnp.float32
BF16 = jnp.bfloat16

D_MODEL = 1024
DEPTH = 2
PAGE_SIZE = 128
M_HEADS = 4
M_HEAD_DIM = 128
M_WIDTH = M_HEADS * M_HEAD_DIM
M_CONV = 4
A_HEADS = 8
A_KV_HEADS = 2
A_HEAD_DIM = 64
A_WIDTH = A_HEADS * A_HEAD_DIM
A_GROUP = A_HEADS // A_KV_HEADS
A_SCALE = A_HEAD_DIM ** -0.5
A_KV_LANES = A_KV_HEADS * A_HEAD_DIM
CMP_STRIDE = 16
CMP_BLOCK = 2 * CMP_STRIDE
CMP_HIDDEN = A_HEAD_DIM
SEL_BLOCK = 64
N_SEL = 16
WINDOW = 512
Q_BLOCK = 128
R_HEADS = 8
R_HEAD_DIM = 64
R_WIDTH = R_HEADS * R_HEAD_DIM
R_DECAY_RANK = 64
R_A_RANK = 64
R_GATE_RANK = 128
R_COLS = 3 * R_WIDTH + R_DECAY_RANK + R_A_RANK + R_GATE_RANK
LN_X_EPS = 64e-5
D_FF = ((-(-8 * D_MODEL // 3)) + 255) // 256 * 256
IN_SPLITS = (2 * M_WIDTH, M_WIDTH, M_WIDTH, 2 * M_HEADS,
             A_WIDTH, 6 * A_KV_HEADS * A_HEAD_DIM, 3 * A_HEADS,
             R_COLS, 3 * D_MODEL)
RMS_EPS = 1e-6
NEG = -1e30
FORCE = 1e9
LOWEST = -3e38

VMEM_LIMIT_BYTES = 56 * 1024 * 1024
SUBLANES = 8
LANES = 128


def _cparams(*sem):
    return pltpu.CompilerParams(dimension_semantics=sem, vmem_limit_bytes=VMEM_LIMIT_BYTES)


def _resident(shape):
    nd = len(shape)
    return pl.BlockSpec(shape, lambda *_: (0,) * nd, pipeline_mode=pl.Buffered(1))


def _dot(a, b):
    return jnp.dot(a.astype(BF16), b.astype(BF16), preferred_element_type=F32)


def _dot_nt(a, b):
    return lax.dot_general(a.astype(BF16), b.astype(BF16), (((1,), (1,)), ((), ())),
                           preferred_element_type=F32)


def _split3(x):
    hi = x.astype(BF16)
    r = x - hi.astype(F32)
    mid = r.astype(BF16)
    lo = (r - mid.astype(F32)).astype(BF16)
    return hi, mid, lo


def _dot_f32_rhs(a_bf16, x):
    hi, mid, lo = _split3(x)
    d = lambda p: jnp.dot(a_bf16, p, preferred_element_type=F32)
    return d(hi) + d(mid) + d(lo)


def _dot_f32_rhs_nt(a_bf16, x):
    hi, mid, lo = _split3(x)
    d = lambda p: lax.dot_general(a_bf16, p, (((1,), (1,)), ((), ())), preferred_element_type=F32)
    return d(hi) + d(mid) + d(lo)


def _transpose_f32(x, eye_bf16):
    hi, mid, lo = _split3(x)
    d = lambda p: lax.dot_general(eye_bf16, p, (((1,), (1,)), ((), ())), preferred_element_type=F32)
    return d(hi) + d(mid) + d(lo)


def _eye(n, dtype=BF16):
    r = lax.broadcasted_iota(jnp.int32, (n, n), 0)
    c = lax.broadcasted_iota(jnp.int32, (n, n), 1)
    return (r == c).astype(dtype)


def _sigmoid(x):
    return 1.0 / (1.0 + jnp.exp(-x))


def _silu(x):
    return x * _sigmoid(x)


def _softplus(x):
    return jnp.maximum(x, 0.0) + jnp.log(1.0 + jnp.exp(-jnp.abs(x)))


def _rmsnorm(x, g):
    return x * lax.rsqrt(jnp.mean(x * x, axis=-1, keepdims=True) + RMS_EPS) * g


def _norm_proj_kernel(x_ref, g_ref, *refs):
    n = len(refs) // 2
    hb = _rmsnorm(x_ref[...], g_ref[...]).astype(BF16)
    for w_ref, o_ref in zip(refs[:n], refs[n:]):
        o_ref[...] = jnp.dot(hb, w_ref[...], preferred_element_type=F32)


def norm_proj(x2d, g, weights, tm):
    m, d = x2d.shape
    assert m % tm == 0
    row = lambda i: (i, 0)
    in_specs = [pl.BlockSpec((tm, d), row), _resident((1, d))]
    in_specs += [_resident(w.shape) for w in weights]
    out_specs = [pl.BlockSpec((tm, w.shape[1]), row) for w in weights]
    out_shape = [jax.ShapeDtypeStruct((m, w.shape[1]), F32) for w in weights]
    return pl.pallas_call(
        _norm_proj_kernel, grid=(m // tm,), in_specs=in_specs, out_specs=out_specs,
        out_shape=out_shape, compiler_params=_cparams("parallel"), name="norm_proj",
    )(x2d, g.reshape(1, d), *weights)


def _merge_ffn_kernel(x_ref, ym_ref, ya_ref, yr_ref, mg_ref, wm_ref, wa_ref, wr_ref, wo_ref,
                      g2_ref, wgu_ref, wd_ref, gf_ref, o_ref, *, final_norm):
    mg = mg_ref[...]
    merged = (_sigmoid(mg[:, :D_MODEL]) * _dot(ym_ref[...], wm_ref[...])
              + _sigmoid(mg[:, D_MODEL:2 * D_MODEL]) * _dot(ya_ref[...], wa_ref[...])
              + _sigmoid(mg[:, 2 * D_MODEL:]) * _dot(yr_ref[...], wr_ref[...]))
    x1 = x_ref[...] + _dot(merged, wo_ref[...])
    gu = _dot(_rmsnorm(x1, g2_ref[...]), wgu_ref[...])
    x2 = x1 + _dot(_silu(gu[:, :D_FF]) * gu[:, D_FF:], wd_ref[...])
    o_ref[...] = _rmsnorm(x2, gf_ref[...]) if final_norm else x2


def merge_ffn(x2d, ym, ya, yr, mg, wm, wa, wr, wo, g2, wgu, wd, gf, tm, final_norm):
    m, d = x2d.shape
    row = lambda i: (i, 0)
    acts = [x2d, ym, ya, yr, mg]
    consts = [wm, wa, wr, wo, g2.reshape(1, d), wgu, wd, gf.reshape(1, d)]
    in_specs = [pl.BlockSpec((tm, a.shape[1]), row) for a in acts]
    in_specs += [_resident(c.shape) for c in consts]
    return pl.pallas_call(
        functools.partial(_merge_ffn_kernel, final_norm=final_norm),
        grid=(m // tm,), in_specs=in_specs, out_specs=pl.BlockSpec((tm, d), row),
        out_shape=jax.ShapeDtypeStruct((m, d), F32), compiler_params=_cparams("parallel"),
        name="merge_ffn",
    )(*acts, *consts)


def _mlstm_kernel(qk_ref, v_ref, o_ref, if_ref, conv0_ref, c0_ref, n0_ref, m0_ref,
                  cw_ref, cb_ref, ifb_ref,
                  y_ref, conv_out_ref, c_out_ref, n_out_ref, m_out_ref,
                  cbuf, c_scr, n_scr, m_scr, *, chunk, n_last):
    c = pl.program_id(1)
    nc = pl.num_programs(1)
    L = chunk
    pad = SUBLANES

    @pl.when(c == 0)
    def _():
        cbuf[0:pad, :] = conv0_ref[0]
        c_scr[...] = c0_ref[0]
        n_scr[...] = n0_ref[0]
        m_scr[...] = m0_ref[0]

    cbuf[pad:pad + L, :] = qk_ref[0]
    acc = cb_ref[...]
    for j in range(M_CONV):
        acc = acc + cbuf[pad - (M_CONV - 1) + j: pad - (M_CONV - 1) + j + L, :] * cw_ref[j:j + 1, :]
    qk = _silu(acc)

    @pl.when(c == nc - 1)
    def _():
        conv_out_ref[0] = cbuf[n_last:n_last + pad, :]

    cbuf[0:pad, :] = cbuf[L:L + pad, :]

    gates = if_ref[0] + ifb_ref[...]
    lane8 = lax.broadcasted_iota(jnp.int32, (L, 2 * M_HEADS), 1)
    lf_all = -_softplus(-gates)
    lif = jnp.where(lane8 < M_HEADS, gates, lf_all)
    row_id = lax.broadcasted_iota(jnp.int32, (L, 2 * M_HEADS), 0)
    is_last_chunk = c == nc - 1
    if n_last < L:
        dead = jnp.logical_and(is_last_chunk, row_id >= n_last)
        lif = jnp.where(dead, jnp.where(lane8 < M_HEADS, NEG, 0.0), lif)
    t_i = lax.broadcasted_iota(jnp.int32, (L, L), 0)
    s_i = lax.broadcasted_iota(jnp.int32, (L, L), 1)
    causal = s_i <= t_i
    tri = causal.astype(BF16)
    csum = _dot_f32_rhs(tri, lif)
    lib = jnp.where(lane8 < M_HEADS, lif, csum)
    lib_t = _transpose_f32(lib, _eye(2 * M_HEADS))
    eye_h = _eye(M_HEAD_DIM)
    last = n_last - 1

    heads = range(M_HEADS)
    sl = [slice(h * M_HEAD_DIM, (h + 1) * M_HEAD_DIM) for h in heads]
    q = [qk[:, sl[h]] for h in heads]
    k = [qk[:, M_WIDTH + h * M_HEAD_DIM: M_WIDTH + (h + 1) * M_HEAD_DIM] * (M_HEAD_DIM ** -0.5) for h in heads]
    v = [v_ref[0, :, sl[h]] for h in heads]
    li_col = [lib[:, h:h + 1] for h in heads]
    b_col = [lib[:, M_HEADS + h:M_HEADS + h + 1] for h in heads]
    m_prev = [m_scr[0:1, h:h + 1] for h in heads]
    c_old = [c_scr[h] for h in heads]
    n_old = [n_scr[h:h + 1, :] for h in heads]
    s_qk = [_dot_nt(q[h], k[h]) for h in heads]
    q_c = [_dot(q[h], c_old[h]) for h in heads]
    dmat = [jnp.where(causal, b_col[h] - lib_t[M_HEADS + h:M_HEADS + h + 1, :] + lib_t[h:h + 1, :], NEG)
            for h in heads]
    inter = [b_col[h] + m_prev[h] for h in heads]
    m_t = [jnp.maximum(inter[h], jnp.max(dmat[h], axis=1, keepdims=True)) for h in heads]
    att = [jnp.exp(dmat[h] - m_t[h]) * s_qk[h] for h in heads]
    w_inter = [jnp.exp(inter[h] - m_t[h]) for h in heads]
    num = [_dot(att[h], v[h]) + w_inter[h] * q_c[h] for h in heads]
    den = [jnp.sum(att[h], axis=1, keepdims=True) + w_inter[h] * jnp.sum(q[h] * n_old[h], axis=1, keepdims=True)
           for h in heads]
    for h in heads:
        hh = num[h] / jnp.maximum(jnp.abs(den[h]), jnp.exp(-m_t[h]))
        y_ref[0, :, sl[h]] = _sigmoid(o_ref[0, :, sl[h]]) * hh

    m_new = [m_t[h][last:last + 1, :] for h in heads]
    b_last = [b_col[h][last:last + 1, :] for h in heads]
    scale = [jnp.exp(b_last[h] + m_prev[h] - m_new[h]) for h in heads]
    kw = [k[h] * jnp.exp(b_last[h] - b_col[h] + li_col[h] - m_new[h]) for h in heads]
    kw_t = [lax.dot_general(eye_h, kw[h].astype(BF16), DOT_NT, preferred_element_type=F32)
            for h in heads]
    for h in heads:
        c_scr[h] = scale[h] * c_old[h] + _dot(kw_t[h], v[h])
        n_scr[h:h + 1, :] = scale[h] * n_old[h] + jnp.sum(kw[h], axis=0, keepdims=True)
        m_scr[0:1, h:h + 1] = m_new[h]

    @pl.when(is_last_chunk)
    def _():
        c_out_ref[0] = c_scr[...]
        n_out_ref[0] = n_scr[...]
        m_out_ref[0] = m_scr[...]


def mlstm(m_qk, m_v, m_o, m_if, conv0, c0, n0, m0, conv_w, conv_b, if_bias, chunk):
    B, T, _ = m_qk.shape
    nc = -(-T // chunk)
    tp = nc * chunk
    n_last = T - (nc - 1) * chunk
    if tp != T:
        padt = lambda a: jnp.pad(a, ((0, 0), (0, tp - T), (0, 0)))
        m_qk, m_v, m_o, m_if = padt(m_qk), padt(m_v), padt(m_o), padt(m_if)
    conv0p = jnp.pad(conv0, ((0, 0), (SUBLANES - (M_CONV - 1), 0), (0, 0)))
    n0p = jnp.pad(n0, ((0, 0), (0, SUBLANES - M_HEADS), (0, 0)))
    m0p = jnp.pad(m0, ((0, 0), (0, LANES - M_HEADS))).reshape(B, 1, LANES)
    cw = jnp.pad(conv_w, ((0, SUBLANES - M_CONV), (0, 0)))
    tile = lambda w: pl.BlockSpec((1, chunk, w), lambda b, c: (b, c, 0))
    perb = lambda *s: pl.BlockSpec((1,) + s, lambda b, c: (b,) + (0,) * len(s))
    outs = pl.pallas_call(
        functools.partial(_mlstm_kernel, chunk=chunk, n_last=n_last),
        grid=(B, nc),
        in_specs=[tile(2 * M_WIDTH), tile(M_WIDTH), tile(M_WIDTH), tile(2 * M_HEADS),
                  perb(SUBLANES, 2 * M_WIDTH), perb(M_HEADS, M_HEAD_DIM, M_HEAD_DIM),
                  perb(SUBLANES, M_HEAD_DIM), perb(1, LANES),
                  _resident((SUBLANES, 2 * M_WIDTH)), _resident((1, 2 * M_WIDTH)),
                  _resident((1, 2 * M_HEADS))],
        out_specs=[tile(M_WIDTH), perb(SUBLANES, 2 * M_WIDTH),
                   perb(M_HEADS, M_HEAD_DIM, M_HEAD_DIM), perb(SUBLANES, M_HEAD_DIM), perb(1, LANES)],
        out_shape=[jax.ShapeDtypeStruct((B, tp, M_WIDTH), F32),
                   jax.ShapeDtypeStruct((B, SUBLANES, 2 * M_WIDTH), F32),
                   jax.ShapeDtypeStruct((B, M_HEADS, M_HEAD_DIM, M_HEAD_DIM), F32),
                   jax.ShapeDtypeStruct((B, SUBLANES, M_HEAD_DIM), F32),
                   jax.ShapeDtypeStruct((B, 1, LANES), F32)],
        scratch_shapes=[pltpu.VMEM((chunk + SUBLANES, 2 * M_WIDTH), F32),
                        pltpu.VMEM((M_HEADS, M_HEAD_DIM, M_HEAD_DIM), F32),
                        pltpu.VMEM((SUBLANES, M_HEAD_DIM), F32),
                        pltpu.VMEM((1, LANES), F32)],
        compiler_params=_cparams("parallel", "arbitrary"), name="mlstm",
    )(m_qk, m_v, m_o, m_if, conv0p, c0, n0p, m0p, cw, conv_b.reshape(1, -1), if_bias.reshape(1, -1))
    y, conv_o, c_o, n_o, m_o_ = outs
    return (y[:, :T], conv_o[:, SUBLANES - (M_CONV - 1):], c_o, n_o[:, :M_HEADS], m_o_[:, 0, :M_HEADS])


def _rwkv_project(cols, prev, mu_ref, w0_ref, w2_ref, a0_ref, a2_ref, g2_ref, kk_ref, ka_ref, rk_ref):
    mixed = cols + (prev - cols) * mu_ref[...]
    W = R_WIDTH
    r = mixed[:, :W]
    k = mixed[:, W:2 * W]
    v = mixed[:, 2 * W:3 * W]
    o = 3 * W
    wd = mixed[:, o:o + R_DECAY_RANK]
    ad = mixed[:, o + R_DECAY_RANK:o + R_DECAY_RANK + R_A_RANK]
    gd = mixed[:, o + R_DECAY_RANK + R_A_RANK:]
    w = -_softplus(-(w0_ref[...] + _dot(jnp.tanh(wd), w2_ref[...]))) - 0.5
    a_gate = _sigmoid(a0_ref[...] + _dot(ad, a2_ref[...]))
    g = _dot(_sigmoid(gd), g2_ref[...])
    kk = k * kk_ref[...]
    k2 = k * (1.0 + (a_gate - 1.0) * ka_ref[...])
    rkk = r * k2 * rk_ref[...]
    a_h, b_h, bonus_h = [], [], []
    for h in range(R_HEADS):
        sl = slice(h * R_HEAD_DIM, (h + 1) * R_HEAD_DIM)
        kkh = kk[:, sl]
        kkh = kkh * lax.rsqrt(jnp.maximum(jnp.sum(kkh * kkh, axis=1, keepdims=True), 1e-24))
        a_h.append(-kkh)
        b_h.append(kkh * a_gate[:, sl])
        bonus_h.append(jnp.sum(rkk[:, sl], axis=1, keepdims=True) * v[:, sl])
    return r, -jnp.exp(w), k2, v, g, a_h, b_h, bonus_h


RWKV_SEQS_PER_STEP = 4
DOT_NN = (((1,), (0,)), ((), ()))
DOT_NT = (((1,), (1,)), ((), ()))
DOT_TN = (((0,), (0,)), ((), ()))


def _dot_x3(x, y, dims):
    xh = x.astype(BF16)
    xl = (x - xh.astype(F32)).astype(BF16)
    yh = y.astype(BF16)
    yl = (y - yh.astype(F32)).astype(BF16)
    d = lambda p, q: lax.dot_general(p, q, dims, preferred_element_type=F32)
    return d(xh, yh) + d(xh, yl) + d(xl, yh)


def _dot_tn(a, b):
    return lax.dot_general(a.astype(BF16), b.astype(BF16), DOT_TN, preferred_element_type=F32)


def _rwkv_chunk_kernel(cols_ref, shift0_ref, mu_ref, w0_ref, w2_ref, a0_ref, a2_ref, g2_ref, kk_ref, ka_ref,
                       rk_ref, lng_ref, lnb_ref, s0_ref, o_ref, s_out_ref, s_scr, carry, *, chunk, n_valid):
    t = pl.program_id(1)
    nt = pl.num_programs(1)
    L, N = chunk, R_HEAD_DIM

    @pl.when(t == 0)
    def _():
        s_scr[...] = s0_ref[...]
        carry[...] = shift0_ref[...]

    ti = lax.broadcasted_iota(jnp.int32, (L, L), 0)
    si = lax.broadcasted_iota(jnp.int32, (L, L), 1)
    incl = si <= ti
    strict = si < ti
    eye = (si == ti).astype(F32)
    tri = incl.astype(BF16)
    first_row = lax.broadcasted_iota(jnp.int32, (L, R_COLS), 0) == 0
    n_seq = cols_ref.shape[0]
    ah, rh, bt, kt, vh, bb, kb, gm, gate, bonus = [], [], [], [], [], [], [], [], [], []
    for i in range(n_seq):
        cols = cols_ref[i]
        prev = jnp.where(first_row, carry[i], pltpu.roll(cols, 1, 0))
        carry[i] = cols[L - 1:L, :]
        r, lw, k, v, g, a_h, b_h, bonus_h = _rwkv_project(cols, prev, mu_ref, w0_ref, w2_ref, a0_ref, a2_ref,
                                                          g2_ref, kk_ref, ka_ref, rk_ref)
        bonus += bonus_h
        if n_valid % L:
            live = (t * L + lax.broadcasted_iota(jnp.int32, (L, R_WIDTH), 0)) < n_valid
            lw, k, v = (jnp.where(live, z, 0.0) for z in (lw, k, v))
            a_h = [jnp.where(live[:, :N], z, 0.0) for z in a_h]
            b_h = [jnp.where(live[:, :N], z, 0.0) for z in b_h]
        c = _dot_f32_rhs(tri, lw)
        c_end = c[L - 1:L, :]
        e_prev = jnp.exp(c - lw)
        r_hat = r * jnp.exp(c)
        e_neg = jnp.exp(-c)
        e_end = jnp.exp(c_end - c)
        gam = jnp.exp(c_end)
        for h in range(R_HEADS):
            hs = slice(h * N, (h + 1) * N)
            ah.append(a_h[h] * e_prev[:, hs])
            rh.append(r_hat[:, hs])
            bt.append(b_h[h] * e_neg[:, hs])
            kt.append(k[:, hs] * e_neg[:, hs])
            vh.append(v[:, hs])
            bb.append(b_h[h] * e_end[:, hs])
            kb.append(k[:, hs] * e_end[:, hs])
            gm.append(gam[:, hs])
            gate.append(g[:, hs])

    heads = range(n_seq * R_HEADS)
    unit = [(i, h, slice(h * N, (h + 1) * N)) for i in range(n_seq) for h in range(R_HEADS)]
    a_ab = [jnp.where(strict, _dot_x3(ah[h], bt[h], DOT_NT), 0.0) for h in heads]
    a_ak = [jnp.where(strict, _dot_nt(ah[h], kt[h]), 0.0) for h in heads]
    a_rb = [jnp.where(incl, _dot_nt(rh[h], bt[h]), 0.0) for h in heads]
    a_rk = [jnp.where(incl, _dot_nt(rh[h], kt[h]), 0.0) for h in heads]
    inv = [eye + a_ab[h] for h in heads]
    pw = a_ab
    span = 2
    while span < L:
        pw = [_dot(pw[h], pw[h]) for h in heads]
        inv = [inv[h] + _dot(inv[h], pw[h]) for h in heads]
        span *= 2
    s = [s_scr[i, h] for i, h, _ in unit]
    rhs = [_dot_nt(ah[h], s[h]) + _dot(a_ak[h], vh[h]) for h in heads]
    u = [_dot(inv[h], rhs[h]) for h in heads]
    y = [_dot_nt(rh[h], s[h]) + _dot(a_rb[h], u[h]) + _dot(a_rk[h], vh[h]) for h in heads]
    for n, (i, h, hs) in enumerate(unit):
        s_scr[i, h] = s[n] * gm[n] + _dot_tn(u[n], bb[n]) + _dot_tn(vh[n], kb[n])
    for n, (i, h, hs) in enumerate(unit):
        mean = jnp.mean(y[n], axis=1, keepdims=True)
        yc = y[n] - mean
        var = jnp.mean(yc * yc, axis=1, keepdims=True)
        yn = yc * lax.rsqrt(var + LN_X_EPS) * lng_ref[:, hs] + lnb_ref[:, hs]
        o_ref[i, :, hs] = (yn + bonus[n]) * gate[n]

    @pl.when(t == nt - 1)
    def _():
        s_out_ref[...] = s_scr[...]


def rwkv(cols, shift0, s0, p, chunk):
    B, T, _ = cols.shape
    tp = -(-T // chunk) * chunk
    assert chunk % SUBLANES == 0 and chunk & (chunk - 1) == 0
    colsp = jnp.pad(cols, ((0, 0), (0, tp - T), (0, 0))) if tp != T else cols
    vec = lambda a: a.reshape(1, -1)
    consts = [vec(p['rwkv_mu']), vec(p['rwkv_w0']), p['rwkv_w2'].astype(BF16), vec(p['rwkv_a0']),
              p['rwkv_a2'].astype(BF16), p['rwkv_g2'].astype(BF16), vec(p['rwkv_k_k']),
              vec(p['rwkv_k_a']), vec(p['rwkv_r_k']), vec(p['rwkv_ln_g']), vec(p['rwkv_ln_b'])]
    ns = RWKV_SEQS_PER_STEP
    assert B % ns == 0
    seq_spec = lambda w: pl.BlockSpec((ns, chunk, w), lambda b, t: (b, t, 0))
    st_spec = pl.BlockSpec((ns, R_HEADS, R_HEAD_DIM, R_HEAD_DIM), lambda b, t: (b, 0, 0, 0))
    out, s_new = pl.pallas_call(
        functools.partial(_rwkv_chunk_kernel, chunk=chunk, n_valid=T),
        grid=(B // ns, tp // chunk),
        in_specs=[seq_spec(R_COLS), pl.BlockSpec((ns, 1, R_COLS), lambda b, t: (b, 0, 0))]
                 + [_resident(c.shape) for c in consts] + [st_spec],
        out_specs=[seq_spec(R_WIDTH), st_spec],
        out_shape=[jax.ShapeDtypeStruct((B, tp, R_WIDTH), F32), jax.ShapeDtypeStruct(s0.shape, F32)],
        scratch_shapes=[pltpu.VMEM((ns, R_HEADS, R_HEAD_DIM, R_HEAD_DIM), F32),
                        pltpu.VMEM((ns, 1, R_COLS), F32)],
        compiler_params=_cparams("parallel", "arbitrary"), name="rwkv_chunk",
    )(colsp, shift0, *consts, s0)
    return out[:, :T], s_new


def compress_weights(pe, w1, w2):
    d, hdim, G = A_HEAD_DIM, CMP_HIDDEN, A_KV_HEADS
    w1r = w1.reshape(2, CMP_STRIDE, d, hdim)
    wl = jnp.zeros((CMP_STRIDE, G, d, 2, G, hdim), F32)
    for g in range(G):
        wl = wl.at[:, g, :, :, g, :].set(jnp.transpose(w1r, (1, 2, 0, 3)))
    wl = wl.reshape(CMP_STRIDE * G * d, 2 * G * hdim).astype(BF16)
    wpe = jnp.transpose(w1r, (1, 0, 2, 3)).reshape(CMP_STRIDE, 2 * d, hdim)
    wpe = jnp.concatenate([wpe] * G, axis=-1).astype(BF16)
    pe_l = jnp.transpose(pe.reshape(2, CMP_STRIDE, d), (1, 0, 2)).reshape(CMP_STRIDE, 1, 2 * d)
    pe_l = jnp.broadcast_to(pe_l, (CMP_STRIDE, SUBLANES, 2 * d))
    w2bd = jnp.zeros((G, hdim, G, d), F32)
    for g in range(G):
        w2bd = w2bd.at[g, :, g, :].set(w2)
    return wl, wpe, pe_l, w2bd.reshape(G * hdim, G * d).astype(BF16)


def _compress_halves(strided_rows, wl_ref, n):
    x = jnp.concatenate([strided_rows(l).astype(BF16) for l in range(CMP_STRIDE)], axis=1)
    return jnp.dot(x, wl_ref[...], preferred_element_type=F32)


def _compress_finish(halves, wpe_ref, pe_ref, w2_ref):
    n = halves.shape[0]
    gh = A_KV_HEADS * CMP_HIDDEN
    pe_term = None
    for l in range(CMP_STRIDE):
        part = _dot(pe_ref[l], wpe_ref[l])
        pe_term = part if pe_term is None else pe_term + part
    second = pltpu.roll(halves[:, gh:], n - 1, 0)
    hid = _silu(halves[:, :gh] + second + pe_term[0:1, :])
    return _dot(hid, w2_ref[...])


def _compress_kernel(x_ref, wl_ref, wpe_ref, pe_ref, w2_ref, o_ref):
    n = o_ref.shape[1]
    rows = lambda l: x_ref[0, pl.ds(l, n, stride=CMP_STRIDE), :]
    o_ref[0] = _compress_finish(_compress_halves(rows, wl_ref, n), wpe_ref, pe_ref, w2_ref)


def compress_rows(rows, cw):
    B, T, W = rows.shape
    n = T // CMP_STRIDE
    return pl.pallas_call(
        _compress_kernel, grid=(B,),
        in_specs=[pl.BlockSpec((1, T, W), lambda b: (b, 0, 0))] + [_resident(c.shape) for c in cw],
        out_specs=pl.BlockSpec((1, n, W), lambda b: (b, 0, 0)),
        out_shape=jax.ShapeDtypeStruct((B, n, W), F32),
        compiler_params=_cparams("parallel"), name="nsa_compress",
    )(rows, *cw)


def _topk_mask_columns(rank, k):
    n_entries = rank.shape[0]
    row = lax.broadcasted_iota(jnp.int32, rank.shape, 0).astype(F32)
    sel = jnp.zeros(rank.shape, F32)
    for _ in range(k):
        m = jnp.max(rank, axis=0, keepdims=True)
        first = jnp.min(jnp.where(rank == m, row, float(n_entries)), axis=0, keepdims=True)
        hit = row == first
        sel = jnp.where(hit, 1.0, sel)
        rank = jnp.where(hit, LOWEST, rank)
    return sel


def _flash_update(state, s, bias, v):
    m_old, l_old, acc = state
    s = s + bias
    m_new = jnp.maximum(m_old, jnp.max(s, axis=1, keepdims=True))
    p = jnp.exp(s - m_new)
    alpha = jnp.exp(m_old - m_new)
    return (m_new, alpha * l_old + jnp.sum(p, axis=1, keepdims=True), alpha * acc + _dot(p, v))


def _nsa_prompt_kernel(q_ref, gate_ref, kc_ref, vc_ref, ks_ref, vs_ref, kw_ref, vw_ref, covt_ref,
                       o_ref, *, seq, tk, wk):
    i = pl.program_id(1)
    QB, d, J = Q_BLOCK, A_HEAD_DIM, A_GROUP
    nc = kc_ref.shape[1]
    nb = covt_ref.shape[0]
    st = i * QB
    q = q_ref[0] * A_SCALE
    gate = _sigmoid(gate_ref[0])
    rows = J * QB
    t_q = st + lax.broadcasted_iota(jnp.int32, (QB, 1), 0)
    t_rows = jnp.concatenate([t_q] * J, axis=0)
    blk_end = lax.broadcasted_iota(jnp.int32, (rows, nc), 1) * CMP_STRIDE + (CMP_BLOCK - 1)
    cmp_mask = blk_end <= t_rows
    jj = lax.broadcasted_iota(jnp.int32, (nb, QB), 0)
    cur = (st + lax.broadcasted_iota(jnp.int32, (1, QB), 1)) // SEL_BLOCK
    forced = (jj == 0) | (jj == cur) | (jj == cur - 1)
    zeros = lambda w: jnp.zeros((rows, w), F32)

    qgs, o_cmps, ranks = [], [], []
    for g in range(A_KV_HEADS):
        gl = slice(g * d, (g + 1) * d)
        qg = jnp.concatenate([q[:, (g * J + j) * d:(g * J + j + 1) * d] for j in range(J)],
                             axis=0).astype(BF16)
        s = jnp.where(cmp_mask, _dot_nt(qg, kc_ref[0, :, gl]), NEG)
        e = jnp.exp(s - jnp.max(s, axis=1, keepdims=True))
        p = jnp.where(cmp_mask, e / jnp.sum(e, axis=1, keepdims=True), 0.0)
        p_grp = p[0:QB]
        for j in range(1, J):
            p_grp = p_grp + p[j * QB:(j + 1) * QB]
        score = _dot_f32_rhs_nt(covt_ref[...], p_grp)
        qgs.append(qg)
        o_cmps.append(_dot(p, vc_ref[0, :, gl]))
        ranks.append(jnp.where(jj <= cur, jnp.where(forced, FORCE, score), NEG))
    sel_all = _topk_mask_columns(jnp.concatenate(ranks, axis=1), min(N_SEL, nb)).astype(BF16)

    k0w = pl.multiple_of(jnp.clip(st + QB - wk, 0, seq - wk), QB)
    rel = t_q - (k0w + lax.broadcasted_iota(jnp.int32, (1, wk), 1))
    win_bias = jnp.concatenate([jnp.where((rel >= 0) & (rel < WINDOW), 0.0, NEG)] * J, axis=0)

    for g in range(A_KV_HEADS):
        gl = slice(g * d, (g + 1) * d)
        qg, o_cmp = qgs[g], o_cmps[g]
        sel = sel_all[:, g * QB:(g + 1) * QB]

        def sel_tile(kt, state):
            k0 = pl.multiple_of(kt * tk, tk)
            s_pos = k0 + lax.broadcasted_iota(jnp.int32, (1, tk), 1)
            blk_of = k0 // SEL_BLOCK + lax.broadcasted_iota(jnp.int32, (nb, tk), 1) // SEL_BLOCK
            expand = (lax.broadcasted_iota(jnp.int32, (nb, tk), 0) == blk_of).astype(BF16)
            chosen = lax.dot_general(sel, expand, DOT_TN, preferred_element_type=F32) > 0.5
            bias = jnp.where(chosen & (s_pos <= t_q), 0.0, NEG)
            sc = _dot_nt(qg, ks_ref[0, pl.ds(k0, tk), gl])
            return _flash_update(state, sc, jnp.concatenate([bias] * J, axis=0), vs_ref[0, pl.ds(k0, tk), gl])

        init = (jnp.full((rows, 1), NEG, F32), jnp.zeros((rows, 1), F32), zeros(d))
        _, l_sel, acc_sel = lax.fori_loop(0, (st + QB - 1) // tk + 1, sel_tile, init)
        o_sel = acc_sel / l_sel

        sw = _dot_nt(qg, kw_ref[0, pl.ds(k0w, wk), gl]) + win_bias
        pw = jnp.exp(sw - jnp.max(sw, axis=1, keepdims=True))
        o_win = _dot(pw, vw_ref[0, pl.ds(k0w, wk), gl]) / jnp.sum(pw, axis=1, keepdims=True)

        for j in range(J):
            h = g * J + j
            rs = slice(j * QB, (j + 1) * QB)
            o_ref[0, :, h * d:(h + 1) * d] = (gate[:, h:h + 1] * o_cmp[rs]
                                              + gate[:, A_HEADS + h:A_HEADS + h + 1] * o_sel[rs]
                                              + gate[:, 2 * A_HEADS + h:2 * A_HEADS + h + 1] * o_win[rs])


def _covers(n_cmp, n_blocks):
    i = np.arange(n_cmp)[:, None]
    j = np.arange(n_blocks)[None, :]
    return (i * CMP_STRIDE < (j + 1) * SEL_BLOCK) & (i * CMP_STRIDE + CMP_BLOCK > j * SEL_BLOCK)


def nsa_prompt_attend(a_q, a_gate, kc, vc, ks, vs, kw, vw):
    B, T, _ = a_q.shape
    nc = kc.shape[1]
    nb = -(-T // SEL_BLOCK)
    tk = min(4 * Q_BLOCK, T)
    wk = min(WINDOW + Q_BLOCK, T)
    cov = jnp.asarray(_covers(nc, nb).T, BF16)
    tile = lambda w: pl.BlockSpec((1, Q_BLOCK, w), lambda b, i: (b, i, 0))
    full = lambda n, w: pl.BlockSpec((1, n, w), lambda b, i: (b, 0, 0))
    W = A_KV_LANES
    return pl.pallas_call(
        functools.partial(_nsa_prompt_kernel, seq=T, tk=tk, wk=wk),
        grid=(B, T // Q_BLOCK),
        in_specs=[tile(A_WIDTH), tile(3 * A_HEADS), full(nc, W), full(nc, W),
                  full(T, W), full(T, W), full(T, W), full(T, W), _resident(cov.shape)],
        out_specs=tile(A_WIDTH),
        out_shape=jax.ShapeDtypeStruct((B, T, A_WIDTH), F32),
        compiler_params=_cparams("parallel", "arbitrary"), name="nsa_prompt",
    )(a_q, a_gate, kc, vc, ks, vs, kw, vw, cov)


GATHER_PAGES = 16
CHUNKS_PER_PAGE = PAGE_SIZE // CMP_STRIDE
HEAD_ROWS = SUBLANES
PICK_ROWS = 2 * SUBLANES


def channel_major(cache):
    L, n, tok, G, d = cache.shape
    return jnp.transpose(cache, (0, 1, 3, 4, 2)).reshape(L, n, G * d, tok)


def _nsa_sample_cmp_kernel(pt_ref, *refs, past_len):
    del pt_ref
    ng = GATHER_PAGES
    pk_refs, pv_refs = refs[:ng], refs[ng:2 * ng]
    (q_ref, wink_ref, winv_ref, kwn_ref, vwn_ref, wlk_ref, wpek_ref, pek_ref, w2k_ref,
     wlv_ref, wpev_ref, pev_ref, w2v_ref, cov_ref, ocmp_ref, owin_ref, idx_ref,
     hk_scr, hv_scr, xk_scr, xv_scr) = refs[2 * ng:]
    c = pl.program_id(1)
    page = pk_refs[0].shape[3]
    rows_per_step = ng * CHUNKS_PER_PAGE
    base = pl.multiple_of(c * rows_per_step, rows_per_step)
    for page_refs, xs, wl_ref, h_scr in ((pk_refs, xk_scr, wlk_ref, hk_scr), (pv_refs, xv_scr, wlv_ref, hv_scr)):
        for i in range(ng):
            xs[i * page:(i + 1) * page, :] = page_refs[i][0, 0].T
        rows = lambda l, xs=xs: xs[pl.ds(l, rows_per_step, stride=CMP_STRIDE), :]
        h_scr[pl.ds(base, rows_per_step), :] = _compress_halves(rows, wl_ref, rows_per_step)

    @pl.when(c == pl.num_programs(1) - 1)
    def _():
        d, R8 = A_HEAD_DIM, HEAD_ROWS
        kc = _compress_finish(hk_scr[...], wpek_ref, pek_ref, w2k_ref)
        vc = _compress_finish(hv_scr[...], wpev_ref, pev_ref, w2v_ref)
        nc = kc.shape[0]
        nb = cov_ref.shape[0]
        keep = wink_ref.shape[3]
        q_pos = past_len
        blk_end = lax.broadcasted_iota(jnp.int32, (R8, nc), 1) * CMP_STRIDE + (CMP_BLOCK - 1)
        cmp_mask = (blk_end <= q_pos) & (lax.broadcasted_iota(jnp.int32, (R8, nc), 1) < nc - 1)
        real_head = lax.broadcasted_iota(jnp.int32, (R8, nc), 0) < A_GROUP
        col8 = lax.broadcasted_iota(jnp.int32, (nb, R8), 1)
        jj = lax.broadcasted_iota(jnp.int32, (nb, R8), 0)
        cur = q_pos // SEL_BLOCK
        forced = (jj == 0) | (jj == cur) | (jj == cur - 1)
        k_idx = lax.broadcasted_iota(jnp.int32, (R8, keep), 1)
        rel = q_pos - (past_len - keep + k_idx)
        win_mask = (rel >= 0) & (rel < WINDOW)
        rank = None
        for g in range(A_KV_HEADS):
            gl = slice(g * d, (g + 1) * d)
            qg = q_ref[0, g] * A_SCALE
            s = jnp.where(cmp_mask, _dot_nt(qg, kc[:, gl]), NEG)
            e = jnp.exp(s - jnp.max(s, axis=1, keepdims=True))
            p = jnp.where(cmp_mask, e / jnp.sum(e, axis=1, keepdims=True), 0.0)
            ocmp_ref[0, g] = _dot(p, vc[:, gl])
            p_grp = jnp.sum(jnp.where(real_head, p, 0.0), axis=0, keepdims=True)
            score = _dot_f32_rhs_nt(cov_ref[...], jnp.broadcast_to(p_grp, (R8, nc)))
            rank_g = jnp.where(jj <= cur, jnp.where(forced, FORCE, score), NEG)
            rank = rank_g if rank is None else jnp.where(col8 == g, rank_g, rank)
            sw = jnp.where(win_mask, _dot(qg, wink_ref[0, 0, gl, :]), NEG)
            s_new = jnp.sum(qg * kwn_ref[0, :, gl], axis=1, keepdims=True)
            m = jnp.maximum(jnp.max(sw, axis=1, keepdims=True), s_new)
            ew = jnp.where(win_mask, jnp.exp(sw - m), 0.0)
            e_new = jnp.exp(s_new - m)
            owin_ref[0, g] = ((_dot_nt(ew, winv_ref[0, 0, gl, :]) + e_new * vwn_ref[0, :, gl])
                              / (jnp.sum(ew, axis=1, keepdims=True) + e_new))
        block = jj.astype(F32)
        out_row = lax.broadcasted_iota(jnp.int32, idx_ref.shape[1:], 0)
        picks = jnp.zeros(idx_ref.shape[1:], jnp.int32)
        for it in range(N_SEL - 1):
            m = jnp.max(rank, axis=0, keepdims=True)
            first = jnp.min(jnp.where(rank == m, block, float(nb)), axis=0, keepdims=True)
            picks = jnp.where(out_row == it, first.astype(jnp.int32), picks)
            rank = jnp.where(block == first, LOWEST, rank)
        idx_ref[0] = picks


def nsa_sample_cmp(page_table, pool_k, pool_v, q_hm, win_k, win_v, layer, kw_new, vw_new, cwk, cwv, past_len):
    B, n_pages = page_table.shape
    _, n_pool, W, page = pool_k.shape
    ng = GATHER_PAGES
    assert n_pages % ng == 0
    nc = n_pages * CHUNKS_PER_PAGE
    nb = past_len // SEL_BLOCK
    cov = jnp.asarray(_covers(nc, nb).T, BF16)
    keep = win_k.shape[3]
    page_spec = lambda i: pl.BlockSpec((1, 1, W, page), lambda b, c, pt: (layer, pt[b, c * ng + i], 0, 0))
    perb = lambda *s: pl.BlockSpec((1,) + s, lambda b, c, pt: (b,) + (0,) * len(s))
    win_spec = pl.BlockSpec((1, 1, W, keep), lambda b, c, pt: (layer, b, 0, 0))
    const = lambda a: pl.BlockSpec(a.shape, lambda b, c, pt: (0,) * a.ndim)
    consts = [*cwk, *cwv, cov]
    hw = cwk[0].shape[1]
    hm = jax.ShapeDtypeStruct((B, A_KV_HEADS, HEAD_ROWS, A_HEAD_DIM), F32)
    grid_spec = pltpu.PrefetchScalarGridSpec(
        num_scalar_prefetch=1, grid=(B, n_pages // ng),
        in_specs=[page_spec(i) for i in range(ng)] * 2
                 + [perb(A_KV_HEADS, HEAD_ROWS, A_HEAD_DIM), win_spec, win_spec, perb(1, W), perb(1, W)]
                 + [const(a) for a in consts],
        out_specs=[perb(A_KV_HEADS, HEAD_ROWS, A_HEAD_DIM), perb(A_KV_HEADS, HEAD_ROWS, A_HEAD_DIM),
                   perb(PICK_ROWS, HEAD_ROWS)],
        scratch_shapes=[pltpu.VMEM((nc, hw), F32), pltpu.VMEM((nc, hw), F32),
                        pltpu.VMEM((ng * page, W), F32), pltpu.VMEM((ng * page, W), F32)])
    return pl.pallas_call(
        functools.partial(_nsa_sample_cmp_kernel, past_len=past_len), grid_spec=grid_spec,
        out_shape=[hm, hm, jax.ShapeDtypeStruct((B, PICK_ROWS, HEAD_ROWS), jnp.int32)],
        compiler_params=_cparams("parallel", "arbitrary"), name="nsa_sample_cmp",
    )(page_table, *([pool_k] * ng), *([pool_v] * ng), q_hm, win_k, win_v, kw_new, vw_new, *consts)


SEL_PICKS_PER_STEP = 5


def _nsa_sample_sel_kernel(pt_ref, pk_ref, *refs, npk):
    del pt_ref
    P = SEL_PICKS_PER_STEP
    n_pages = A_KV_HEADS * P
    k_refs, v_refs = refs[:n_pages], refs[n_pages:2 * n_pages]
    q_ref, gate_ref, ksn_ref, vsn_ref, ocmp_ref, owin_ref, o_ref, m_scr, l_scr, acc_scr = refs[2 * n_pages:]
    b = pl.program_id(0)
    c = pl.program_id(1)
    d = A_HEAD_DIM
    page = k_refs[0].shape[3]
    tok_blk = lax.broadcasted_iota(jnp.int32, (HEAD_ROWS, page), 1) // SEL_BLOCK
    for g in range(A_KV_HEADS):
        gl = slice(g * d, (g + 1) * d)
        qg = q_ref[0, g] * A_SCALE

        @pl.when(c == 0)
        def _():
            m_scr[g] = jnp.sum(qg * ksn_ref[0, :, gl], axis=1, keepdims=True)
            l_scr[g] = jnp.ones((HEAD_ROWS, 1), F32)
            acc_scr[g] = jnp.broadcast_to(vsn_ref[0, :, gl], (HEAD_ROWS, d))

        chosen = [tok_blk == pk_ref[b, g * npk + c * P + i] % (page // SEL_BLOCK) for i in range(P)]
        s = [jnp.where(chosen[i], _dot(qg, k_refs[g * P + i][0, 0, gl, :]), NEG) for i in range(P)]
        m_old = m_scr[g]
        m = m_old
        for i in range(P):
            m = jnp.maximum(m, jnp.max(s[i], axis=1, keepdims=True))
        alpha = jnp.exp(m_old - m)
        l = alpha * l_scr[g]
        acc = alpha * acc_scr[g]
        for i in range(P):
            p = jnp.where(chosen[i], jnp.exp(s[i] - m), 0.0)
            l = l + jnp.sum(p, axis=1, keepdims=True)
            acc = acc + _dot_nt(p, v_refs[g * P + i][0, 0, gl, :])
        m_scr[g] = m
        l_scr[g] = l
        acc_scr[g] = acc

        @pl.when(c == pl.num_programs(1) - 1)
        def _():
            gate = _sigmoid(gate_ref[0, g])
            o_ref[0, g] = (gate[:, 0:1] * ocmp_ref[0, g] + gate[:, 1:2] * (acc / l)
                           + gate[:, 2:3] * owin_ref[0, g])


def nsa_sample_sel(page_table, picks, pool_k, pool_v, layer, q_hm, gate_hm, ks_new, vs_new, o_cmp, o_win):
    B = page_table.shape[0]
    _, n_pool, W, page = pool_k.shape
    per_page = page // SEL_BLOCK
    npk = picks.shape[2]
    P = SEL_PICKS_PER_STEP
    assert npk % P == 0
    picks = picks.reshape(B, A_KV_HEADS * npk)

    def blk(g, i):
        def index(b, c, pt, pk):
            return (layer, pt[b, pk[b, g * npk + c * P + i] // per_page], 0, 0)
        return pl.BlockSpec((1, 1, W, page), index)

    pages = [blk(g, i) for g in range(A_KV_HEADS) for i in range(P)]
    perb = lambda *s: pl.BlockSpec((1,) + s, lambda b, c, pt, pk: (b,) + (0,) * len(s))
    hm_spec = perb(A_KV_HEADS, HEAD_ROWS, A_HEAD_DIM)
    grid_spec = pltpu.PrefetchScalarGridSpec(
        num_scalar_prefetch=2, grid=(B, npk // P),
        in_specs=pages * 2 + [hm_spec, perb(A_KV_HEADS, HEAD_ROWS, 3), perb(1, W), perb(1, W), hm_spec, hm_spec],
        out_specs=hm_spec,
        scratch_shapes=[pltpu.VMEM((A_KV_HEADS, HEAD_ROWS, 1), F32), pltpu.VMEM((A_KV_HEADS, HEAD_ROWS, 1), F32),
                        pltpu.VMEM((A_KV_HEADS, HEAD_ROWS, A_HEAD_DIM), F32)])
    return pl.pallas_call(
        functools.partial(_nsa_sample_sel_kernel, npk=npk), grid_spec=grid_spec,
        out_shape=jax.ShapeDtypeStruct((B, A_KV_HEADS, HEAD_ROWS, A_HEAD_DIM), F32),
        compiler_params=_cparams("parallel", "arbitrary"), name="nsa_sample_sel",
    )(page_table, picks, *([pool_k] * len(pages)), *([pool_v] * len(pages)), q_hm, gate_hm, ks_new, vs_new,
      o_cmp, o_win)


def nsa_sample(a_q, a_gate, new_rows, pools, win_k, win_v, layer, page_table, cwk, cwv):
    B = a_q.shape[0]
    past_len = page_table.shape[1] * PAGE_SIZE
    _, _, ksn, vsn, kwn, vwn = new_rows
    pool_kc, pool_vc, pool_ks, pool_vs = pools
    padj = ((0, 0), (0, 0), (0, HEAD_ROWS - A_GROUP), (0, 0))
    q_hm = jnp.pad(a_q.reshape(B, A_KV_HEADS, A_GROUP, A_HEAD_DIM), padj)
    gate_hm = jnp.pad(jnp.transpose(a_gate.reshape(B, 3, A_KV_HEADS, A_GROUP), (0, 2, 3, 1)), padj)
    o_cmp, o_win, picks = nsa_sample_cmp(page_table, pool_kc, pool_vc, q_hm, win_k, win_v, layer, kwn, vwn,
                                         cwk, cwv, past_len)
    picks = jnp.swapaxes(picks, 1, 2)[:, :A_KV_HEADS, :N_SEL - 1]
    o = nsa_sample_sel(page_table, picks, pool_ks, pool_vs, layer, q_hm, gate_hm, ksn, vsn, o_cmp, o_win)
    return o[:, :, :A_GROUP].reshape(B, 1, A_WIDTH)


PROJ_GROUPS = (2 * M_WIDTH, M_WIDTH, M_WIDTH, 2 * M_HEADS, A_WIDTH) + (A_KV_LANES,) * 6 + (
    3 * A_HEADS, R_COLS, 3 * D_MODEL)
PROMPT_ROW_TILE = 256
MLSTM_CHUNK = 128
RWKV_CHUNK = 64


def _layer_params(l, w_in, w_branch_m, w_branch_a, w_branch_r, w_out, w_gate_up, w_down,
                  nsa_cmp_pe, nsa_cmp_w1, nsa_cmp_w2):
    offs = np.cumsum((0,) + PROJ_GROUPS)
    assert offs[-1] == w_in.shape[2]
    wi = w_in[l].astype(BF16)
    return {
        'proj': [wi[:, a:b] for a, b in zip(offs[:-1], offs[1:])],
        'wm': w_branch_m[l].astype(BF16), 'wa': w_branch_a[l].astype(BF16), 'wr': w_branch_r[l].astype(BF16),
        'wo': w_out[l].astype(BF16), 'wgu': w_gate_up[l].astype(BF16), 'wd': w_down[l].astype(BF16),
        'cwk': compress_weights(nsa_cmp_pe[l, 0], nsa_cmp_w1[l, 0], nsa_cmp_w2[l, 0]),
        'cwv': compress_weights(nsa_cmp_pe[l, 1], nsa_cmp_w1[l, 1], nsa_cmp_w2[l, 1]),
    }


def _trunk_layer(x, lw, lp, nsa_fn, m_state, r_state, tm, chunk, rwkv_chunk, gf, final_norm):
    B, T, D = x.shape
    x2d = x.reshape(B * T, D)
    outs = norm_proj(x2d, lp['norm1_g'], lw['proj'], tm)
    (m_qk, m_v, m_o, m_if, a_q), new_rows, (a_gate, r_cols, merge) = outs[:5], outs[5:11], outs[11:]
    b3 = lambda a: a.reshape(B, T, a.shape[-1])
    new_rows = [b3(r) for r in new_rows]
    y_m, conv_n, c_n, n_n, m_n = mlstm(b3(m_qk), b3(m_v), b3(m_o), b3(m_if), *m_state,
                                        lp['mlstm_conv_w'], lp['mlstm_conv_b'], lp['mlstm_if_bias'], chunk)
    y_a = nsa_fn(b3(a_q), b3(a_gate), new_rows)
    r_cols = b3(r_cols)
    y_r, s_n = rwkv(r_cols, r_state[0], r_state[1], lp, rwkv_chunk)
    x_new = merge_ffn(x2d, y_m.reshape(B * T, -1), y_a.reshape(B * T, -1), y_r.reshape(B * T, -1), merge,
                      lw['wm'], lw['wa'], lw['wr'], lw['wo'], lp['norm2_g'], lw['wgu'], lw['wd'], gf,
                      tm, final_norm)
    return x_new.reshape(B, T, D), new_rows, (conv_n, c_n, n_n, m_n), (r_cols[:, -1:], s_n)


def kernel(x_prompt, x_sample, cache_cmp_k, cache_cmp_v, cache_slc_k, cache_slc_v, cache_win_k, cache_win_v, state_mlstm_conv, state_mlstm_C, state_mlstm_n, state_mlstm_m, state_rwkv_shift, state_rwkv, page_table, norm1_g, w_in, mlstm_if_bias, mlstm_conv_w, mlstm_conv_b, nsa_cmp_pe, nsa_cmp_w1, nsa_cmp_w2, rwkv_mu, rwkv_w0, rwkv_w2, rwkv_a0, rwkv_a2, rwkv_g2, rwkv_k_k, rwkv_k_a, rwkv_r_k, rwkv_ln_g, rwkv_ln_b, w_branch_m, w_branch_a, w_branch_r, w_out, norm2_g, w_gate_up, w_down, final_norm_g):
    bp, tp, _ = x_prompt.shape
    bs, ts, _ = x_sample.shape
    assert ts == 1
    depth = w_in.shape[0]
    kvshape = lambda a: a.reshape(a.shape[0], a.shape[1], A_KV_HEADS, A_HEAD_DIM)
    xp, xs = x_prompt, x_sample
    pools = [channel_major(c) for c in (cache_cmp_k, cache_cmp_v, cache_slc_k, cache_slc_v)]
    win_k, win_v = channel_major(cache_win_k), channel_major(cache_win_v)
    states_p, states_s = [], []
    for l in range(depth):
        lw = _layer_params(l, w_in, w_branch_m, w_branch_a, w_branch_r, w_out, w_gate_up, w_down,
                           nsa_cmp_pe, nsa_cmp_w1, nsa_cmp_w2)
        lp = {'norm1_g': norm1_g[l], 'mlstm_if_bias': mlstm_if_bias[l], 'mlstm_conv_w': mlstm_conv_w[l],
              'mlstm_conv_b': mlstm_conv_b[l], 'rwkv_mu': rwkv_mu[l], 'rwkv_w0': rwkv_w0[l],
              'rwkv_w2': rwkv_w2[l], 'rwkv_a0': rwkv_a0[l], 'rwkv_a2': rwkv_a2[l], 'rwkv_g2': rwkv_g2[l],
              'rwkv_k_k': rwkv_k_k[l], 'rwkv_k_a': rwkv_k_a[l], 'rwkv_r_k': rwkv_r_k[l],
              'rwkv_ln_g': rwkv_ln_g[l], 'rwkv_ln_b': rwkv_ln_b[l], 'norm2_g': norm2_g[l]}
        last = l == depth - 1

        def nsa_p(a_q, a_gate, rows):
            kc = compress_rows(rows[0], lw['cwk'])
            vc = compress_rows(rows[1], lw['cwv'])
            return nsa_prompt_attend(a_q, a_gate, kc, vc, *rows[2:])

        m_zero = (jnp.zeros((bp, M_CONV - 1, 2 * M_WIDTH), F32),
                  jnp.zeros((bp, M_HEADS, M_HEAD_DIM, M_HEAD_DIM), F32),
                  jnp.zeros((bp, M_HEADS, M_HEAD_DIM), F32), jnp.zeros((bp, M_HEADS), F32))
        r_zero = (jnp.zeros((bp, 1, R_COLS), F32), jnp.zeros((bp, R_HEADS, R_HEAD_DIM, R_HEAD_DIM), F32))
        xp, rows, m_new, r_new = _trunk_layer(xp, lw, lp, nsa_p, m_zero, r_zero, PROMPT_ROW_TILE,
                                              MLSTM_CHUNK, RWKV_CHUNK, final_norm_g, last)
        keep = min(WINDOW, tp)
        states_p.append(tuple(kvshape(r) for r in rows[:4])
                        + (kvshape(rows[4][:, tp - keep:]), kvshape(rows[5][:, tp - keep:])) + m_new + r_new)

        def nsa_s(a_q, a_gate, rows, l=l):
            return nsa_sample(a_q, a_gate, rows, pools, win_k, win_v, l, page_table, lw['cwk'], lw['cwv'])

        xs, rows, m_new, r_new = _trunk_layer(
            xs, lw, lp, nsa_s,
            (state_mlstm_conv[l], state_mlstm_C[l], state_mlstm_n[l], state_mlstm_m[l]),
            (state_rwkv_shift[l], state_rwkv[l]), bs * ts, SUBLANES, SUBLANES, final_norm_g, last)
        keep = win_k.shape[3]

        def win_new(win, new_row):
            w = jnp.concatenate([win[l], jnp.swapaxes(new_row, 1, 2)], axis=2)[:, :, -keep:]
            return jnp.transpose(w.reshape(bs, A_KV_HEADS, A_HEAD_DIM, keep), (0, 3, 1, 2))

        states_s.append(tuple(kvshape(r) for r in rows[:4])
                        + (win_new(win_k, rows[4]), win_new(win_v, rows[5])) + m_new + r_new)

    stack = lambda states: [jnp.stack(a) for a in zip(*states)]
    return (xp, xs, *stack(states_p), *stack(states_s))
```

```python
import functools
import math

import numpy as np
import jax
import jax.numpy as jnp
from jax import lax
from jax.experimental import pallas as pl
from jax.experimental.pallas import tpu as pltpu

F32 = jnp.float32
BF16 = jnp.bfloat16

D_MODEL = 1024
DEPTH = 2
PAGE_SIZE = 128
M_HEADS = 4
M_HEAD_DIM = 128
M_WIDTH = M_HEADS * M_HEAD_DIM
M_CONV = 4
A_HEADS = 8
A_KV_HEADS = 2
A_HEAD_DIM = 64
A_WIDTH = A_HEADS * A_HEAD_DIM
A_GROUP = A_HEADS // A_KV_HEADS
A_SCALE = A_HEAD_DIM ** -0.5
A_KV_LANES = A_KV_HEADS * A_HEAD_DIM
CMP_STRIDE = 16
CMP_BLOCK = 2 * CMP_STRIDE
CMP_HIDDEN = A_HEAD_DIM
SEL_BLOCK = 64
N_SEL = 16
WINDOW = 512
Q_BLOCK = 128
R_HEADS = 8
R_HEAD_DIM = 64
R_WIDTH = R_HEADS * R_HEAD_DIM
R_DECAY_RANK = 64
R_A_RANK = 64
R_GATE_RANK = 128
R_COLS = 3 * R_WIDTH + R_DECAY_RANK + R_A_RANK + R_GATE_RANK
LN_X_EPS = 64e-5
D_FF = ((-(-8 * D_MODEL // 3)) + 255) // 256 * 256
IN_SPLITS = (2 * M_WIDTH, M_WIDTH, M_WIDTH, 2 * M_HEADS,
             A_WIDTH, 6 * A_KV_HEADS * A_HEAD_DIM, 3 * A_HEADS,
             R_COLS, 3 * D_MODEL)
RMS_EPS = 1e-6
NEG = -1e30
FORCE = 1e9
LOWEST = -3e38

VMEM_LIMIT_BYTES = 56 * 1024 * 1024
SUBLANES = 8
LANES = 128


def _cparams(*sem):
    return pltpu.CompilerParams(dimension_semantics=sem, vmem_limit_bytes=VMEM_LIMIT_BYTES)


def _resident(shape):
    nd = len(shape)
    return pl.BlockSpec(shape, lambda *_: (0,) * nd, pipeline_mode=pl.Buffered(1))


def _dot(a, b):
    return jnp.dot(a.astype(BF16), b.astype(BF16), preferred_element_type=F32)


def _dot_nt(a, b):
    return lax.dot_general(a.astype(BF16), b.astype(BF16), (((1,), (1,)), ((), ())),
                           preferred_element_type=F32)


def _split3(x):
    hi = x.astype(BF16)
    r = x - hi.astype(F32)
    mid = r.astype(BF16)
    lo = (r - mid.astype(F32)).astype(BF16)
    return hi, mid, lo


def _dot_f32_rhs(a_bf16, x):
    hi, mid, lo = _split3(x)
    d = lambda p: jnp.dot(a_bf16, p, preferred_element_type=F32)
    return d(hi) + d(mid) + d(lo)


def _dot_f32_rhs_nt(a_bf16, x):
    hi, mid, lo = _split3(x)
    d = lambda p: lax.dot_general(a_bf16, p, (((1,), (1,)), ((), ())), preferred_element_type=F32)
    return d(hi) + d(mid) + d(lo)


def _transpose_f32(x, eye_bf16):
    hi, mid, lo = _split3(x)
    d = lambda p: lax.dot_general(eye_bf16, p, (((1,), (1,)), ((), ())), preferred_element_type=F32)
    return d(hi) + d(mid) + d(lo)


def _eye(n, dtype=BF16):
    r = lax.broadcasted_iota(jnp.int32, (n, n), 0)
    c = lax.broadcasted_iota(jnp.int32, (n, n), 1)
    return (r == c).astype(dtype)


def _sigmoid(x):
    return 1.0 / (1.0 + jnp.exp(-x))


def _silu(x):
    return x * _sigmoid(x)


def _softplus(x):
    return jnp.maximum(x, 0.0) + jnp.log(1.0 + jnp.exp(-jnp.abs(x)))


def _rmsnorm(x, g):
    return x * lax.rsqrt(jnp.mean(x * x, axis=-1, keepdims=True) + RMS_EPS) * g


def _norm_proj_kernel(x_ref, g_ref, *refs):
    n = len(refs) // 2
    hb = _rmsnorm(x_ref[...], g_ref[...]).astype(BF16)
    for w_ref, o_ref in zip(refs[:n], refs[n:]):
        o_ref[...] = jnp.dot(hb, w_ref[...], preferred_element_type=F32)


def norm_proj(x2d, g, weights, tm):
    m, d = x2d.shape
    assert m % tm == 0
    row = lambda i: (i, 0)
    in_specs = [pl.BlockSpec((tm, d), row), _resident((1, d))]
    in_specs += [_resident(w.shape) for w in weights]
    out_specs = [pl.BlockSpec((tm, w.shape[1]), row) for w in weights]
    out_shape = [jax.ShapeDtypeStruct((m, w.shape[1]), F32) for w in weights]
    return pl.pallas_call(
        _norm_proj_kernel, grid=(m // tm,), in_specs=in_specs, out_specs=out_specs,
        out_shape=out_shape, compiler_params=_cparams("parallel"), name="norm_proj",
    )(x2d, g.reshape(1, d), *weights)


def _merge_ffn_kernel(x_ref, ym_ref, ya_ref, yr_ref, mg_ref, wm_ref, wa_ref, wr_ref, wo_ref,
                      g2_ref, wgu_ref, wd_ref, gf_ref, o_ref, *, final_norm):
    mg = mg_ref[...]
    merged = (_sigmoid(mg[:, :D_MODEL]) * _dot(ym_ref[...], wm_ref[...])
              + _sigmoid(mg[:, D_MODEL:2 * D_MODEL]) * _dot(ya_ref[...], wa_ref[...])
              + _sigmoid(mg[:, 2 * D_MODEL:]) * _dot(yr_ref[...], wr_ref[...]))
    x1 = x_ref[...] + _dot(merged, wo_ref[...])
    gu = _dot(_rmsnorm(x1, g2_ref[...]), wgu_ref[...])
    x2 = x1 + _dot(_silu(gu[:, :D_FF]) * gu[:, D_FF:], wd_ref[...])
    o_ref[...] = _rmsnorm(x2, gf_ref[...]) if final_norm else x2


def merge_ffn(x2d, ym, ya, yr, mg, wm, wa, wr, wo, g2, wgu, wd, gf, tm, final_norm):
    m, d = x2d.shape
    row = lambda i: (i, 0)
    acts = [x2d, ym, ya, yr, mg]
    consts = [wm, wa, wr, wo, g2.reshape(1, d), wgu, wd, gf.reshape(1, d)]
    in_specs = [pl.BlockSpec((tm, a.shape[1]), row) for a in acts]
    in_specs += [_resident(c.shape) for c in consts]
    return pl.pallas_call(
        functools.partial(_merge_ffn_kernel, final_norm=final_norm),
        grid=(m // tm,), in_specs=in_specs, out_specs=pl.BlockSpec((tm, d), row),
        out_shape=jax.ShapeDtypeStruct((m, d), F32), compiler_params=_cparams("parallel"),
        name="merge_ffn",
    )(*acts, *consts)


def _mlstm_kernel(qk_ref, v_ref, o_ref, if_ref, conv0_ref, c0_ref, n0_ref, m0_ref,
                  cw_ref, cb_ref, ifb_ref,
                  y_ref, conv_out_ref, c_out_ref, n_out_ref, m_out_ref,
                  cbuf, c_scr, n_scr, m_scr, *, chunk, n_last):
    c = pl.program_id(1)
    nc = pl.num_programs(1)
    L = chunk
    pad = SUBLANES

    @pl.when(c == 0)
    def _():
        cbuf[0:pad, :] = conv0_ref[0]
        c_scr[...] = c0_ref[0]
        n_scr[...] = n0_ref[0]
        m_scr[...] = m0_ref[0]

    cbuf[pad:pad + L, :] = qk_ref[0]
    acc = cb_ref[...]
    for j in range(M_CONV):
        acc = acc + cbuf[pad - (M_CONV - 1) + j: pad - (M_CONV - 1) + j + L, :] * cw_ref[j:j + 1, :]
    qk = _silu(acc)

    @pl.when(c == nc - 1)
    def _():
        conv_out_ref[0] = cbuf[n_last:n_last + pad, :]

    cbuf[0:pad, :] = cbuf[L:L + pad, :]

    gates = if_ref[0] + ifb_ref[...]
    lane8 = lax.broadcasted_iota(jnp.int32, (L, 2 * M_HEADS), 1)
    lf_all = -_softplus(-gates)
    lif = jnp.where(lane8 < M_HEADS, gates, lf_all)
    row_id = lax.broadcasted_iota(jnp.int32, (L, 2 * M_HEADS), 0)
    is_last_chunk = c == nc - 1
    if n_last < L:
        dead = jnp.logical_and(is_last_chunk, row_id >= n_last)
        lif = jnp.where(dead, jnp.where(lane8 < M_HEADS, NEG, 0.0), lif)
    t_i = lax.broadcasted_iota(jnp.int32, (L, L), 0)
    s_i = lax.broadcasted_iota(jnp.int32, (L, L), 1)
    causal = s_i <= t_i
    tri = causal.astype(BF16)
    csum = _dot_f32_rhs(tri, lif)
    lib = jnp.where(lane8 < M_HEADS, lif, csum)
    lib_t = _transpose_f32(lib, _eye(2 * M_HEADS))
    eye_h = _eye(M_HEAD_DIM)
    last = n_last - 1

    heads = range(M_HEADS)
    sl = [slice(h * M_HEAD_DIM, (h + 1) * M_HEAD_DIM) for h in heads]
    q = [qk[:, sl[h]] for h in heads]
    k = [qk[:, M_WIDTH + h * M_HEAD_DIM: M_WIDTH + (h + 1) * M_HEAD_DIM] * (M_HEAD_DIM ** -0.5) for h in heads]
    v = [v_ref[0, :, sl[h]] for h in heads]
    li_col = [lib[:, h:h + 1] for h in heads]
    b_col = [lib[:, M_HEADS + h:M_HEADS + h + 1] for h in heads]
    m_prev = [m_scr[0:1, h:h + 1] for h in heads]
    c_old = [c_scr[h] for h in heads]
    n_old = [n_scr[h:h + 1, :] for h in heads]
    s_qk = [_dot_nt(q[h], k[h]) for h in heads]
    q_c = [_dot(q[h], c_old[h]) for h in heads]
    dmat = [jnp.where(causal, b_col[h] - lib_t[M_HEADS + h:M_HEADS + h + 1, :] + lib_t[h:h + 1, :], NEG)
            for h in heads]
    inter = [b_col[h] + m_prev[h] for h in heads]
    m_t = [jnp.maximum(inter[h], jnp.max(dmat[h], axis=1, keepdims=True)) for h in heads]
    att = [jnp.exp(dmat[h] - m_t[h]) * s_qk[h] for h in heads]
    w_inter = [jnp.exp(inter[h] - m_t[h]) for h in heads]
    num = [_dot(att[h], v[h]) + w_inter[h] * q_c[h] for h in heads]
    den = [jnp.sum(att[h], axis=1, keepdims=True) + w_inter[h] * jnp.sum(q[h] * n_old[h], axis=1, keepdims=True)
           for h in heads]
    for h in heads:
        hh = num[h] / jnp.maximum(jnp.abs(den[h]), jnp.exp(-m_t[h]))
        y_ref[0, :, sl[h]] = _sigmoid(o_ref[0, :, sl[h]]) * hh

    m_new = [m_t[h][last:last + 1, :] for h in heads]
    b_last = [b_col[h][last:last + 1, :] for h in heads]
    scale = [jnp.exp(b_last[h] + m_prev[h] - m_new[h]) for h in heads]
    kw = [k[h] * jnp.exp(b_last[h] - b_col[h] + li_col[h] - m_new[h]) for h in heads]
    kw_t = [lax.dot_general(eye_h, kw[h].astype(BF16), DOT_NT, preferred_element_type=F32)
            for h in heads]
    for h in heads:
        c_scr[h] = scale[h] * c_old[h] + _dot(kw_t[h], v[h])
        n_scr[h:h + 1, :] = scale[h] * n_old[h] + jnp.sum(kw[h], axis=0, keepdims=True)
        m_scr[0:1, h:h + 1] = m_new[h]

    @pl.when(is_last_chunk)
    def _():
        c_out_ref[0] = c_scr[...]
        n_out_ref[0] = n_scr[...]
        m_out_ref[0] = m_scr[...]


def mlstm(m_qk, m_v, m_o, m_if, conv0, c0, n0, m0, conv_w, conv_b, if_bias, chunk):
    B, T, _ = m_qk.shape
    nc = -(-T // chunk)
    tp = nc * chunk
    n_last = T - (nc - 1) * chunk
    if tp != T:
        padt = lambda a: jnp.pad(a, ((0, 0), (0, tp - T), (0, 0)))
        m_qk, m_v, m_o, m_if = padt(m_qk), padt(m_v), padt(m_o), padt(m_if)
    conv0p = jnp.pad(conv0, ((0, 0), (SUBLANES - (M_CONV - 1), 0), (0, 0)))
    n0p = jnp.pad(n0, ((0, 0), (0, SUBLANES - M_HEADS), (0, 0)))
    m0p = jnp.pad(m0, ((0, 0), (0, LANES - M_HEADS))).reshape(B, 1, LANES)
    cw = jnp.pad(conv_w, ((0, SUBLANES - M_CONV), (0, 0)))
    tile = lambda w: pl.BlockSpec((1, chunk, w), lambda b, c: (b, c, 0))
    perb = lambda *s: pl.BlockSpec((1,) + s, lambda b, c: (b,) + (0,) * len(s))
    outs = pl.pallas_call(
        functools.partial(_mlstm_kernel, chunk=chunk, n_last=n_last),
        grid=(B, nc),
        in_specs=[tile(2 * M_WIDTH), tile(M_WIDTH), tile(M_WIDTH), tile(2 * M_HEADS),
                  perb(SUBLANES, 2 * M_WIDTH), perb(M_HEADS, M_HEAD_DIM, M_HEAD_DIM),
                  perb(SUBLANES, M_HEAD_DIM), perb(1, LANES),
                  _resident((SUBLANES, 2 * M_WIDTH)), _resident((1, 2 * M_WIDTH)),
                  _resident((1, 2 * M_HEADS))],
        out_specs=[tile(M_WIDTH), perb(SUBLANES, 2 * M_WIDTH),
                   perb(M_HEADS, M_HEAD_DIM, M_HEAD_DIM), perb(SUBLANES, M_HEAD_DIM), perb(1, LANES)],
        out_shape=[jax.ShapeDtypeStruct((B, tp, M_WIDTH), F32),
                   jax.ShapeDtypeStruct((B, SUBLANES, 2 * M_WIDTH), F32),
                   jax.ShapeDtypeStruct((B, M_HEADS, M_HEAD_DIM, M_HEAD_DIM), F32),
                   jax.ShapeDtypeStruct((B, SUBLANES, M_HEAD_DIM), F32),
                   jax.ShapeDtypeStruct((B, 1, LANES), F32)],
        scratch_shapes=[pltpu.VMEM((chunk + SUBLANES, 2 * M_WIDTH), F32),
                        pltpu.VMEM((M_HEADS, M_HEAD_DIM, M_HEAD_DIM), F32),
                        pltpu.VMEM((SUBLANES, M_HEAD_DIM), F32),
                        pltpu.VMEM((1, LANES), F32)],
        compiler_params=_cparams("parallel", "arbitrary"), name="mlstm",
    )(m_qk, m_v, m_o, m_if, conv0p, c0, n0p, m0p, cw, conv_b.reshape(1, -1), if_bias.reshape(1, -1))
    y, conv_o, c_o, n_o, m_o_ = outs
    return (y[:, :T], conv_o[:, SUBLANES - (M_CONV - 1):], c_o, n_o[:, :M_HEADS], m_o_[:, 0, :M_HEADS])


def _rwkv_project(cols, prev, mu_ref, w0_ref, w2_ref, a0_ref, a2_ref, g2_ref, kk_ref, ka_ref, rk_ref):
    mixed = cols + (prev - cols) * mu_ref[...]
    W = R_WIDTH
    r = mixed[:, :W]
    k = mixed[:, W:2 * W]
    v = mixed[:, 2 * W:3 * W]
    o = 3 * W
    wd = mixed[:, o:o + R_DECAY_RANK]
    ad = mixed[:, o + R_DECAY_RANK:o + R_DECAY_RANK + R_A_RANK]
    gd = mixed[:, o + R_DECAY_RANK + R_A_RANK:]
    w = -_softplus(-(w0_ref[...] + _dot(jnp.tanh(wd), w2_ref[...]))) - 0.5
    a_gate = _sigmoid(a0_ref[...] + _dot(ad, a2_ref[...]))
    g = _dot(_sigmoid(gd), g2_ref[...])
    kk = k * kk_ref[...]
    k2 = k * (1.0 + (a_gate - 1.0) * ka_ref[...])
    rkk = r * k2 * rk_ref[...]
    a_h, b_h, bonus_h = [], [], []
    for h in range(R_HEADS):
        sl = slice(h * R_HEAD_DIM, (h + 1) * R_HEAD_DIM)
        kkh = kk[:, sl]
        kkh = kkh * lax.rsqrt(jnp.maximum(jnp.sum(kkh * kkh, axis=1, keepdims=True), 1e-24))
        a_h.append(-kkh)
        b_h.append(kkh * a_gate[:, sl])
        bonus_h.append(jnp.sum(rkk[:, sl], axis=1, keepdims=True) * v[:, sl])
    return r, -jnp.exp(w), k2, v, g, a_h, b_h, bonus_h


RWKV_SEQS_PER_STEP = 4
DOT_NN = (((1,), (0,)), ((), ()))
DOT_NT = (((1,), (1,)), ((), ()))
DOT_TN = (((0,), (0,)), ((), ()))


def _dot_x3(x, y, dims):
    xh = x.astype(BF16)
    xl = (x - xh.astype(F32)).astype(BF16)
    yh = y.astype(BF16)
    yl = (y - yh.astype(F32)).astype(BF16)
    d = lambda p, q: lax.dot_general(p, q, dims, preferred_element_type=F32)
    return d(xh, yh) + d(xh, yl) + d(xl, yh)


def _dot_tn(a, b):
    return lax.dot_general(a.astype(BF16), b.astype(BF16), DOT_TN, preferred_element_type=F32)


def _rwkv_chunk_kernel(cols_ref, shift0_ref, mu_ref, w0_ref, w2_ref, a0_ref, a2_ref, g2_ref, kk_ref, ka_ref,
                       rk_ref, lng_ref, lnb_ref, s0_ref, o_ref, s_out_ref, s_scr, carry, *, chunk, n_valid):
    t = pl.program_id(1)
    nt = pl.num_programs(1)
    L, N = chunk, R_HEAD_DIM

    @pl.when(t == 0)
    def _():
        s_scr[...] = s0_ref[...]
        carry[...] = shift0_ref[...]

    ti = lax.broadcasted_iota(jnp.int32, (L, L), 0)
    si = lax.broadcasted_iota(jnp.int32, (L, L), 1)
    incl = si <= ti
    strict = si < ti
    eye = (si == ti).astype(F32)
    tri = incl.astype(BF16)
    first_row = lax.broadcasted_iota(jnp.int32, (L, R_COLS), 0) == 0
    n_seq = cols_ref.shape[0]
    ah, rh, bt, kt, vh, bb, kb, gm, gate, bonus = [], [], [], [], [], [], [], [], [], []
    for i in range(n_seq):
        cols = cols_ref[i]
        prev = jnp.where(first_row, carry[i], pltpu.roll(cols, 1, 0))
        carry[i] = cols[L - 1:L, :]
        r, lw, k, v, g, a_h, b_h, bonus_h = _rwkv_project(cols, prev, mu_ref, w0_ref, w2_ref, a0_ref, a2_ref,
                                                          g2_ref, kk_ref, ka_ref, rk_ref)
        bonus += bonus_h
        if n_valid % L:
            live = (t * L + lax.broadcasted_iota(jnp.int32, (L, R_WIDTH), 0)) < n_valid
            lw, k, v = (jnp.where(live, z, 0.0) for z in (lw, k, v))
            a_h = [jnp.where(live[:, :N], z, 0.0) for z in a_h]
            b_h = [jnp.where(live[:, :N], z, 0.0) for z in b_h]
        c = _dot_f32_rhs(tri, lw)
        c_end = c[L - 1:L, :]
        e_prev = jnp.exp(c - lw)
        r_hat = r * jnp.exp(c)
        e_neg = jnp.exp(-c)
        e_end = jnp.exp(c_end - c)
        gam = jnp.exp(c_end)
        for h in range(R_HEADS):
            hs = slice(h * N, (h + 1) * N)
            ah.append(a_h[h] * e_prev[:, hs])
            rh.append(r_hat[:, hs])
            bt.append(b_h[h] * e_neg[:, hs])
            kt.append(k[:, hs] * e_neg[:, hs])
            vh.append(v[:, hs])
            bb.append(b_h[h] * e_end[:, hs])
            kb.append(k[:, hs] * e_end[:, hs])
            gm.append(gam[:, hs])
            gate.append(g[:, hs])

    heads = range(n_seq * R_HEADS)
    unit = [(i, h, slice(h * N, (h + 1) * N)) for i in range(n_seq) for h in range(R_HEADS)]
    a_ab = [jnp.where(strict, _dot_x3(ah[h], bt[h], DOT_NT), 0.0) for h in heads]
    a_ak = [jnp.where(strict, _dot_nt(ah[h], kt[h]), 0.0) for h in heads]
    a_rb = [jnp.where(incl, _dot_nt(rh[h], bt[h]), 0.0) for h in heads]
    a_rk = [jnp.where(incl, _dot_nt(rh[h], kt[h]), 0.0) for h in heads]
    inv = [eye + a_ab[h] for h in heads]
    pw = a_ab
    span = 2
    while span < L:
        pw = [_dot(pw[h], pw[h]) for h in heads]
        inv = [inv[h] + _dot(inv[h], pw[h]) for h in heads]
        span *= 2
    s = [s_scr[i, h] for i, h, _ in unit]
    rhs = [_dot_nt(ah[h], s[h]) + _dot(a_ak[h], vh[h]) for h in heads]
    u = [_dot(inv[h], rhs[h]) for h in heads]
    y = [_dot_nt(rh[h], s[h]) + _dot(a_rb[h], u[h]) + _dot(a_rk[h], vh[h]) for h in heads]
    for n, (i, h, hs) in enumerate(unit):
        s_scr[i, h] = s[n] * gm[n] + _dot_tn(u[n], bb[n]) + _dot_tn(vh[n], kb[n])
    for n, (i, h, hs) in enumerate(unit):
        mean = jnp.mean(y[n], axis=1, keepdims=True)
        yc = y[n] - mean
        var = jnp.mean(yc * yc, axis=1, keepdims=True)
        yn = yc * lax.rsqrt(var + LN_X_EPS) * lng_ref[:, hs] + lnb_ref[:, hs]
        o_ref[i, :, hs] = (yn + bonus[n]) * gate[n]

    @pl.when(t == nt - 1)
    def _():
        s_out_ref[...] = s_scr[...]


def rwkv(cols, shift0, s0, p, chunk):
    B, T, _ = cols.shape
    tp = -(-T // chunk) * chunk
    assert chunk % SUBLANES == 0 and chunk & (chunk - 1) == 0
    colsp = jnp.pad(cols, ((0, 0), (0, tp - T), (0, 0))) if tp != T else cols
    vec = lambda a: a.reshape(1, -1)
    consts = [vec(p['rwkv_mu']), vec(p['rwkv_w0']), p['rwkv_w2'].astype(BF16), vec(p['rwkv_a0']),
              p['rwkv_a2'].astype(BF16), p['rwkv_g2'].astype(BF16), vec(p['rwkv_k_k']),
              vec(p['rwkv_k_a']), vec(p['rwkv_r_k']), vec(p['rwkv_ln_g']), vec(p['rwkv_ln_b'])]
    ns = RWKV_SEQS_PER_STEP
    assert B % ns == 0
    seq_spec = lambda w: pl.BlockSpec((ns, chunk, w), lambda b, t: (b, t, 0))
    st_spec = pl.BlockSpec((ns, R_HEADS, R_HEAD_DIM, R_HEAD_DIM), lambda b, t: (b, 0, 0, 0))
    out, s_new = pl.pallas_call(
        functools.partial(_rwkv_chunk_kernel, chunk=chunk, n_valid=T),
        grid=(B // ns, tp // chunk),
        in_specs=[seq_spec(R_COLS), pl.BlockSpec((ns, 1, R_COLS), lambda b, t: (b, 0, 0))]
                 + [_resident(c.shape) for c in consts] + [st_spec],
        out_specs=[seq_spec(R_WIDTH), st_spec],
        out_shape=[jax.ShapeDtypeStruct((B, tp, R_WIDTH), F32), jax.ShapeDtypeStruct(s0.shape, F32)],
        scratch_shapes=[pltpu.VMEM((ns, R_HEADS, R_HEAD_DIM, R_HEAD_DIM), F32),
                        pltpu.VMEM((ns, 1, R_COLS), F32)],
        compiler_params=_cparams("parallel", "arbitrary"), name="rwkv_chunk",
    )(colsp, shift0, *consts, s0)
    return out[:, :T], s_new


def compress_weights(pe, w1, w2):
    d, hdim, G = A_HEAD_DIM, CMP_HIDDEN, A_KV_HEADS
    w1r = w1.reshape(2, CMP_STRIDE, d, hdim)
    same_head = jnp.eye(G, dtype=F32)
    wl = (jnp.transpose(w1r, (1, 2, 0, 3))[:, None, :, :, None, :]
          * same_head[None, :, None, None, :, None])
    wl = wl.reshape(CMP_STRIDE * G * d, 2 * G * hdim).astype(BF16)
    wpe = jnp.transpose(w1r, (1, 0, 2, 3)).reshape(CMP_STRIDE, 2 * d, hdim)
    wpe = jnp.concatenate([wpe] * G, axis=-1).astype(BF16)
    pe_l = jnp.transpose(pe.reshape(2, CMP_STRIDE, d), (1, 0, 2)).reshape(CMP_STRIDE, 1, 2 * d)
    pe_l = jnp.broadcast_to(pe_l, (CMP_STRIDE, SUBLANES, 2 * d))
    w2bd = w2[None, :, None, :] * same_head[:, None, :, None]
    return wl, wpe, pe_l, w2bd.reshape(G * hdim, G * d).astype(BF16)


def _compress_halves(strided_rows, wl_ref, n):
    x = jnp.concatenate([strided_rows(l).astype(BF16) for l in range(CMP_STRIDE)], axis=1)
    return jnp.dot(x, wl_ref[...], preferred_element_type=F32)


def _compress_finish(halves, wpe_ref, pe_ref, w2_ref):
    n = halves.shape[0]
    gh = A_KV_HEADS * CMP_HIDDEN
    pe_term = None
    for l in range(CMP_STRIDE):
        part = _dot(pe_ref[l], wpe_ref[l])
        pe_term = part if pe_term is None else pe_term + part
    second = pltpu.roll(halves[:, gh:], n - 1, 0)
    hid = _silu(halves[:, :gh] + second + pe_term[0:1, :])
    return _dot(hid, w2_ref[...])


def _compress_kernel(x_ref, wl_ref, wpe_ref, pe_ref, w2_ref, o_ref):
    n = o_ref.shape[1]
    rows = lambda l: x_ref[0, pl.ds(l, n, stride=CMP_STRIDE), :]
    o_ref[0] = _compress_finish(_compress_halves(rows, wl_ref, n), wpe_ref, pe_ref, w2_ref)


def compress_rows(rows, cw):
    B, T, W = rows.shape
    n = T // CMP_STRIDE
    return pl.pallas_call(
        _compress_kernel, grid=(B,),
        in_specs=[pl.BlockSpec((1, T, W), lambda b: (b, 0, 0))] + [_resident(c.shape) for c in cw],
        out_specs=pl.BlockSpec((1, n, W), lambda b: (b, 0, 0)),
        out_shape=jax.ShapeDtypeStruct((B, n, W), F32),
        compiler_params=_cparams("parallel"), name="nsa_compress",
    )(rows, *cw)


def _topk_mask_columns(rank, k):
    n_entries = rank.shape[0]
    row = lax.broadcasted_iota(jnp.int32, rank.shape, 0).astype(F32)
    sel = jnp.zeros(rank.shape, F32)
    for _ in range(k):
        m = jnp.max(rank, axis=0, keepdims=True)
        first = jnp.min(jnp.where(rank == m, row, float(n_entries)), axis=0, keepdims=True)
        hit = row == first
        sel = jnp.where(hit, 1.0, sel)
        rank = jnp.where(hit, LOWEST, rank)
    return sel


def _flash_update(state, s, bias, v):
    m_old, l_old, acc = state
    reps = s.shape[0] // bias.shape[0]
    n = bias.shape[0]
    s = jnp.concatenate([s[j * n:(j + 1) * n] + bias for j in range(reps)], axis=0)
    m_new = jnp.maximum(m_old, jnp.max(s, axis=1, keepdims=True))
    p = jnp.exp(s - m_new)
    alpha = jnp.exp(m_old - m_new)
    return (m_new, alpha * l_old + jnp.sum(p, axis=1, keepdims=True), alpha * acc + _dot(p, v))


def _nsa_prompt_kernel(q_ref, gate_ref, kc_ref, vc_ref, ks_ref, vs_ref, kw_ref, vw_ref, covt_ref,
                       o_ref, *, seq, tk, wk):
    i = pl.program_id(1)
    QB, d, J = Q_BLOCK, A_HEAD_DIM, A_GROUP
    nc = kc_ref.shape[1]
    nb = covt_ref.shape[0]
    st = i * QB
    q = q_ref[0] * A_SCALE
    gate = _sigmoid(gate_ref[0])
    rows = J * QB
    t_q = st + lax.broadcasted_iota(jnp.int32, (QB, 1), 0)
    t_rows = jnp.concatenate([t_q] * J, axis=0)
    blk_end = lax.broadcasted_iota(jnp.int32, (rows, nc), 1) * CMP_STRIDE + (CMP_BLOCK - 1)
    cmp_mask = blk_end <= t_rows
    jj = lax.broadcasted_iota(jnp.int32, (nb, QB), 0)
    cur = (st + lax.broadcasted_iota(jnp.int32, (1, QB), 1)) // SEL_BLOCK
    forced = (jj == 0) | (jj == cur) | (jj == cur - 1)
    zeros = lambda w: jnp.zeros((rows, w), F32)

    qgs, o_cmps, ranks = [], [], []
    for g in range(A_KV_HEADS):
        gl = slice(g * d, (g + 1) * d)
        qg = jnp.concatenate([q[:, (g * J + j) * d:(g * J + j + 1) * d] for j in range(J)],
                             axis=0).astype(BF16)
        s = jnp.where(cmp_mask, _dot_nt(qg, kc_ref[0, :, gl]), NEG)
        e = jnp.exp(s - jnp.max(s, axis=1, keepdims=True))
        p = jnp.where(cmp_mask, e / jnp.sum(e, axis=1, keepdims=True), 0.0)
        p_grp = p[0:QB]
        for j in range(1, J):
            p_grp = p_grp + p[j * QB:(j + 1) * QB]
        score = _dot_f32_rhs_nt(covt_ref[...], p_grp)
        qgs.append(qg)
        o_cmps.append(_dot(p, vc_ref[0, :, gl]))
        ranks.append(jnp.where(jj <= cur, jnp.where(forced, FORCE, score), NEG))
    sel_all = _topk_mask_columns(jnp.concatenate(ranks, axis=1), min(N_SEL, nb)).astype(BF16)

    k0w = pl.multiple_of(jnp.clip(st + QB - wk, 0, seq - wk), QB)
    rel = t_q - (k0w + lax.broadcasted_iota(jnp.int32, (1, wk), 1))
    win_bias = jnp.where((rel >= 0) & (rel < WINDOW), 0.0, NEG)

    def sel_tile(kt, states):
        k0 = pl.multiple_of(kt * tk, tk)
        s_pos = k0 + lax.broadcasted_iota(jnp.int32, (1, tk), 1)
        blk_of = k0 // SEL_BLOCK + lax.broadcasted_iota(jnp.int32, (nb, tk), 1) // SEL_BLOCK
        expand = (lax.broadcasted_iota(jnp.int32, (nb, tk), 0) == blk_of).astype(BF16)
        causal = s_pos <= t_q
        new_states = []
        for g in range(A_KV_HEADS):
            gl = slice(g * d, (g + 1) * d)
            chosen = lax.dot_general(sel_all[:, g * QB:(g + 1) * QB], expand, DOT_TN,
                                     preferred_element_type=F32) > 0.5
            bias = jnp.where(chosen & causal, 0.0, NEG)
            sc = _dot_nt(qgs[g], ks_ref[0, pl.ds(k0, tk), gl])
            new_states.append(_flash_update(states[g], sc, bias, vs_ref[0, pl.ds(k0, tk), gl]))
        return tuple(new_states)

    init = (jnp.full((rows, 1), NEG, F32), jnp.zeros((rows, 1), F32), zeros(d))
    sel_states = lax.fori_loop(0, (st + QB - 1) // tk + 1, sel_tile, (init,) * A_KV_HEADS)

    for g in range(A_KV_HEADS):
        gl = slice(g * d, (g + 1) * d)
        qg, o_cmp = qgs[g], o_cmps[g]
        _, l_sel, acc_sel = sel_states[g]
        o_sel = acc_sel / l_sel

        sw = _dot_nt(qg, kw_ref[0, pl.ds(k0w, wk), gl])
        sw = jnp.concatenate([sw[j * QB:(j + 1) * QB] + win_bias for j in range(J)], axis=0)
        pw = jnp.exp(sw - jnp.max(sw, axis=1, keepdims=True))
        o_win = _dot(pw, vw_ref[0, pl.ds(k0w, wk), gl]) / jnp.sum(pw, axis=1, keepdims=True)

        for j in range(J):
            h = g * J + j
            rs = slice(j * QB, (j + 1) * QB)
            o_ref[0, :, h * d:(h + 1) * d] = (gate[:, h:h + 1] * o_cmp[rs]
                                              + gate[:, A_HEADS + h:A_HEADS + h + 1] * o_sel[rs]
                                              + gate[:, 2 * A_HEADS + h:2 * A_HEADS + h + 1] * o_win[rs])


def _covers(n_cmp, n_blocks):
    i = np.arange(n_cmp)[:, None]
    j = np.arange(n_blocks)[None, :]
    return (i * CMP_STRIDE < (j + 1) * SEL_BLOCK) & (i * CMP_STRIDE + CMP_BLOCK > j * SEL_BLOCK)


def nsa_prompt_attend(a_q, a_gate, kc, vc, ks, vs, kw, vw):
    B, T, _ = a_q.shape
    nc = kc.shape[1]
    nb = -(-T // SEL_BLOCK)
    tk = min(4 * Q_BLOCK, T)
    wk = min(WINDOW + Q_BLOCK, T)
    cov = jnp.asarray(_covers(nc, nb).T, BF16)
    tile = lambda w: pl.BlockSpec((1, Q_BLOCK, w), lambda b, i: (b, i, 0))
    full = lambda n, w: pl.BlockSpec((1, n, w), lambda b, i: (b, 0, 0))
    W = A_KV_LANES
    return pl.pallas_call(
        functools.partial(_nsa_prompt_kernel, seq=T, tk=tk, wk=wk),
        grid=(B, T // Q_BLOCK),
        in_specs=[tile(A_WIDTH), tile(3 * A_HEADS), full(nc, W), full(nc, W),
                  full(T, W), full(T, W), full(T, W), full(T, W), _resident(cov.shape)],
        out_specs=tile(A_WIDTH),
        out_shape=jax.ShapeDtypeStruct((B, T, A_WIDTH), F32),
        compiler_params=_cparams("parallel", "arbitrary"), name="nsa_prompt",
    )(a_q, a_gate, kc, vc, ks, vs, kw, vw, cov)


GATHER_PAGES = 16
CHUNKS_PER_PAGE = PAGE_SIZE // CMP_STRIDE
HEAD_ROWS = SUBLANES
PICK_ROWS = 2 * SUBLANES


def channel_major(cache):
    L, n, tok, G, d = cache.shape
    return jnp.transpose(cache, (0, 1, 3, 4, 2)).reshape(L, n, G * d, tok)


def _nsa_sample_cmp_kernel(pt_ref, *refs, past_len):
    del pt_ref
    ng = GATHER_PAGES
    pk_refs, pv_refs = refs[:ng], refs[ng:2 * ng]
    (q_ref, wink_ref, winv_ref, kwn_ref, vwn_ref, wlk_ref, wpek_ref, pek_ref, w2k_ref,
     wlv_ref, wpev_ref, pev_ref, w2v_ref, cov_ref, ocmp_ref, owin_ref, idx_ref,
     hk_scr, hv_scr, xk_scr, xv_scr) = refs[2 * ng:]
    c = pl.program_id(1)
    page = pk_refs[0].shape[3]
    rows_per_step = ng * CHUNKS_PER_PAGE
    base = pl.multiple_of(c * rows_per_step, rows_per_step)
    for page_refs, xs, wl_ref, h_scr in ((pk_refs, xk_scr, wlk_ref, hk_scr), (pv_refs, xv_scr, wlv_ref, hv_scr)):
        for i in range(ng):
            xs[i * page:(i + 1) * page, :] = page_refs[i][0, 0].T
        rows = lambda l, xs=xs: xs[pl.ds(l, rows_per_step, stride=CMP_STRIDE), :]
        h_scr[pl.ds(base, rows_per_step), :] = _compress_halves(rows, wl_ref, rows_per_step)

    @pl.when(c == pl.num_programs(1) - 1)
    def _():
        d, R8 = A_HEAD_DIM, HEAD_ROWS
        kc = _compress_finish(hk_scr[...], wpek_ref, pek_ref, w2k_ref)
        vc = _compress_finish(hv_scr[...], wpev_ref, pev_ref, w2v_ref)
        nc = kc.shape[0]
        nb = cov_ref.shape[0]
        keep = wink_ref.shape[3]
        q_pos = past_len
        blk_end = lax.broadcasted_iota(jnp.int32, (R8, nc), 1) * CMP_STRIDE + (CMP_BLOCK - 1)
        cmp_mask = (blk_end <= q_pos) & (lax.broadcasted_iota(jnp.int32, (R8, nc), 1) < nc - 1)
        real_head = lax.broadcasted_iota(jnp.int32, (R8, nc), 0) < A_GROUP
        col8 = lax.broadcasted_iota(jnp.int32, (nb, R8), 1)
        jj = lax.broadcasted_iota(jnp.int32, (nb, R8), 0)
        cur = q_pos // SEL_BLOCK
        forced = (jj == 0) | (jj == cur) | (jj == cur - 1)
        k_idx = lax.broadcasted_iota(jnp.int32, (R8, keep), 1)
        rel = q_pos - (past_len - keep + k_idx)
        win_mask = (rel >= 0) & (rel < WINDOW)
        rank = None
        for g in range(A_KV_HEADS):
            gl = slice(g * d, (g + 1) * d)
            qg = q_ref[0, g] * A_SCALE
            s = jnp.where(cmp_mask, _dot_nt(qg, kc[:, gl]), NEG)
            e = jnp.exp(s - jnp.max(s, axis=1, keepdims=True))
            p = jnp.where(cmp_mask, e / jnp.sum(e, axis=1, keepdims=True), 0.0)
            ocmp_ref[0, g] = _dot(p, vc[:, gl])
            p_grp = jnp.sum(jnp.where(real_head, p, 0.0), axis=0, keepdims=True)
            score = _dot_f32_rhs_nt(cov_ref[...], jnp.broadcast_to(p_grp, (R8, nc)))
            rank_g = jnp.where(jj <= cur, jnp.where(forced, FORCE, score), NEG)
            rank = rank_g if rank is None else jnp.where(col8 == g, rank_g, rank)
            sw = jnp.where(win_mask, _dot(qg, wink_ref[0, 0, gl, :]), NEG)
            s_new = jnp.sum(qg * kwn_ref[0, :, gl], axis=1, keepdims=True)
            m = jnp.maximum(jnp.max(sw, axis=1, keepdims=True), s_new)
            ew = jnp.where(win_mask, jnp.exp(sw - m), 0.0)
            e_new = jnp.exp(s_new - m)
            owin_ref[0, g] = ((_dot_nt(ew, winv_ref[0, 0, gl, :]) + e_new * vwn_ref[0, :, gl])
                              / (jnp.sum(ew, axis=1, keepdims=True) + e_new))
        block = jj.astype(F32)
        out_row = lax.broadcasted_iota(jnp.int32, idx_ref.shape[1:], 0)
        picks = jnp.zeros(idx_ref.shape[1:], jnp.int32)
        for it in range(N_SEL - 1):
            m = jnp.max(rank, axis=0, keepdims=True)
            first = jnp.min(jnp.where(rank == m, block, float(nb)), axis=0, keepdims=True)
            picks = jnp.where(out_row == it, first.astype(jnp.int32), picks)
            rank = jnp.where(block == first, LOWEST, rank)
        idx_ref[0] = picks


def nsa_sample_cmp(page_table, pool_k, pool_v, q_hm, win_k, win_v, layer, kw_new, vw_new, cwk, cwv, past_len):
    B, n_pages = page_table.shape
    _, n_pool, W, page = pool_k.shape
    ng = GATHER_PAGES
    assert n_pages % ng == 0
    nc = n_pages * CHUNKS_PER_PAGE
    nb = past_len // SEL_BLOCK
    cov = jnp.asarray(_covers(nc, nb).T, BF16)
    keep = win_k.shape[3]
    page_spec = lambda i: pl.BlockSpec((1, 1, W, page), lambda b, c, pt: (layer, pt[b, c * ng + i], 0, 0))
    perb = lambda *s: pl.BlockSpec((1,) + s, lambda b, c, pt: (b,) + (0,) * len(s))
    win_spec = pl.BlockSpec((1, 1, W, keep), lambda b, c, pt: (layer, b, 0, 0))
    const = lambda a: pl.BlockSpec(a.shape, lambda b, c, pt: (0,) * a.ndim)
    consts = [*cwk, *cwv, cov]
    hw = cwk[0].shape[1]
    hm = jax.ShapeDtypeStruct((B, A_KV_HEADS, HEAD_ROWS, A_HEAD_DIM), F32)
    grid_spec = pltpu.PrefetchScalarGridSpec(
        num_scalar_prefetch=1, grid=(B, n_pages // ng),
        in_specs=[page_spec(i) for i in range(ng)] * 2
                 + [perb(A_KV_HEADS, HEAD_ROWS, A_HEAD_DIM), win_spec, win_spec, perb(1, W), perb(1, W)]
                 + [const(a) for a in consts],
        out_specs=[perb(A_KV_HEADS, HEAD_ROWS, A_HEAD_DIM), perb(A_KV_HEADS, HEAD_ROWS, A_HEAD_DIM),
                   perb(PICK_ROWS, HEAD_ROWS)],
        scratch_shapes=[pltpu.VMEM((nc, hw), F32), pltpu.VMEM((nc, hw), F32),
                        pltpu.VMEM((ng * page, W), F32), pltpu.VMEM((ng * page, W), F32)])
    return pl.pallas_call(
        functools.partial(_nsa_sample_cmp_kernel, past_len=past_len), grid_spec=grid_spec,
        out_shape=[hm, hm, jax.ShapeDtypeStruct((B, PICK_ROWS, HEAD_ROWS), jnp.int32)],
        compiler_params=_cparams("parallel", "arbitrary"), name="nsa_sample_cmp",
    )(page_table, *([pool_k] * ng), *([pool_v] * ng), q_hm, win_k, win_v, kw_new, vw_new, *consts)


SEL_PICKS_PER_STEP = 5


def _nsa_sample_sel_kernel(pt_ref, pk_ref, *refs, npk):
    del pt_ref
    P = SEL_PICKS_PER_STEP
    n_pages = A_KV_HEADS * P
    k_refs, v_refs = refs[:n_pages], refs[n_pages:2 * n_pages]
    q_ref, gate_ref, ksn_ref, vsn_ref, ocmp_ref, owin_ref, o_ref, m_scr, l_scr, acc_scr = refs[2 * n_pages:]
    b = pl.program_id(0)
    c = pl.program_id(1)
    d = A_HEAD_DIM
    page = k_refs[0].shape[3]
    tok_blk = lax.broadcasted_iota(jnp.int32, (HEAD_ROWS, page), 1) // SEL_BLOCK
    for g in range(A_KV_HEADS):
        gl = slice(g * d, (g + 1) * d)
        qg = q_ref[0, g] * A_SCALE

        @pl.when(c == 0)
        def _():
            m_scr[g] = jnp.sum(qg * ksn_ref[0, :, gl], axis=1, keepdims=True)
            l_scr[g] = jnp.ones((HEAD_ROWS, 1), F32)
            acc_scr[g] = jnp.broadcast_to(vsn_ref[0, :, gl], (HEAD_ROWS, d))

        chosen = [tok_blk == pk_ref[b, g * npk + c * P + i] % (page // SEL_BLOCK) for i in range(P)]
        s = [jnp.where(chosen[i], _dot(qg, k_refs[g * P + i][0, 0, gl, :]), NEG) for i in range(P)]
        m_old = m_scr[g]
        m = m_old
        for i in range(P):
            m = jnp.maximum(m, jnp.max(s[i], axis=1, keepdims=True))
        alpha = jnp.exp(m_old - m)
        l = alpha * l_scr[g]
        acc = alpha * acc_scr[g]
        for i in range(P):
            p = jnp.where(chosen[i], jnp.exp(s[i] - m), 0.0)
            l = l + jnp.sum(p, axis=1, keepdims=True)
            acc = acc + _dot_nt(p, v_refs[g * P + i][0, 0, gl, :])
        m_scr[g] = m
        l_scr[g] = l
        acc_scr[g] = acc

        @pl.when(c == pl.num_programs(1) - 1)
        def _():
            gate = _sigmoid(gate_ref[0, g])
            o_ref[0, g] = (gate[:, 0:1] * ocmp_ref[0, g] + gate[:, 1:2] * (acc / l)
                           + gate[:, 2:3] * owin_ref[0, g])


def nsa_sample_sel(page_table, picks, pool_k, pool_v, layer, q_hm, gate_hm, ks_new, vs_new, o_cmp, o_win):
    B = page_table.shape[0]
    _, n_pool, W, page = pool_k.shape
    per_page = page // SEL_BLOCK
    npk = picks.shape[2]
    P = SEL_PICKS_PER_STEP
    assert npk % P == 0
    picks = picks.reshape(B, A_KV_HEADS * npk)

    def blk(g, i):
        def index(b, c, pt, pk):
            return (layer, pt[b, pk[b, g * npk + c * P + i] // per_page], 0, 0)
        return pl.BlockSpec((1, 1, W, page), index)

    pages = [blk(g, i) for g in range(A_KV_HEADS) for i in range(P)]
    perb = lambda *s: pl.BlockSpec((1,) + s, lambda b, c, pt, pk: (b,) + (0,) * len(s))
    hm_spec = perb(A_KV_HEADS, HEAD_ROWS, A_HEAD_DIM)
    grid_spec = pltpu.PrefetchScalarGridSpec(
        num_scalar_prefetch=2, grid=(B, npk // P),
        in_specs=pages * 2 + [hm_spec, perb(A_KV_HEADS, HEAD_ROWS, 3), perb(1, W), perb(1, W), hm_spec, hm_spec],
        out_specs=hm_spec,
        scratch_shapes=[pltpu.VMEM((A_KV_HEADS, HEAD_ROWS, 1), F32), pltpu.VMEM((A_KV_HEADS, HEAD_ROWS, 1), F32),
                        pltpu.VMEM((A_KV_HEADS, HEAD_ROWS, A_HEAD_DIM), F32)])
    return pl.pallas_call(
        functools.partial(_nsa_sample_sel_kernel, npk=npk), grid_spec=grid_spec,
        out_shape=jax.ShapeDtypeStruct((B, A_KV_HEADS, HEAD_ROWS, A_HEAD_DIM), F32),
        compiler_params=_cparams("parallel", "arbitrary"), name="nsa_sample_sel",
    )(page_table, picks, *([pool_k] * len(pages)), *([pool_v] * len(pages)), q_hm, gate_hm, ks_new, vs_new,
      o_cmp, o_win)


def nsa_sample(a_q, a_gate, new_rows, pools, win_k, win_v, layer, page_table, cwk, cwv):
    B = a_q.shape[0]
    past_len = page_table.shape[1] * PAGE_SIZE
    _, _, ksn, vsn, kwn, vwn = new_rows
    pool_kc, pool_vc, pool_ks, pool_vs = pools
    padj = ((0, 0), (0, 0), (0, HEAD_ROWS - A_GROUP), (0, 0))
    q_hm = jnp.pad(a_q.reshape(B, A_KV_HEADS, A_GROUP, A_HEAD_DIM), padj)
    gate_hm = jnp.pad(jnp.transpose(a_gate.reshape(B, 3, A_KV_HEADS, A_GROUP), (0, 2, 3, 1)), padj)
    o_cmp, o_win, picks = nsa_sample_cmp(page_table, pool_kc, pool_vc, q_hm, win_k, win_v, layer, kwn, vwn,
                                         cwk, cwv, past_len)
    picks = jnp.swapaxes(picks, 1, 2)[:, :A_KV_HEADS, :N_SEL - 1]
    o = nsa_sample_sel(page_table, picks, pool_ks, pool_vs, layer, q_hm, gate_hm, ksn, vsn, o_cmp, o_win)
    return o[:, :, :A_GROUP].reshape(B, 1, A_WIDTH)


PROJ_GROUPS = (2 * M_WIDTH, M_WIDTH, M_WIDTH, 2 * M_HEADS, A_WIDTH) + (A_KV_LANES,) * 6 + (
    3 * A_HEADS, R_COLS, 3 * D_MODEL)
PROMPT_ROW_TILE = 256
MLSTM_CHUNK = 128
RWKV_CHUNK = 64


def _layer_params(l, w_in, w_branch_m, w_branch_a, w_branch_r, w_out, w_gate_up, w_down,
                  nsa_cmp_pe, nsa_cmp_w1, nsa_cmp_w2):
    offs = np.cumsum((0,) + PROJ_GROUPS)
    assert offs[-1] == w_in.shape[2]
    wi = w_in[l].astype(BF16)
    return {
        'proj': [wi[:, a:b] for a, b in zip(offs[:-1], offs[1:])],
        'wm': w_branch_m[l].astype(BF16), 'wa': w_branch_a[l].astype(BF16), 'wr': w_branch_r[l].astype(BF16),
        'wo': w_out[l].astype(BF16), 'wgu': w_gate_up[l].astype(BF16), 'wd': w_down[l].astype(BF16),
        'cwk': compress_weights(nsa_cmp_pe[l, 0], nsa_cmp_w1[l, 0], nsa_cmp_w2[l, 0]),
        'cwv': compress_weights(nsa_cmp_pe[l, 1], nsa_cmp_w1[l, 1], nsa_cmp_w2[l, 1]),
    }


def _trunk_layer(x, lw, lp, nsa_fn, m_state, r_state, tm, chunk, rwkv_chunk, gf, final_norm):
    B, T, D = x.shape
    x2d = x.reshape(B * T, D)
    outs = norm_proj(x2d, lp['norm1_g'], lw['proj'], tm)
    (m_qk, m_v, m_o, m_if, a_q), new_rows, (a_gate, r_cols, merge) = outs[:5], outs[5:11], outs[11:]
    b3 = lambda a: a.reshape(B, T, a.shape[-1])
    new_rows = [b3(r) for r in new_rows]
    y_m, conv_n, c_n, n_n, m_n = mlstm(b3(m_qk), b3(m_v), b3(m_o), b3(m_if), *m_state,
                                        lp['mlstm_conv_w'], lp['mlstm_conv_b'], lp['mlstm_if_bias'], chunk)
    y_a = nsa_fn(b3(a_q), b3(a_gate), new_rows)
    r_cols = b3(r_cols)
    y_r, s_n = rwkv(r_cols, r_state[0], r_state[1], lp, rwkv_chunk)
    x_new = merge_ffn(x2d, y_m.reshape(B * T, -1), y_a.reshape(B * T, -1), y_r.reshape(B * T, -1), merge,
                      lw['wm'], lw['wa'], lw['wr'], lw['wo'], lp['norm2_g'], lw['wgu'], lw['wd'], gf,
                      tm, final_norm)
    return x_new.reshape(B, T, D), new_rows, (conv_n, c_n, n_n, m_n), (r_cols[:, -1:], s_n)


def kernel(x_prompt, x_sample, cache_cmp_k, cache_cmp_v, cache_slc_k, cache_slc_v, cache_win_k, cache_win_v, state_mlstm_conv, state_mlstm_C, state_mlstm_n, state_mlstm_m, state_rwkv_shift, state_rwkv, page_table, norm1_g, w_in, mlstm_if_bias, mlstm_conv_w, mlstm_conv_b, nsa_cmp_pe, nsa_cmp_w1, nsa_cmp_w2, rwkv_mu, rwkv_w0, rwkv_w2, rwkv_a0, rwkv_a2, rwkv_g2, rwkv_k_k, rwkv_k_a, rwkv_r_k, rwkv_ln_g, rwkv_ln_b, w_branch_m, w_branch_a, w_branch_r, w_out, norm2_g, w_gate_up, w_down, final_norm_g):
    bp, tp, _ = x_prompt.shape
    bs, ts, _ = x_sample.shape
    assert ts == 1
    depth = w_in.shape[0]
    kvshape = lambda a: a.reshape(a.shape[0], a.shape[1], A_KV_HEADS, A_HEAD_DIM)
    xp, xs = x_prompt, x_sample
    pools = [channel_major(c) for c in (cache_cmp_k, cache_cmp_v, cache_slc_k, cache_slc_v)]
    win_k, win_v = channel_major(cache_win_k), channel_major(cache_win_v)
    states_p, states_s = [], []
    for l in range(depth):
        lw = _layer_params(l, w_in, w_branch_m, w_branch_a, w_branch_r, w_out, w_gate_up, w_down,
                           nsa_cmp_pe, nsa_cmp_w1, nsa_cmp_w2)
        lp = {'norm1_g': norm1_g[l], 'mlstm_if_bias': mlstm_if_bias[l], 'mlstm_conv_w': mlstm_conv_w[l],
              'mlstm_conv_b': mlstm_conv_b[l], 'rwkv_mu': rwkv_mu[l], 'rwkv_w0': rwkv_w0[l],
              'rwkv_w2': rwkv_w2[l], 'rwkv_a0': rwkv_a0[l], 'rwkv_a2': rwkv_a2[l], 'rwkv_g2': rwkv_g2[l],
              'rwkv_k_k': rwkv_k_k[l], 'rwkv_k_a': rwkv_k_a[l], 'rwkv_r_k': rwkv_r_k[l],
              'rwkv_ln_g': rwkv_ln_g[l], 'rwkv_ln_b': rwkv_ln_b[l], 'norm2_g': norm2_g[l]}
        last = l == depth - 1

        def nsa_p(a_q, a_gate, rows):
            kc = compress_rows(rows[0], lw['cwk'])
            vc = compress_rows(rows[1], lw['cwv'])
            return nsa_prompt_attend(a_q, a_gate, kc, vc, *rows[2:])

        m_zero = (jnp.zeros((bp, M_CONV - 1, 2 * M_WIDTH), F32),
                  jnp.zeros((bp, M_HEADS, M_HEAD_DIM, M_HEAD_DIM), F32),
                  jnp.zeros((bp, M_HEADS, M_HEAD_DIM), F32), jnp.zeros((bp, M_HEADS), F32))
        r_zero = (jnp.zeros((bp, 1, R_COLS), F32), jnp.zeros((bp, R_HEADS, R_HEAD_DIM, R_HEAD_DIM), F32))
        xp, rows, m_new, r_new = _trunk_layer(xp, lw, lp, nsa_p, m_zero, r_zero, PROMPT_ROW_TILE,
                                              MLSTM_CHUNK, RWKV_CHUNK, final_norm_g, last)
        keep = min(WINDOW, tp)
        states_p.append(tuple(kvshape(r) for r in rows[:4])
                        + (kvshape(rows[4][:, tp - keep:]), kvshape(rows[5][:, tp - keep:])) + m_new + r_new)

        def nsa_s(a_q, a_gate, rows, l=l):
            return nsa_sample(a_q, a_gate, rows, pools, win_k, win_v, l, page_table, lw['cwk'], lw['cwv'])

        xs, rows, m_new, r_new = _trunk_layer(
            xs, lw, lp, nsa_s,
            (state_mlstm_conv[l], state_mlstm_C[l], state_mlstm_n[l], state_mlstm_m[l]),
            (state_rwkv_shift[l], state_rwkv[l]), bs * ts, SUBLANES, SUBLANES, final_norm_g, last)
        keep = win_k.shape[3]

        def win_new(win, new_row):
            w = jnp.concatenate([win[l], jnp.swapaxes(new_row, 1, 2)], axis=2)[:, :, -keep:]
            return jnp.transpose(w.reshape(bs, A_KV_HEADS, A_HEAD_DIM, keep), (0, 3, 1, 2))

        states_s.append(tuple(kvshape(r) for r in rows[:4])
                        + (win_new(win_k, rows[4]), win_new(win_v, rows[5])) + m_new + r_new)

    stack = lambda states: [jnp.stack(a) for a in zip(*states)]
    return (xp, xs, *stack(states_p), *stack(states_s))
```

```python
import functools
import math

import numpy as np
import jax
import jax.numpy as jnp
from jax import lax
from jax.experimental import pallas as pl
from jax.experimental.pallas import tpu as pltpu

F32 = jnp.float32
BF16 = jnp.bfloat16

D_MODEL = 1024
DEPTH = 2
PAGE_SIZE = 128
M_HEADS = 4
M_HEAD_DIM = 128
M_WIDTH = M_HEADS * M_HEAD_DIM
M_CONV = 4
A_HEADS = 8
A_KV_HEADS = 2
A_HEAD_DIM = 64
A_WIDTH = A_HEADS * A_HEAD_DIM
A_GROUP = A_HEADS // A_KV_HEADS
A_SCALE = A_HEAD_DIM ** -0.5
A_KV_LANES = A_KV_HEADS * A_HEAD_DIM
CMP_STRIDE = 16
CMP_BLOCK = 2 * CMP_STRIDE
CMP_HIDDEN = A_HEAD_DIM
SEL_BLOCK = 64
N_SEL = 16
WINDOW = 512
Q_BLOCK = 128
R_HEADS = 8
R_HEAD_DIM = 64
R_WIDTH = R_HEADS * R_HEAD_DIM
R_DECAY_RANK = 64
R_A_RANK = 64
R_GATE_RANK = 128
R_COLS = 3 * R_WIDTH + R_DECAY_RANK + R_A_RANK + R_GATE_RANK
LN_X_EPS = 64e-5
D_FF = ((-(-8 * D_MODEL // 3)) + 255) // 256 * 256
IN_SPLITS = (2 * M_WIDTH, M_WIDTH, M_WIDTH, 2 * M_HEADS,
             A_WIDTH, 6 * A_KV_HEADS * A_HEAD_DIM, 3 * A_HEADS,
             R_COLS, 3 * D_MODEL)
RMS_EPS = 1e-6
NEG = -1e30
FORCE = 1e9
LOWEST = -3e38

VMEM_LIMIT_BYTES = 56 * 1024 * 1024
SUBLANES = 8
LANES = 128


def _cparams(*sem):
    return pltpu.CompilerParams(dimension_semantics=sem, vmem_limit_bytes=VMEM_LIMIT_BYTES)


def _resident(shape):
    nd = len(shape)
    return pl.BlockSpec(shape, lambda *_: (0,) * nd, pipeline_mode=pl.Buffered(1))


def _dot(a, b):
    return jnp.dot(a.astype(BF16), b.astype(BF16), preferred_element_type=F32)


def _dot_nt(a, b):
    return lax.dot_general(a.astype(BF16), b.astype(BF16), (((1,), (1,)), ((), ())),
                           preferred_element_type=F32)


def _split3(x):
    hi = x.astype(BF16)
    r = x - hi.astype(F32)
    mid = r.astype(BF16)
    lo = (r - mid.astype(F32)).astype(BF16)
    return hi, mid, lo


def _dot_f32_rhs(a_bf16, x):
    hi, mid, lo = _split3(x)
    d = lambda p: jnp.dot(a_bf16, p, preferred_element_type=F32)
    return d(hi) + d(mid) + d(lo)


def _dot_f32_rhs_nt(a_bf16, x):
    hi, mid, lo = _split3(x)
    d = lambda p: lax.dot_general(a_bf16, p, (((1,), (1,)), ((), ())), preferred_element_type=F32)
    return d(hi) + d(mid) + d(lo)


def _transpose_f32(x, eye_bf16):
    hi, mid, lo = _split3(x)
    d = lambda p: lax.dot_general(eye_bf16, p, (((1,), (1,)), ((), ())), preferred_element_type=F32)
    return d(hi) + d(mid) + d(lo)


def _eye(n, dtype=BF16):
    r = lax.broadcasted_iota(jnp.int32, (n, n), 0)
    c = lax.broadcasted_iota(jnp.int32, (n, n), 1)
    return (r == c).astype(dtype)


def _sigmoid(x):
    return 1.0 / (1.0 + jnp.exp(-x))


def _silu(x):
    return x * _sigmoid(x)


def _softplus(x):
    return jnp.maximum(x, 0.0) + jnp.log(1.0 + jnp.exp(-jnp.abs(x)))


def _rmsnorm(x, g):
    return x * lax.rsqrt(jnp.mean(x * x, axis=-1, keepdims=True) + RMS_EPS) * g


def _norm_proj_kernel(x_ref, g_ref, *refs):
    n = len(refs) // 2
    hb = _rmsnorm(x_ref[...], g_ref[...]).astype(BF16)
    for w_ref, o_ref in zip(refs[:n], refs[n:]):
        o_ref[...] = jnp.dot(hb, w_ref[...], preferred_element_type=F32)


def norm_proj(x2d, g, weights, tm):
    m, d = x2d.shape
    assert m % tm == 0
    row = lambda i: (i, 0)
    in_specs = [pl.BlockSpec((tm, d), row), _resident((1, d))]
    in_specs += [_resident(w.shape) for w in weights]
    out_specs = [pl.BlockSpec((tm, w.shape[1]), row) for w in weights]
    out_shape = [jax.ShapeDtypeStruct((m, w.shape[1]), F32) for w in weights]
    return pl.pallas_call(
        _norm_proj_kernel, grid=(m // tm,), in_specs=in_specs, out_specs=out_specs,
        out_shape=out_shape, compiler_params=_cparams("parallel"), name="norm_proj",
    )(x2d, g.reshape(1, d), *weights)


def _merge_ffn_kernel(x_ref, ym_ref, ya_ref, yr_ref, mg_ref, wm_ref, wa_ref, wr_ref, wo_ref,
                      g2_ref, wgu_ref, wd_ref, gf_ref, o_ref, *, final_norm):
    mg = mg_ref[...]
    merged = (_sigmoid(mg[:, :D_MODEL]) * _dot(ym_ref[...], wm_ref[...])
              + _sigmoid(mg[:, D_MODEL:2 * D_MODEL]) * _dot(ya_ref[...], wa_ref[...])
              + _sigmoid(mg[:, 2 * D_MODEL:]) * _dot(yr_ref[...], wr_ref[...]))
    x1 = x_ref[...] + _dot(merged, wo_ref[...])
    gu = _dot(_rmsnorm(x1, g2_ref[...]), wgu_ref[...])
    x2 = x1 + _dot(_silu(gu[:, :D_FF]) * gu[:, D_FF:], wd_ref[...])
    o_ref[...] = _rmsnorm(x2, gf_ref[...]) if final_norm else x2


def merge_ffn(x2d, ym, ya, yr, mg, wm, wa, wr, wo, g2, wgu, wd, gf, tm, final_norm):
    m, d = x2d.shape
    row = lambda i: (i, 0)
    acts = [x2d, ym, ya, yr, mg]
    consts = [wm, wa, wr, wo, g2.reshape(1, d), wgu, wd, gf.reshape(1, d)]
    in_specs = [pl.BlockSpec((tm, a.shape[1]), row) for a in acts]
    in_specs += [_resident(c.shape) for c in consts]
    return pl.pallas_call(
        functools.partial(_merge_ffn_kernel, final_norm=final_norm),
        grid=(m // tm,), in_specs=in_specs, out_specs=pl.BlockSpec((tm, d), row),
        out_shape=jax.ShapeDtypeStruct((m, d), F32), compiler_params=_cparams("parallel"),
        name="merge_ffn",
    )(*acts, *consts)


def _mlstm_kernel(qk_ref, v_ref, o_ref, if_ref, conv0_ref, c0_ref, n0_ref, m0_ref,
                  cw_ref, cb_ref, ifb_ref,
                  y_ref, conv_out_ref, c_out_ref, n_out_ref, m_out_ref,
                  cbuf, c_scr, n_scr, m_scr, *, chunk, n_last):
    c = pl.program_id(1)
    nc = pl.num_programs(1)
    L = chunk
    pad = SUBLANES

    @pl.when(c == 0)
    def _():
        cbuf[0:pad, :] = conv0_ref[0]
        c_scr[...] = c0_ref[0]
        n_scr[...] = n0_ref[0]
        m_scr[...] = m0_ref[0]

    cbuf[pad:pad + L, :] = qk_ref[0]
    acc = cb_ref[...]
    for j in range(M_CONV):
        acc = acc + cbuf[pad - (M_CONV - 1) + j: pad - (M_CONV - 1) + j + L, :] * cw_ref[j:j + 1, :]
    qk = _silu(acc)

    @pl.when(c == nc - 1)
    def _():
        conv_out_ref[0] = cbuf[n_last:n_last + pad, :]

    cbuf[0:pad, :] = cbuf[L:L + pad, :]

    gates = if_ref[0] + ifb_ref[...]
    lane8 = lax.broadcasted_iota(jnp.int32, (L, 2 * M_HEADS), 1)
    lf_all = -_softplus(-gates)
    lif = jnp.where(lane8 < M_HEADS, gates, lf_all)
    row_id = lax.broadcasted_iota(jnp.int32, (L, 2 * M_HEADS), 0)
    is_last_chunk = c == nc - 1
    if n_last < L:
        dead = jnp.logical_and(is_last_chunk, row_id >= n_last)
        lif = jnp.where(dead, jnp.where(lane8 < M_HEADS, NEG, 0.0), lif)
    t_i = lax.broadcasted_iota(jnp.int32, (L, L), 0)
    s_i = lax.broadcasted_iota(jnp.int32, (L, L), 1)
    causal = s_i <= t_i
    tri = causal.astype(BF16)
    csum = _dot_f32_rhs(tri, lif)
    lib = jnp.where(lane8 < M_HEADS, lif, csum)
    lib_t = _transpose_f32(lib, _eye(2 * M_HEADS))
    eye_h = _eye(M_HEAD_DIM)
    last = n_last - 1

    heads = range(M_HEADS)
    sl = [slice(h * M_HEAD_DIM, (h + 1) * M_HEAD_DIM) for h in heads]
    q = [qk[:, sl[h]] for h in heads]
    k = [qk[:, M_WIDTH + h * M_HEAD_DIM: M_WIDTH + (h + 1) * M_HEAD_DIM] * (M_HEAD_DIM ** -0.5) for h in heads]
    v = [v_ref[0, :, sl[h]] for h in heads]
    li_col = [lib[:, h:h + 1] for h in heads]
    b_col = [lib[:, M_HEADS + h:M_HEADS + h + 1] for h in heads]
    m_prev = [m_scr[0:1, h:h + 1] for h in heads]
    c_old = [c_scr[h] for h in heads]
    n_old = [n_scr[h:h + 1, :] for h in heads]
    s_qk = [_dot_nt(q[h], k[h]) for h in heads]
    q_c = [_dot(q[h], c_old[h]) for h in heads]
    dmat = [jnp.where(causal, b_col[h] - lib_t[M_HEADS + h:M_HEADS + h + 1, :] + lib_t[h:h + 1, :], NEG)
            for h in heads]
    inter = [b_col[h] + m_prev[h] for h in heads]
    m_t = [jnp.maximum(inter[h], jnp.max(dmat[h], axis=1, keepdims=True)) for h in heads]
    att = [jnp.exp(dmat[h] - m_t[h]) * s_qk[h] for h in heads]
    w_inter = [jnp.exp(inter[h] - m_t[h]) for h in heads]
    num = [_dot(att[h], v[h]) + w_inter[h] * q_c[h] for h in heads]
    den = [jnp.sum(att[h], axis=1, keepdims=True) + w_inter[h] * jnp.sum(q[h] * n_old[h], axis=1, keepdims=True)
           for h in heads]
    for h in heads:
        hh = num[h] / jnp.maximum(jnp.abs(den[h]), jnp.exp(-m_t[h]))
        y_ref[0, :, sl[h]] = _sigmoid(o_ref[0, :, sl[h]]) * hh

    m_new = [m_t[h][last:last + 1, :] for h in heads]
    b_last = [b_col[h][last:last + 1, :] for h in heads]
    scale = [jnp.exp(b_last[h] + m_prev[h] - m_new[h]) for h in heads]
    kw = [k[h] * jnp.exp(b_last[h] - b_col[h] + li_col[h] - m_new[h]) for h in heads]
    kw_t = [lax.dot_general(eye_h, kw[h].astype(BF16), DOT_NT, preferred_element_type=F32)
            for h in heads]
    for h in heads:
        c_scr[h] = scale[h] * c_old[h] + _dot(kw_t[h], v[h])
        n_scr[h:h + 1, :] = scale[h] * n_old[h] + jnp.sum(kw[h], axis=0, keepdims=True)
        m_scr[0:1, h:h + 1] = m_new[h]

    @pl.when(is_last_chunk)
    def _():
        c_out_ref[0] = c_scr[...]
        n_out_ref[0] = n_scr[...]
        m_out_ref[0] = m_scr[...]


def mlstm(m_qk, m_v, m_o, m_if, conv0, c0, n0, m0, conv_w, conv_b, if_bias, chunk):
    B, T, _ = m_qk.shape
    nc = -(-T // chunk)
    tp = nc * chunk
    n_last = T - (nc - 1) * chunk
    if tp != T:
        padt = lambda a: jnp.pad(a, ((0, 0), (0, tp - T), (0, 0)))
        m_qk, m_v, m_o, m_if = padt(m_qk), padt(m_v), padt(m_o), padt(m_if)
    conv0p = jnp.pad(conv0, ((0, 0), (SUBLANES - (M_CONV - 1), 0), (0, 0)))
    n0p = jnp.pad(n0, ((0, 0), (0, SUBLANES - M_HEADS), (0, 0)))
    m0p = jnp.pad(m0, ((0, 0), (0, LANES - M_HEADS))).reshape(B, 1, LANES)
    cw = jnp.pad(conv_w, ((0, SUBLANES - M_CONV), (0, 0)))
    tile = lambda w: pl.BlockSpec((1, chunk, w), lambda b, c: (b, c, 0))
    perb = lambda *s: pl.BlockSpec((1,) + s, lambda b, c: (b,) + (0,) * len(s))
    outs = pl.pallas_call(
        functools.partial(_mlstm_kernel, chunk=chunk, n_last=n_last),
        grid=(B, nc),
        in_specs=[tile(2 * M_WIDTH), tile(M_WIDTH), tile(M_WIDTH), tile(2 * M_HEADS),
                  perb(SUBLANES, 2 * M_WIDTH), perb(M_HEADS, M_HEAD_DIM, M_HEAD_DIM),
                  perb(SUBLANES, M_HEAD_DIM), perb(1, LANES),
                  _resident((SUBLANES, 2 * M_WIDTH)), _resident((1, 2 * M_WIDTH)),
                  _resident((1, 2 * M_HEADS))],
        out_specs=[tile(M_WIDTH), perb(SUBLANES, 2 * M_WIDTH),
                   perb(M_HEADS, M_HEAD_DIM, M_HEAD_DIM), perb(SUBLANES, M_HEAD_DIM), perb(1, LANES)],
        out_shape=[jax.ShapeDtypeStruct((B, tp, M_WIDTH), F32),
                   jax.ShapeDtypeStruct((B, SUBLANES, 2 * M_WIDTH), F32),
                   jax.ShapeDtypeStruct((B, M_HEADS, M_HEAD_DIM, M_HEAD_DIM), F32),
                   jax.ShapeDtypeStruct((B, SUBLANES, M_HEAD_DIM), F32),
                   jax.ShapeDtypeStruct((B, 1, LANES), F32)],
        scratch_shapes=[pltpu.VMEM((chunk + SUBLANES, 2 * M_WIDTH), F32),
                        pltpu.VMEM((M_HEADS, M_HEAD_DIM, M_HEAD_DIM), F32),
                        pltpu.VMEM((SUBLANES, M_HEAD_DIM), F32),
                        pltpu.VMEM((1, LANES), F32)],
        compiler_params=_cparams("parallel", "arbitrary"), name="mlstm",
    )(m_qk, m_v, m_o, m_if, conv0p, c0, n0p, m0p, cw, conv_b.reshape(1, -1), if_bias.reshape(1, -1))
    y, conv_o, c_o, n_o, m_o_ = outs
    return (y[:, :T], conv_o[:, SUBLANES - (M_CONV - 1):], c_o, n_o[:, :M_HEADS], m_o_[:, 0, :M_HEADS])


def _rwkv_project(cols, prev, mu_ref, w0_ref, w2_ref, a0_ref, a2_ref, g2_ref, kk_ref, ka_ref, rk_ref):
    mixed = cols + (prev - cols) * mu_ref[...]
    W = R_WIDTH
    r = mixed[:, :W]
    k = mixed[:, W:2 * W]
    v = mixed[:, 2 * W:3 * W]
    o = 3 * W
    wd = mixed[:, o:o + R_DECAY_RANK]
    ad = mixed[:, o + R_DECAY_RANK:o + R_DECAY_RANK + R_A_RANK]
    gd = mixed[:, o + R_DECAY_RANK + R_A_RANK:]
    w = -_softplus(-(w0_ref[...] + _dot(jnp.tanh(wd), w2_ref[...]))) - 0.5
    a_gate = _sigmoid(a0_ref[...] + _dot(ad, a2_ref[...]))
    g = _dot(_sigmoid(gd), g2_ref[...])
    kk = k * kk_ref[...]
    k2 = k * (1.0 + (a_gate - 1.0) * ka_ref[...])
    rkk = r * k2 * rk_ref[...]
    a_h, b_h, bonus_h = [], [], []
    for h in range(R_HEADS):
        sl = slice(h * R_HEAD_DIM, (h + 1) * R_HEAD_DIM)
        kkh = kk[:, sl]
        kkh = kkh * lax.rsqrt(jnp.maximum(jnp.sum(kkh * kkh, axis=1, keepdims=True), 1e-24))
        a_h.append(-kkh)
        b_h.append(kkh * a_gate[:, sl])
        bonus_h.append(jnp.sum(rkk[:, sl], axis=1, keepdims=True) * v[:, sl])
    return r, -jnp.exp(w), k2, v, g, a_h, b_h, bonus_h


RWKV_SEQS_PER_STEP = 4
DOT_NN = (((1,), (0,)), ((), ()))
DOT_NT = (((1,), (1,)), ((), ()))
DOT_TN = (((0,), (0,)), ((), ()))


def _dot_x3(x, y, dims):
    xh = x.astype(BF16)
    xl = (x - xh.astype(F32)).astype(BF16)
    yh = y.astype(BF16)
    yl = (y - yh.astype(F32)).astype(BF16)
    d = lambda p, q: lax.dot_general(p, q, dims, preferred_element_type=F32)
    return d(xh, yh) + d(xh, yl) + d(xl, yh)


def _dot_tn(a, b):
    return lax.dot_general(a.astype(BF16), b.astype(BF16), DOT_TN, preferred_element_type=F32)


def _rwkv_chunk_kernel(cols_ref, shift0_ref, mu_ref, w0_ref, w2_ref, a0_ref, a2_ref, g2_ref, kk_ref, ka_ref,
                       rk_ref, lng_ref, lnb_ref, s0_ref, o_ref, s_out_ref, s_scr, carry, *, chunk, n_valid):
    t = pl.program_id(1)
    nt = pl.num_programs(1)
    L, N = chunk, R_HEAD_DIM

    @pl.when(t == 0)
    def _():
        s_scr[...] = s0_ref[...]
        carry[...] = shift0_ref[...]

    ti = lax.broadcasted_iota(jnp.int32, (L, L), 0)
    si = lax.broadcasted_iota(jnp.int32, (L, L), 1)
    incl = si <= ti
    strict = si < ti
    eye = (si == ti).astype(F32)
    tri = incl.astype(BF16)
    first_row = lax.broadcasted_iota(jnp.int32, (L, R_COLS), 0) == 0
    n_seq = cols_ref.shape[0]
    ah, rh, bt, kt, vh, bb, kb, gm, gate, bonus = [], [], [], [], [], [], [], [], [], []
    for i in range(n_seq):
        cols = cols_ref[i]
        prev = jnp.where(first_row, carry[i], pltpu.roll(cols, 1, 0))
        carry[i] = cols[L - 1:L, :]
        r, lw, k, v, g, a_h, b_h, bonus_h = _rwkv_project(cols, prev, mu_ref, w0_ref, w2_ref, a0_ref, a2_ref,
                                                          g2_ref, kk_ref, ka_ref, rk_ref)
        bonus += bonus_h
        if n_valid % L:
            live = (t * L + lax.broadcasted_iota(jnp.int32, (L, R_WIDTH), 0)) < n_valid
            lw, k, v = (jnp.where(live, z, 0.0) for z in (lw, k, v))
            a_h = [jnp.where(live[:, :N], z, 0.0) for z in a_h]
            b_h = [jnp.where(live[:, :N], z, 0.0) for z in b_h]
        c = _dot_f32_rhs(tri, lw)
        c_end = c[L - 1:L, :]
        e_prev = jnp.exp(c - lw)
        r_hat = r * jnp.exp(c)
        e_neg = jnp.exp(-c)
        e_end = jnp.exp(c_end - c)
        gam = jnp.exp(c_end)
        for h in range(R_HEADS):
            hs = slice(h * N, (h + 1) * N)
            ah.append(a_h[h] * e_prev[:, hs])
            rh.append(r_hat[:, hs])
            bt.append(b_h[h] * e_neg[:, hs])
            kt.append(k[:, hs] * e_neg[:, hs])
            vh.append(v[:, hs])
            bb.append(b_h[h] * e_end[:, hs])
            kb.append(k[:, hs] * e_end[:, hs])
            gm.append(gam[:, hs])
            gate.append(g[:, hs])

    heads = range(n_seq * R_HEADS)
    unit = [(i, h, slice(h * N, (h + 1) * N)) for i in range(n_seq) for h in range(R_HEADS)]
    a_ab = [jnp.where(strict, _dot_x3(ah[h], bt[h], DOT_NT), 0.0) for h in heads]
    a_ak = [jnp.where(strict, _dot_nt(ah[h], kt[h]), 0.0) for h in heads]
    a_rb = [jnp.where(incl, _dot_nt(rh[h], bt[h]), 0.0) for h in heads]
    a_rk = [jnp.where(incl, _dot_nt(rh[h], kt[h]), 0.0) for h in heads]
    inv = [eye + a_ab[h] for h in heads]
    pw = a_ab
    span = 2
    while span < L:
        pw = [_dot(pw[h], pw[h]) for h in heads]
        inv = [inv[h] + _dot(inv[h], pw[h]) for h in heads]
        span *= 2
    s = [s_scr[i, h] for i, h, _ in unit]
    rhs = [_dot_nt(ah[h], s[h]) + _dot(a_ak[h], vh[h]) for h in heads]
    u = [_dot(inv[h], rhs[h]) for h in heads]
    y = [_dot_nt(rh[h], s[h]) + _dot(a_rb[h], u[h]) + _dot(a_rk[h], vh[h]) for h in heads]
    for n, (i, h, hs) in enumerate(unit):
        s_scr[i, h] = s[n] * gm[n] + _dot_tn(u[n], bb[n]) + _dot_tn(vh[n], kb[n])
    for n, (i, h, hs) in enumerate(unit):
        mean = jnp.mean(y[n], axis=1, keepdims=True)
        yc = y[n] - mean
        var = jnp.mean(yc * yc, axis=1, keepdims=True)
        yn = yc * lax.rsqrt(var + LN_X_EPS) * lng_ref[:, hs] + lnb_ref[:, hs]
        o_ref[i, :, hs] = (yn + bonus[n]) * gate[n]

    @pl.when(t == nt - 1)
    def _():
        s_out_ref[...] = s_scr[...]


def rwkv(cols, shift0, s0, p, chunk):
    B, T, _ = cols.shape
    tp = -(-T // chunk) * chunk
    assert chunk % SUBLANES == 0 and chunk & (chunk - 1) == 0
    colsp = jnp.pad(cols, ((0, 0), (0, tp - T), (0, 0))) if tp != T else cols
    vec = lambda a: a.reshape(1, -1)
    consts = [vec(p['rwkv_mu']), vec(p['rwkv_w0']), p['rwkv_w2'].astype(BF16), vec(p['rwkv_a0']),
              p['rwkv_a2'].astype(BF16), p['rwkv_g2'].astype(BF16), vec(p['rwkv_k_k']),
              vec(p['rwkv_k_a']), vec(p['rwkv_r_k']), vec(p['rwkv_ln_g']), vec(p['rwkv_ln_b'])]
    ns = RWKV_SEQS_PER_STEP
    assert B % ns == 0
    seq_spec = lambda w: pl.BlockSpec((ns, chunk, w), lambda b, t: (b, t, 0))
    st_spec = pl.BlockSpec((ns, R_HEADS, R_HEAD_DIM, R_HEAD_DIM), lambda b, t: (b, 0, 0, 0))
    out, s_new = pl.pallas_call(
        functools.partial(_rwkv_chunk_kernel, chunk=chunk, n_valid=T),
        grid=(B // ns, tp // chunk),
        in_specs=[seq_spec(R_COLS), pl.BlockSpec((ns, 1, R_COLS), lambda b, t: (b, 0, 0))]
                 + [_resident(c.shape) for c in consts] + [st_spec],
        out_specs=[seq_spec(R_WIDTH), st_spec],
        out_shape=[jax.ShapeDtypeStruct((B, tp, R_WIDTH), F32), jax.ShapeDtypeStruct(s0.shape, F32)],
        scratch_shapes=[pltpu.VMEM((ns, R_HEADS, R_HEAD_DIM, R_HEAD_DIM), F32),
                        pltpu.VMEM((ns, 1, R_COLS), F32)],
        compiler_params=_cparams("parallel", "arbitrary"), name="rwkv_chunk",
    )(colsp, shift0, *consts, s0)
    return out[:, :T], s_new


def compress_weights(pe, w1, w2):
    d, hdim, G = A_HEAD_DIM, CMP_HIDDEN, A_KV_HEADS
    w1r = w1.reshape(2, CMP_STRIDE, d, hdim)
    same_head = jnp.eye(G, dtype=F32)
    wl = (jnp.transpose(w1r, (1, 2, 0, 3))[:, None, :, :, None, :]
          * same_head[None, :, None, None, :, None])
    wl = wl.reshape(CMP_STRIDE * G * d, 2 * G * hdim).astype(BF16)
    wpe = jnp.transpose(w1r, (1, 0, 2, 3)).reshape(CMP_STRIDE, 2 * d, hdim)
    wpe = jnp.concatenate([wpe] * G, axis=-1).astype(BF16)
    pe_l = jnp.transpose(pe.reshape(2, CMP_STRIDE, d), (1, 0, 2)).reshape(CMP_STRIDE, 1, 2 * d)
    pe_l = jnp.broadcast_to(pe_l, (CMP_STRIDE, SUBLANES, 2 * d))
    w2bd = w2[None, :, None, :] * same_head[:, None, :, None]
    return wl, wpe, pe_l, w2bd.reshape(G * hdim, G * d).astype(BF16)


def _compress_halves(strided_rows, wl_ref, n):
    x = jnp.concatenate([strided_rows(l).astype(BF16) for l in range(CMP_STRIDE)], axis=1)
    return jnp.dot(x, wl_ref[...], preferred_element_type=F32)


def _compress_finish(halves, wpe_ref, pe_ref, w2_ref):
    n = halves.shape[0]
    gh = A_KV_HEADS * CMP_HIDDEN
    pe_term = None
    for l in range(CMP_STRIDE):
        part = _dot(pe_ref[l], wpe_ref[l])
        pe_term = part if pe_term is None else pe_term + part
    second = pltpu.roll(halves[:, gh:], n - 1, 0)
    hid = _silu(halves[:, :gh] + second + pe_term[0:1, :])
    return _dot(hid, w2_ref[...])


def _compress_kernel(x_ref, wl_ref, wpe_ref, pe_ref, w2_ref, o_ref):
    n = o_ref.shape[1]
    rows = lambda l: x_ref[0, pl.ds(l, n, stride=CMP_STRIDE), :]
    o_ref[0] = _compress_finish(_compress_halves(rows, wl_ref, n), wpe_ref, pe_ref, w2_ref)


def compress_rows(rows, cw):
    B, T, W = rows.shape
    n = T // CMP_STRIDE
    return pl.pallas_call(
        _compress_kernel, grid=(B,),
        in_specs=[pl.BlockSpec((1, T, W), lambda b: (b, 0, 0))] + [_resident(c.shape) for c in cw],
        out_specs=pl.BlockSpec((1, n, W), lambda b: (b, 0, 0)),
        out_shape=jax.ShapeDtypeStruct((B, n, W), F32),
        compiler_params=_cparams("parallel"), name="nsa_compress",
    )(rows, *cw)


def _topk_mask_columns(rank, k):
    n_entries = rank.shape[0]
    row = lax.broadcasted_iota(jnp.int32, rank.shape, 0).astype(F32)
    sel = jnp.zeros(rank.shape, F32)
    for _ in range(k):
        m = jnp.max(rank, axis=0, keepdims=True)
        first = jnp.min(jnp.where(rank == m, row, float(n_entries)), axis=0, keepdims=True)
        hit = row == first
        sel = jnp.where(hit, 1.0, sel)
        rank = jnp.where(hit, LOWEST, rank)
    return sel


def _flash_update(state, s, bias, v):
    m_old, l_old, acc = state
    reps = s.shape[0] // bias.shape[0]
    n = bias.shape[0]
    s = jnp.concatenate([s[j * n:(j + 1) * n] + bias for j in range(reps)], axis=0)
    m_new = jnp.maximum(m_old, jnp.max(s, axis=1, keepdims=True))
    p = jnp.exp(s - m_new)
    alpha = jnp.exp(m_old - m_new)
    return (m_new, alpha * l_old + jnp.sum(p, axis=1, keepdims=True), alpha * acc + _dot(p, v))


def _nsa_prompt_kernel(q_ref, gate_ref, kc_ref, vc_ref, ks_ref, vs_ref, kw_ref, vw_ref, covt_ref,
                       o_ref, *, seq, tk, wk):
    i = pl.program_id(1)
    QB, d, J = Q_BLOCK, A_HEAD_DIM, A_GROUP
    nc = kc_ref.shape[1]
    nb = covt_ref.shape[0]
    st = i * QB
    q = q_ref[0] * A_SCALE
    gate = _sigmoid(gate_ref[0])
    rows = J * QB
    t_q = st + lax.broadcasted_iota(jnp.int32, (QB, 1), 0)
    t_rows = jnp.concatenate([t_q] * J, axis=0)
    blk_end = lax.broadcasted_iota(jnp.int32, (rows, nc), 1) * CMP_STRIDE + (CMP_BLOCK - 1)
    cmp_mask = blk_end <= t_rows
    jj = lax.broadcasted_iota(jnp.int32, (nb, QB), 0)
    cur = (st + lax.broadcasted_iota(jnp.int32, (1, QB), 1)) // SEL_BLOCK
    forced = (jj == 0) | (jj == cur) | (jj == cur - 1)
    zeros = lambda w: jnp.zeros((rows, w), F32)

    qgs, o_cmps, ranks = [], [], []
    for g in range(A_KV_HEADS):
        gl = slice(g * d, (g + 1) * d)
        qg = jnp.concatenate([q[:, (g * J + j) * d:(g * J + j + 1) * d] for j in range(J)],
                             axis=0).astype(BF16)
        s = jnp.where(cmp_mask, _dot_nt(qg, kc_ref[0, :, gl]), NEG)
        e = jnp.exp(s - jnp.max(s, axis=1, keepdims=True))
        p = jnp.where(cmp_mask, e / jnp.sum(e, axis=1, keepdims=True), 0.0)
        p_grp = p[0:QB]
        for j in range(1, J):
            p_grp = p_grp + p[j * QB:(j + 1) * QB]
        score = _dot_f32_rhs_nt(covt_ref[...], p_grp)
        qgs.append(qg)
        o_cmps.append(_dot(p, vc_ref[0, :, gl]))
        ranks.append(jnp.where(jj <= cur, jnp.where(forced, FORCE, score), NEG))
    sel_all = _topk_mask_columns(jnp.concatenate(ranks, axis=1), min(N_SEL, nb)).astype(BF16)

    k0w = pl.multiple_of(jnp.clip(st + QB - wk, 0, seq - wk), QB)
    rel = t_q - (k0w + lax.broadcasted_iota(jnp.int32, (1, wk), 1))
    win_bias = jnp.where((rel >= 0) & (rel < WINDOW), 0.0, NEG)

    def sel_tile(kt, states):
        k0 = pl.multiple_of(kt * tk, tk)
        s_pos = k0 + lax.broadcasted_iota(jnp.int32, (1, tk), 1)
        blk_of = k0 // SEL_BLOCK + lax.broadcasted_iota(jnp.int32, (nb, tk), 1) // SEL_BLOCK
        expand = (lax.broadcasted_iota(jnp.int32, (nb, tk), 0) == blk_of).astype(BF16)
        causal = s_pos <= t_q
        new_states = []
        for g in range(A_KV_HEADS):
            gl = slice(g * d, (g + 1) * d)
            chosen = lax.dot_general(sel_all[:, g * QB:(g + 1) * QB], expand, DOT_TN,
                                     preferred_element_type=F32) > 0.5
            bias = jnp.where(chosen & causal, 0.0, NEG)
            sc = _dot_nt(qgs[g], ks_ref[0, pl.ds(k0, tk), gl])
            new_states.append(_flash_update(states[g], sc, bias, vs_ref[0, pl.ds(k0, tk), gl]))
        return tuple(new_states)

    init = (jnp.full((rows, 1), NEG, F32), jnp.zeros((rows, 1), F32), zeros(d))
    sel_states = lax.fori_loop(0, (st + QB - 1) // tk + 1, sel_tile, (init,) * A_KV_HEADS)

    for g in range(A_KV_HEADS):
        gl = slice(g * d, (g + 1) * d)
        qg, o_cmp = qgs[g], o_cmps[g]
        _, l_sel, acc_sel = sel_states[g]
        o_sel = acc_sel / l_sel

        sw = _dot_nt(qg, kw_ref[0, pl.ds(k0w, wk), gl])
        sw = jnp.concatenate([sw[j * QB:(j + 1) * QB] + win_bias for j in range(J)], axis=0)
        pw = jnp.exp(sw - jnp.max(sw, axis=1, keepdims=True))
        o_win = _dot(pw, vw_ref[0, pl.ds(k0w, wk), gl]) / jnp.sum(pw, axis=1, keepdims=True)

        for j in range(J):
            h = g * J + j
            rs = slice(j * QB, (j + 1) * QB)
            o_ref[0, :, h * d:(h + 1) * d] = (gate[:, h:h + 1] * o_cmp[rs]
                                              + gate[:, A_HEADS + h:A_HEADS + h + 1] * o_sel[rs]
                                              + gate[:, 2 * A_HEADS + h:2 * A_HEADS + h + 1] * o_win[rs])


def _covers(n_cmp, n_blocks):
    i = np.arange(n_cmp)[:, None]
    j = np.arange(n_blocks)[None, :]
    return (i * CMP_STRIDE < (j + 1) * SEL_BLOCK) & (i * CMP_STRIDE + CMP_BLOCK > j * SEL_BLOCK)


def nsa_prompt_attend(a_q, a_gate, kc, vc, ks, vs, kw, vw):
    B, T, _ = a_q.shape
    nc = kc.shape[1]
    nb = -(-T // SEL_BLOCK)
    tk = min(8 * Q_BLOCK, T)
    wk = min(WINDOW + Q_BLOCK, T)
    cov = jnp.asarray(_covers(nc, nb).T, BF16)
    tile = lambda w: pl.BlockSpec((1, Q_BLOCK, w), lambda b, i: (b, i, 0))
    full = lambda n, w: pl.BlockSpec((1, n, w), lambda b, i: (b, 0, 0))
    W = A_KV_LANES
    return pl.pallas_call(
        functools.partial(_nsa_prompt_kernel, seq=T, tk=tk, wk=wk),
        grid=(B, T // Q_BLOCK),
        in_specs=[tile(A_WIDTH), tile(3 * A_HEADS), full(nc, W), full(nc, W),
                  full(T, W), full(T, W), full(T, W), full(T, W), _resident(cov.shape)],
        out_specs=tile(A_WIDTH),
        out_shape=jax.ShapeDtypeStruct((B, T, A_WIDTH), F32),
        compiler_params=_cparams("parallel", "arbitrary"), name="nsa_prompt",
    )(a_q, a_gate, kc, vc, ks, vs, kw, vw, cov)


GATHER_PAGES = 16
CHUNKS_PER_PAGE = PAGE_SIZE // CMP_STRIDE
HEAD_ROWS = SUBLANES
PICK_ROWS = 2 * SUBLANES


def channel_major(cache):
    L, n, tok, G, d = cache.shape
    return jnp.transpose(cache, (0, 1, 3, 4, 2)).reshape(L, n, G * d, tok)


def _nsa_sample_cmp_kernel(pt_ref, *refs, past_len):
    del pt_ref
    ng = GATHER_PAGES
    pk_refs, pv_refs = refs[:ng], refs[ng:2 * ng]
    (q_ref, wink_ref, winv_ref, kwn_ref, vwn_ref, wlk_ref, wpek_ref, pek_ref, w2k_ref,
     wlv_ref, wpev_ref, pev_ref, w2v_ref, cov_ref, ocmp_ref, owin_ref, idx_ref,
     hk_scr, hv_scr, xk_scr, xv_scr) = refs[2 * ng:]
    c = pl.program_id(1)
    page = pk_refs[0].shape[3]
    rows_per_step = ng * CHUNKS_PER_PAGE
    base = pl.multiple_of(c * rows_per_step, rows_per_step)
    for page_refs, xs, wl_ref, h_scr in ((pk_refs, xk_scr, wlk_ref, hk_scr), (pv_refs, xv_scr, wlv_ref, hv_scr)):
        for i in range(ng):
            xs[i * page:(i + 1) * page, :] = page_refs[i][0, 0].T
        rows = lambda l, xs=xs: xs[pl.ds(l, rows_per_step, stride=CMP_STRIDE), :]
        h_scr[pl.ds(base, rows_per_step), :] = _compress_halves(rows, wl_ref, rows_per_step)

    @pl.when(c == pl.num_programs(1) - 1)
    def _():
        d, R8 = A_HEAD_DIM, HEAD_ROWS
        kc = _compress_finish(hk_scr[...], wpek_ref, pek_ref, w2k_ref)
        vc = _compress_finish(hv_scr[...], wpev_ref, pev_ref, w2v_ref)
        nc = kc.shape[0]
        nb = cov_ref.shape[0]
        keep = wink_ref.shape[3]
        q_pos = past_len
        blk_end = lax.broadcasted_iota(jnp.int32, (R8, nc), 1) * CMP_STRIDE + (CMP_BLOCK - 1)
        cmp_mask = (blk_end <= q_pos) & (lax.broadcasted_iota(jnp.int32, (R8, nc), 1) < nc - 1)
        real_head = lax.broadcasted_iota(jnp.int32, (R8, nc), 0) < A_GROUP
        col8 = lax.broadcasted_iota(jnp.int32, (nb, R8), 1)
        jj = lax.broadcasted_iota(jnp.int32, (nb, R8), 0)
        cur = q_pos // SEL_BLOCK
        forced = (jj == 0) | (jj == cur) | (jj == cur - 1)
        k_idx = lax.broadcasted_iota(jnp.int32, (R8, keep), 1)
        rel = q_pos - (past_len - keep + k_idx)
        win_mask = (rel >= 0) & (rel < WINDOW)
        rank = None
        for g in range(A_KV_HEADS):
            gl = slice(g * d, (g + 1) * d)
            qg = q_ref[0, g] * A_SCALE
            s = jnp.where(cmp_mask, _dot_nt(qg, kc[:, gl]), NEG)
            e = jnp.exp(s - jnp.max(s, axis=1, keepdims=True))
            p = jnp.where(cmp_mask, e / jnp.sum(e, axis=1, keepdims=True), 0.0)
            ocmp_ref[0, g] = _dot(p, vc[:, gl])
            p_grp = jnp.sum(jnp.where(real_head, p, 0.0), axis=0, keepdims=True)
            score = _dot_f32_rhs_nt(cov_ref[...], jnp.broadcast_to(p_grp, (R8, nc)))
            rank_g = jnp.where(jj <= cur, jnp.where(forced, FORCE, score), NEG)
            rank = rank_g if rank is None else jnp.where(col8 == g, rank_g, rank)
            sw = jnp.where(win_mask, _dot(qg, wink_ref[0, 0, gl, :]), NEG)
            s_new = jnp.sum(qg * kwn_ref[0, :, gl], axis=1, keepdims=True)
            m = jnp.maximum(jnp.max(sw, axis=1, keepdims=True), s_new)
            ew = jnp.where(win_mask, jnp.exp(sw - m), 0.0)
            e_new = jnp.exp(s_new - m)
            owin_ref[0, g] = ((_dot_nt(ew, winv_ref[0, 0, gl, :]) + e_new * vwn_ref[0, :, gl])
                              / (jnp.sum(ew, axis=1, keepdims=True) + e_new))
        block = jj.astype(F32)
        out_row = lax.broadcasted_iota(jnp.int32, idx_ref.shape[1:], 0)
        picks = jnp.zeros(idx_ref.shape[1:], jnp.int32)
        for it in range(N_SEL - 1):
            m = jnp.max(rank, axis=0, keepdims=True)
            first = jnp.min(jnp.where(rank == m, block, float(nb)), axis=0, keepdims=True)
            picks = jnp.where(out_row == it, first.astype(jnp.int32), picks)
            rank = jnp.where(block == first, LOWEST, rank)
        idx_ref[0] = picks


def nsa_sample_cmp(page_table, pool_k, pool_v, q_hm, win_k, win_v, layer, kw_new, vw_new, cwk, cwv, past_len):
    B, n_pages = page_table.shape
    _, n_pool, W, page = pool_k.shape
    ng = GATHER_PAGES
    assert n_pages % ng == 0
    nc = n_pages * CHUNKS_PER_PAGE
    nb = past_len // SEL_BLOCK
    cov = jnp.asarray(_covers(nc, nb).T, BF16)
    keep = win_k.shape[3]
    page_spec = lambda i: pl.BlockSpec((1, 1, W, page), lambda b, c, pt: (layer, pt[b, c * ng + i], 0, 0))
    perb = lambda *s: pl.BlockSpec((1,) + s, lambda b, c, pt: (b,) + (0,) * len(s))
    win_spec = pl.BlockSpec((1, 1, W, keep), lambda b, c, pt: (layer, b, 0, 0))
    const = lambda a: pl.BlockSpec(a.shape, lambda b, c, pt: (0,) * a.ndim)
    consts = [*cwk, *cwv, cov]
    hw = cwk[0].shape[1]
    hm = jax.ShapeDtypeStruct((B, A_KV_HEADS, HEAD_ROWS, A_HEAD_DIM), F32)
    grid_spec = pltpu.PrefetchScalarGridSpec(
        num_scalar_prefetch=1, grid=(B, n_pages // ng),
        in_specs=[page_spec(i) for i in range(ng)] * 2
                 + [perb(A_KV_HEADS, HEAD_ROWS, A_HEAD_DIM), win_spec, win_spec, perb(1, W), perb(1, W)]
                 + [const(a) for a in consts],
        out_specs=[perb(A_KV_HEADS, HEAD_ROWS, A_HEAD_DIM), perb(A_KV_HEADS, HEAD_ROWS, A_HEAD_DIM),
                   perb(PICK_ROWS, HEAD_ROWS)],
        scratch_shapes=[pltpu.VMEM((nc, hw), F32), pltpu.VMEM((nc, hw), F32),
                        pltpu.VMEM((ng * page, W), F32), pltpu.VMEM((ng * page, W), F32)])
    return pl.pallas_call(
        functools.partial(_nsa_sample_cmp_kernel, past_len=past_len), grid_spec=grid_spec,
        out_shape=[hm, hm, jax.ShapeDtypeStruct((B, PICK_ROWS, HEAD_ROWS), jnp.int32)],
        compiler_params=_cparams("parallel", "arbitrary"), name="nsa_sample_cmp",
    )(page_table, *([pool_k] * ng), *([pool_v] * ng), q_hm, win_k, win_v, kw_new, vw_new, *consts)


SEL_PICKS_PER_STEP = 5


def _nsa_sample_sel_kernel(pt_ref, pk_ref, *refs, npk):
    del pt_ref
    P = SEL_PICKS_PER_STEP
    n_pages = A_KV_HEADS * P
    k_refs, v_refs = refs[:n_pages], refs[n_pages:2 * n_pages]
    q_ref, gate_ref, ksn_ref, vsn_ref, ocmp_ref, owin_ref, o_ref, m_scr, l_scr, acc_scr = refs[2 * n_pages:]
    b = pl.program_id(0)
    c = pl.program_id(1)
    d = A_HEAD_DIM
    page = k_refs[0].shape[3]
    tok_blk = lax.broadcasted_iota(jnp.int32, (HEAD_ROWS, page), 1) // SEL_BLOCK
    for g in range(A_KV_HEADS):
        gl = slice(g * d, (g + 1) * d)
        qg = q_ref[0, g] * A_SCALE

        @pl.when(c == 0)
        def _():
            m_scr[g] = jnp.sum(qg * ksn_ref[0, :, gl], axis=1, keepdims=True)
            l_scr[g] = jnp.ones((HEAD_ROWS, 1), F32)
            acc_scr[g] = jnp.broadcast_to(vsn_ref[0, :, gl], (HEAD_ROWS, d))

        chosen = [tok_blk == pk_ref[b, g * npk + c * P + i] % (page // SEL_BLOCK) for i in range(P)]
        s = [jnp.where(chosen[i], _dot(qg, k_refs[g * P + i][0, 0, gl, :]), NEG) for i in range(P)]
        m_old = m_scr[g]
        m = m_old
        for i in range(P):
            m = jnp.maximum(m, jnp.max(s[i], axis=1, keepdims=True))
        alpha = jnp.exp(m_old - m)
        l = alpha * l_scr[g]
        acc = alpha * acc_scr[g]
        for i in range(P):
            p = jnp.where(chosen[i], jnp.exp(s[i] - m), 0.0)
            l = l + jnp.sum(p, axis=1, keepdims=True)
            acc = acc + _dot_nt(p, v_refs[g * P + i][0, 0, gl, :])
        m_scr[g] = m
        l_scr[g] = l
        acc_scr[g] = acc

        @pl.when(c == pl.num_programs(1) - 1)
        def _():
            gate = _sigmoid(gate_ref[0, g])
            o_ref[0, g] = (gate[:, 0:1] * ocmp_ref[0, g] + gate[:, 1:2] * (acc / l)
                           + gate[:, 2:3] * owin_ref[0, g])


def nsa_sample_sel(page_table, picks, pool_k, pool_v, layer, q_hm, gate_hm, ks_new, vs_new, o_cmp, o_win):
    B = page_table.shape[0]
    _, n_pool, W, page = pool_k.shape
    per_page = page // SEL_BLOCK
    npk = picks.shape[2]
    P = SEL_PICKS_PER_STEP
    assert npk % P == 0
    picks = picks.reshape(B, A_KV_HEADS * npk)

    def blk(g, i):
        def index(b, c, pt, pk):
            return (layer, pt[b, pk[b, g * npk + c * P + i] // per_page], 0, 0)
        return pl.BlockSpec((1, 1, W, page), index)

    pages = [blk(g, i) for g in range(A_KV_HEADS) for i in range(P)]
    perb = lambda *s: pl.BlockSpec((1,) + s, lambda b, c, pt, pk: (b,) + (0,) * len(s))
    hm_spec = perb(A_KV_HEADS, HEAD_ROWS, A_HEAD_DIM)
    grid_spec = pltpu.PrefetchScalarGridSpec(
        num_scalar_prefetch=2, grid=(B, npk // P),
        in_specs=pages * 2 + [hm_spec, perb(A_KV_HEADS, HEAD_ROWS, 3), perb(1, W), perb(1, W), hm_spec, hm_spec],
        out_specs=hm_spec,
        scratch_shapes=[pltpu.VMEM((A_KV_HEADS, HEAD_ROWS, 1), F32), pltpu.VMEM((A_KV_HEADS, HEAD_ROWS, 1), F32),
                        pltpu.VMEM((A_KV_HEADS, HEAD_ROWS, A_HEAD_DIM), F32)])
    return pl.pallas_call(
        functools.partial(_nsa_sample_sel_kernel, npk=npk), grid_spec=grid_spec,
        out_shape=jax.ShapeDtypeStruct((B, A_KV_HEADS, HEAD_ROWS, A_HEAD_DIM), F32),
        compiler_params=_cparams("parallel", "arbitrary"), name="nsa_sample_sel",
    )(page_table, picks, *([pool_k] * len(pages)), *([pool_v] * len(pages)), q_hm, gate_hm, ks_new, vs_new,
      o_cmp, o_win)


def nsa_sample(a_q, a_gate, new_rows, pools, win_k, win_v, layer, page_table, cwk, cwv):
    B = a_q.shape[0]
    past_len = page_table.shape[1] * PAGE_SIZE
    _, _, ksn, vsn, kwn, vwn = new_rows
    pool_kc, pool_vc, pool_ks, pool_vs = pools
    padj = ((0, 0), (0, 0), (0, HEAD_ROWS - A_GROUP), (0, 0))
    q_hm = jnp.pad(a_q.reshape(B, A_KV_HEADS, A_GROUP, A_HEAD_DIM), padj)
    gate_hm = jnp.pad(jnp.transpose(a_gate.reshape(B, 3, A_KV_HEADS, A_GROUP), (0, 2, 3, 1)), padj)
    o_cmp, o_win, picks = nsa_sample_cmp(page_table, pool_kc, pool_vc, q_hm, win_k, win_v, layer, kwn, vwn,
                                         cwk, cwv, past_len)
    picks = jnp.swapaxes(picks, 1, 2)[:, :A_KV_HEADS, :N_SEL - 1]
    o = nsa_sample_sel(page_table, picks, pool_ks, pool_vs, layer, q_hm, gate_hm, ksn, vsn, o_cmp, o_win)
    return o[:, :, :A_GROUP].reshape(B, 1, A_WIDTH)


PROJ_GROUPS = (2 * M_WIDTH, M_WIDTH, M_WIDTH, 2 * M_HEADS, A_WIDTH) + (A_KV_LANES,) * 6 + (
    3 * A_HEADS, R_COLS, 3 * D_MODEL)
PROMPT_ROW_TILE = 256
MLSTM_CHUNK = 128
RWKV_CHUNK = 64


def _layer_params(l, w_in, w_branch_m, w_branch_a, w_branch_r, w_out, w_gate_up, w_down,
                  nsa_cmp_pe, nsa_cmp_w1, nsa_cmp_w2):
    offs = np.cumsum((0,) + PROJ_GROUPS)
    assert offs[-1] == w_in.shape[2]
    wi = w_in[l].astype(BF16)
    return {
        'proj': [wi[:, a:b] for a, b in zip(offs[:-1], offs[1:])],
        'wm': w_branch_m[l].astype(BF16), 'wa': w_branch_a[l].astype(BF16), 'wr': w_branch_r[l].astype(BF16),
        'wo': w_out[l].astype(BF16), 'wgu': w_gate_up[l].astype(BF16), 'wd': w_down[l].astype(BF16),
        'cwk': compress_weights(nsa_cmp_pe[l, 0], nsa_cmp_w1[l, 0], nsa_cmp_w2[l, 0]),
        'cwv': compress_weights(nsa_cmp_pe[l, 1], nsa_cmp_w1[l, 1], nsa_cmp_w2[l, 1]),
    }


def _trunk_layer(x, lw, lp, nsa_fn, m_state, r_state, tm, chunk, rwkv_chunk, gf, final_norm):
    B, T, D = x.shape
    x2d = x.reshape(B * T, D)
    outs = norm_proj(x2d, lp['norm1_g'], lw['proj'], tm)
    (m_qk, m_v, m_o, m_if, a_q), new_rows, (a_gate, r_cols, merge) = outs[:5], outs[5:11], outs[11:]
    b3 = lambda a: a.reshape(B, T, a.shape[-1])
    new_rows = [b3(r) for r in new_rows]
    y_m, conv_n, c_n, n_n, m_n = mlstm(b3(m_qk), b3(m_v), b3(m_o), b3(m_if), *m_state,
                                        lp['mlstm_conv_w'], lp['mlstm_conv_b'], lp['mlstm_if_bias'], chunk)
    y_a = nsa_fn(b3(a_q), b3(a_gate), new_rows)
    r_cols = b3(r_cols)
    y_r, s_n = rwkv(r_cols, r_state[0], r_state[1], lp, rwkv_chunk)
    x_new = merge_ffn(x2d, y_m.reshape(B * T, -1), y_a.reshape(B * T, -1), y_r.reshape(B * T, -1), merge,
                      lw['wm'], lw['wa'], lw['wr'], lw['wo'], lp['norm2_g'], lw['wgu'], lw['wd'], gf,
                      tm, final_norm)
    return x_new.reshape(B, T, D), new_rows, (conv_n, c_n, n_n, m_n), (r_cols[:, -1:], s_n)


def kernel(x_prompt, x_sample, cache_cmp_k, cache_cmp_v, cache_slc_k, cache_slc_v, cache_win_k, cache_win_v, state_mlstm_conv, state_mlstm_C, state_mlstm_n, state_mlstm_m, state_rwkv_shift, state_rwkv, page_table, norm1_g, w_in, mlstm_if_bias, mlstm_conv_w, mlstm_conv_b, nsa_cmp_pe, nsa_cmp_w1, nsa_cmp_w2, rwkv_mu, rwkv_w0, rwkv_w2, rwkv_a0, rwkv_a2, rwkv_g2, rwkv_k_k, rwkv_k_a, rwkv_r_k, rwkv_ln_g, rwkv_ln_b, w_branch_m, w_branch_a, w_branch_r, w_out, norm2_g, w_gate_up, w_down, final_norm_g):
    bp, tp, _ = x_prompt.shape
    bs, ts, _ = x_sample.shape
    assert ts == 1
    depth = w_in.shape[0]
    kvshape = lambda a: a.reshape(a.shape[0], a.shape[1], A_KV_HEADS, A_HEAD_DIM)
    xp, xs = x_prompt, x_sample
    pools = [channel_major(c) for c in (cache_cmp_k, cache_cmp_v, cache_slc_k, cache_slc_v)]
    win_k, win_v = channel_major(cache_win_k), channel_major(cache_win_v)
    states_p, states_s = [], []
    for l in range(depth):
        lw = _layer_params(l, w_in, w_branch_m, w_branch_a, w_branch_r, w_out, w_gate_up, w_down,
                           nsa_cmp_pe, nsa_cmp_w1, nsa_cmp_w2)
        lp = {'norm1_g': norm1_g[l], 'mlstm_if_bias': mlstm_if_bias[l], 'mlstm_conv_w': mlstm_conv_w[l],
              'mlstm_conv_b': mlstm_conv_b[l], 'rwkv_mu': rwkv_mu[l], 'rwkv_w0': rwkv_w0[l],
              'rwkv_w2': rwkv_w2[l], 'rwkv_a0': rwkv_a0[l], 'rwkv_a2': rwkv_a2[l], 'rwkv_g2': rwkv_g2[l],
              'rwkv_k_k': rwkv_k_k[l], 'rwkv_k_a': rwkv_k_a[l], 'rwkv_r_k': rwkv_r_k[l],
              'rwkv_ln_g': rwkv_ln_g[l], 'rwkv_ln_b': rwkv_ln_b[l], 'norm2_g': norm2_g[l]}
        last = l == depth - 1

        def nsa_p(a_q, a_gate, rows):
            kc = compress_rows(rows[0], lw['cwk'])
            vc = compress_rows(rows[1], lw['cwv'])
            return nsa_prompt_attend(a_q, a_gate, kc, vc, *rows[2:])

        m_zero = (jnp.zeros((bp, M_CONV - 1, 2 * M_WIDTH), F32),
                  jnp.zeros((bp, M_HEADS, M_HEAD_DIM, M_HEAD_DIM), F32),
                  jnp.zeros((bp, M_HEADS, M_HEAD_DIM), F32), jnp.zeros((bp, M_HEADS), F32))
        r_zero = (jnp.zeros((bp, 1, R_COLS), F32), jnp.zeros((bp, R_HEADS, R_HEAD_DIM, R_HEAD_DIM), F32))
        xp, rows, m_new, r_new = _trunk_layer(xp, lw, lp, nsa_p, m_zero, r_zero, PROMPT_ROW_TILE,
                                              MLSTM_CHUNK, RWKV_CHUNK, final_norm_g, last)
        keep = min(WINDOW, tp)
        states_p.append(tuple(kvshape(r) for r in rows[:4])
                        + (kvshape(rows[4][:, tp - keep:]), kvshape(rows[5][:, tp - keep:])) + m_new + r_new)

        def nsa_s(a_q, a_gate, rows, l=l):
            return nsa_sample(a_q, a_gate, rows, pools, win_k, win_v, l, page_table, lw['cwk'], lw['cwv'])

        xs, rows, m_new, r_new = _trunk_layer(
            xs, lw, lp, nsa_s,
            (state_mlstm_conv[l], state_mlstm_C[l], state_mlstm_n[l], state_mlstm_m[l]),
            (state_rwkv_shift[l], state_rwkv[l]), bs * ts, SUBLANES, SUBLANES, final_norm_g, last)
        keep = win_k.shape[3]

        def win_new(win, new_row):
            w = jnp.concatenate([win[l], jnp.swapaxes(new_row, 1, 2)], axis=2)[:, :, -keep:]
            return jnp.transpose(w.reshape(bs, A_KV_HEADS, A_HEAD_DIM, keep), (0, 3, 1, 2))

        states_s.append(tuple(kvshape(r) for r in rows[:4])
                        + (win_new(win_k, rows[4]), win_new(win_v, rows[5])) + m_new + r_new)

    stack = lambda states: [jnp.stack(a) for a in zip(*states)]
    return (xp, xs, *stack(states_p), *stack(states_s))
```

```python
import functools
import math

import numpy as np
import jax
import jax.numpy as jnp
from jax import lax
from jax.experimental import pallas as pl
from jax.experimental.pallas import tpu as pltpu

F32 = jnp.float32
BF16 = jnp.bfloat16

D_MODEL = 1024
DEPTH = 2
PAGE_SIZE = 128
M_HEADS = 4
M_HEAD_DIM = 128
M_WIDTH = M_HEADS * M_HEAD_DIM
M_CONV = 4
A_HEADS = 8
A_KV_HEADS = 2
A_HEAD_DIM = 64
A_WIDTH = A_HEADS * A_HEAD_DIM
A_GROUP = A_HEADS // A_KV_HEADS
A_SCALE = A_HEAD_DIM ** -0.5
A_KV_LANES = A_KV_HEADS * A_HEAD_DIM
CMP_STRIDE = 16
CMP_BLOCK = 2 * CMP_STRIDE
CMP_HIDDEN = A_HEAD_DIM
SEL_BLOCK = 64
N_SEL = 16
WINDOW = 512
Q_BLOCK = 128
R_HEADS = 8
R_HEAD_DIM = 64
R_WIDTH = R_HEADS * R_HEAD_DIM
R_DECAY_RANK = 64
R_A_RANK = 64
R_GATE_RANK = 128
R_COLS = 3 * R_WIDTH + R_DECAY_RANK + R_A_RANK + R_GATE_RANK
LN_X_EPS = 64e-5
D_FF = ((-(-8 * D_MODEL // 3)) + 255) // 256 * 256
IN_SPLITS = (2 * M_WIDTH, M_WIDTH, M_WIDTH, 2 * M_HEADS,
             A_WIDTH, 6 * A_KV_HEADS * A_HEAD_DIM, 3 * A_HEADS,
             R_COLS, 3 * D_MODEL)
RMS_EPS = 1e-6
NEG = -1e30
FORCE = 1e9
LOWEST = -3e38

VMEM_LIMIT_BYTES = 56 * 1024 * 1024
SUBLANES = 8
LANES = 128


def _cparams(*sem):
    return pltpu.CompilerParams(dimension_semantics=sem, vmem_limit_bytes=VMEM_LIMIT_BYTES)


def _resident(shape):
    nd = len(shape)
    return pl.BlockSpec(shape, lambda *_: (0,) * nd, pipeline_mode=pl.Buffered(1))


def _dot(a, b):
    return jnp.dot(a.astype(BF16), b.astype(BF16), preferred_element_type=F32)


def _dot_nt(a, b):
    return lax.dot_general(a.astype(BF16), b.astype(BF16), (((1,), (1,)), ((), ())),
                           preferred_element_type=F32)


def _split3(x):
    hi = x.astype(BF16)
    r = x - hi.astype(F32)
    mid = r.astype(BF16)
    lo = (r - mid.astype(F32)).astype(BF16)
    return hi, mid, lo


def _dot_f32_rhs(a_bf16, x):
    hi, mid, lo = _split3(x)
    d = lambda p: jnp.dot(a_bf16, p, preferred_element_type=F32)
    return d(hi) + d(mid) + d(lo)


def _dot_f32_rhs_nt(a_bf16, x):
    hi, mid, lo = _split3(x)
    d = lambda p: lax.dot_general(a_bf16, p, (((1,), (1,)), ((), ())), preferred_element_type=F32)
    return d(hi) + d(mid) + d(lo)


def _transpose_f32(x, eye_bf16):
    hi, mid, lo = _split3(x)
    d = lambda p: lax.dot_general(eye_bf16, p, (((1,), (1,)), ((), ())), preferred_element_type=F32)
    return d(hi) + d(mid) + d(lo)


def _eye(n, dtype=BF16):
    r = lax.broadcasted_iota(jnp.int32, (n, n), 0)
    c = lax.broadcasted_iota(jnp.int32, (n, n), 1)
    return (r == c).astype(dtype)


def _sigmoid(x):
    return 1.0 / (1.0 + jnp.exp(-x))


def _silu(x):
    return x * _sigmoid(x)


def _softplus(x):
    return jnp.maximum(x, 0.0) + jnp.log(1.0 + jnp.exp(-jnp.abs(x)))


def _rmsnorm(x, g):
    return x * lax.rsqrt(jnp.mean(x * x, axis=-1, keepdims=True) + RMS_EPS) * g


def _norm_proj_kernel(x_ref, g_ref, *refs):
    n = len(refs) // 2
    hb = _rmsnorm(x_ref[...], g_ref[...]).astype(BF16)
    for w_ref, o_ref in zip(refs[:n], refs[n:]):
        o_ref[...] = jnp.dot(hb, w_ref[...], preferred_element_type=F32)


def norm_proj(x2d, g, weights, tm):
    m, d = x2d.shape
    assert m % tm == 0
    row = lambda i: (i, 0)
    in_specs = [pl.BlockSpec((tm, d), row), _resident((1, d))]
    in_specs += [_resident(w.shape) for w in weights]
    out_specs = [pl.BlockSpec((tm, w.shape[1]), row) for w in weights]
    out_shape = [jax.ShapeDtypeStruct((m, w.shape[1]), F32) for w in weights]
    return pl.pallas_call(
        _norm_proj_kernel, grid=(m // tm,), in_specs=in_specs, out_specs=out_specs,
        out_shape=out_shape, compiler_params=_cparams("parallel"), name="norm_proj",
    )(x2d, g.reshape(1, d), *weights)


def _merge_ffn_kernel(x_ref, ym_ref, ya_ref, yr_ref, mg_ref, wm_ref, wa_ref, wr_ref, wo_ref,
                      g2_ref, wgu_ref, wd_ref, gf_ref, o_ref, *, final_norm):
    mg = mg_ref[...]
    merged = (_sigmoid(mg[:, :D_MODEL]) * _dot(ym_ref[...], wm_ref[...])
              + _sigmoid(mg[:, D_MODEL:2 * D_MODEL]) * _dot(ya_ref[...], wa_ref[...])
              + _sigmoid(mg[:, 2 * D_MODEL:]) * _dot(yr_ref[...], wr_ref[...]))
    x1 = x_ref[...] + _dot(merged, wo_ref[...])
    gu = _dot(_rmsnorm(x1, g2_ref[...]), wgu_ref[...])
    x2 = x1 + _dot(_silu(gu[:, :D_FF]) * gu[:, D_FF:], wd_ref[...])
    o_ref[...] = _rmsnorm(x2, gf_ref[...]) if final_norm else x2


def merge_ffn(x2d, ym, ya, yr, mg, wm, wa, wr, wo, g2, wgu, wd, gf, tm, final_norm):
    m, d = x2d.shape
    row = lambda i: (i, 0)
    acts = [x2d, ym, ya, yr, mg]
    consts = [wm, wa, wr, wo, g2.reshape(1, d), wgu, wd, gf.reshape(1, d)]
    in_specs = [pl.BlockSpec((tm, a.shape[1]), row) for a in acts]
    in_specs += [_resident(c.shape) for c in consts]
    return pl.pallas_call(
        functools.partial(_merge_ffn_kernel, final_norm=final_norm),
        grid=(m // tm,), in_specs=in_specs, out_specs=pl.BlockSpec((tm, d), row),
        out_shape=jax.ShapeDtypeStruct((m, d), F32), compiler_params=_cparams("parallel"),
        name="merge_ffn",
    )(*acts, *consts)


def _mlstm_kernel(qk_ref, v_ref, o_ref, if_ref, conv0_ref, c0_ref, n0_ref, m0_ref,
                  cw_ref, cb_ref, ifb_ref,
                  y_ref, conv_out_ref, c_out_ref, n_out_ref, m_out_ref,
                  cbuf, c_scr, n_scr, m_scr, *, chunk, n_last):
    c = pl.program_id(1)
    nc = pl.num_programs(1)
    L = chunk
    pad = SUBLANES

    @pl.when(c == 0)
    def _():
        cbuf[0:pad, :] = conv0_ref[0]
        c_scr[...] = c0_ref[0]
        n_scr[...] = n0_ref[0]
        m_scr[...] = m0_ref[0]

    cbuf[pad:pad + L, :] = qk_ref[0]
    acc = cb_ref[...]
    for j in range(M_CONV):
        acc = acc + cbuf[pad - (M_CONV - 1) + j: pad - (M_CONV - 1) + j + L, :] * cw_ref[j:j + 1, :]
    qk = _silu(acc)

    @pl.when(c == nc - 1)
    def _():
        conv_out_ref[0] = cbuf[n_last:n_last + pad, :]

    cbuf[0:pad, :] = cbuf[L:L + pad, :]

    gates = if_ref[0] + ifb_ref[...]
    lane8 = lax.broadcasted_iota(jnp.int32, (L, 2 * M_HEADS), 1)
    lf_all = -_softplus(-gates)
    lif = jnp.where(lane8 < M_HEADS, gates, lf_all)
    row_id = lax.broadcasted_iota(jnp.int32, (L, 2 * M_HEADS), 0)
    is_last_chunk = c == nc - 1
    if n_last < L:
        dead = jnp.logical_and(is_last_chunk, row_id >= n_last)
        lif = jnp.where(dead, jnp.where(lane8 < M_HEADS, NEG, 0.0), lif)
    t_i = lax.broadcasted_iota(jnp.int32, (L, L), 0)
    s_i = lax.broadcasted_iota(jnp.int32, (L, L), 1)
    causal = s_i <= t_i
    tri = causal.astype(BF16)
    csum = _dot_f32_rhs(tri, lif)
    lib = jnp.where(lane8 < M_HEADS, lif, csum)
    lib_t = _transpose_f32(lib, _eye(2 * M_HEADS))
    eye_h = _eye(M_HEAD_DIM)
    last = n_last - 1

    heads = range(M_HEADS)
    sl = [slice(h * M_HEAD_DIM, (h + 1) * M_HEAD_DIM) for h in heads]
    q = [qk[:, sl[h]] for h in heads]
    k = [qk[:, M_WIDTH + h * M_HEAD_DIM: M_WIDTH + (h + 1) * M_HEAD_DIM] * (M_HEAD_DIM ** -0.5) for h in heads]
    v = [v_ref[0, :, sl[h]] for h in heads]
    li_col = [lib[:, h:h + 1] for h in heads]
    b_col = [lib[:, M_HEADS + h:M_HEADS + h + 1] for h in heads]
    m_prev = [m_scr[0:1, h:h + 1] for h in heads]
    c_old = [c_scr[h] for h in heads]
    n_old = [n_scr[h:h + 1, :] for h in heads]
    s_qk = [_dot_nt(q[h], k[h]) for h in heads]
    q_c = [_dot(q[h], c_old[h]) for h in heads]
    dmat = [jnp.where(causal, b_col[h] - lib_t[M_HEADS + h:M_HEADS + h + 1, :] + lib_t[h:h + 1, :], NEG)
            for h in heads]
    inter = [b_col[h] + m_prev[h] for h in heads]
    m_t = [jnp.maximum(inter[h], jnp.max(dmat[h], axis=1, keepdims=True)) for h in heads]
    att = [jnp.exp(dmat[h] - m_t[h]) * s_qk[h] for h in heads]
    w_inter = [jnp.exp(inter[h] - m_t[h]) for h in heads]
    num = [_dot(att[h], v[h]) + w_inter[h] * q_c[h] for h in heads]
    den = [jnp.sum(att[h], axis=1, keepdims=True) + w_inter[h] * jnp.sum(q[h] * n_old[h], axis=1, keepdims=True)
           for h in heads]
    for h in heads:
        hh = num[h] / jnp.maximum(jnp.abs(den[h]), jnp.exp(-m_t[h]))
        y_ref[0, :, sl[h]] = _sigmoid(o_ref[0, :, sl[h]]) * hh

    m_new = [m_t[h][last:last + 1, :] for h in heads]
    b_last = [b_col[h][last:last + 1, :] for h in heads]
    scale = [jnp.exp(b_last[h] + m_prev[h] - m_new[h]) for h in heads]
    kw = [k[h] * jnp.exp(b_last[h] - b_col[h] + li_col[h] - m_new[h]) for h in heads]
    kw_t = [lax.dot_general(eye_h, kw[h].astype(BF16), DOT_NT, preferred_element_type=F32)
            for h in heads]
    for h in heads:
        c_scr[h] = scale[h] * c_old[h] + _dot(kw_t[h], v[h])
        n_scr[h:h + 1, :] = scale[h] * n_old[h] + jnp.sum(kw[h], axis=0, keepdims=True)
        m_scr[0:1, h:h + 1] = m_new[h]

    @pl.when(is_last_chunk)
    def _():
        c_out_ref[0] = c_scr[...]
        n_out_ref[0] = n_scr[...]
        m_out_ref[0] = m_scr[...]


def mlstm(m_qk, m_v, m_o, m_if, conv0, c0, n0, m0, conv_w, conv_b, if_bias, chunk):
    B, T, _ = m_qk.shape
    nc = -(-T // chunk)
    tp = nc * chunk
    n_last = T - (nc - 1) * chunk
    if tp != T:
        padt = lambda a: jnp.pad(a, ((0, 0), (0, tp - T), (0, 0)))
        m_qk, m_v, m_o, m_if = padt(m_qk), padt(m_v), padt(m_o), padt(m_if)
    conv0p = jnp.pad(conv0, ((0, 0), (SUBLANES - (M_CONV - 1), 0), (0, 0)))
    n0p = jnp.pad(n0, ((0, 0), (0, SUBLANES - M_HEADS), (0, 0)))
    m0p = jnp.pad(m0, ((0, 0), (0, LANES - M_HEADS))).reshape(B, 1, LANES)
    cw = jnp.pad(conv_w, ((0, SUBLANES - M_CONV), (0, 0)))
    tile = lambda w: pl.BlockSpec((1, chunk, w), lambda b, c: (b, c, 0))
    perb = lambda *s: pl.BlockSpec((1,) + s, lambda b, c: (b,) + (0,) * len(s))
    outs = pl.pallas_call(
        functools.partial(_mlstm_kernel, chunk=chunk, n_last=n_last),
        grid=(B, nc),
        in_specs=[tile(2 * M_WIDTH), tile(M_WIDTH), tile(M_WIDTH), tile(2 * M_HEADS),
                  perb(SUBLANES, 2 * M_WIDTH), perb(M_HEADS, M_HEAD_DIM, M_HEAD_DIM),
                  perb(SUBLANES, M_HEAD_DIM), perb(1, LANES),
                  _resident((SUBLANES, 2 * M_WIDTH)), _resident((1, 2 * M_WIDTH)),
                  _resident((1, 2 * M_HEADS))],
        out_specs=[tile(M_WIDTH), perb(SUBLANES, 2 * M_WIDTH),
                   perb(M_HEADS, M_HEAD_DIM, M_HEAD_DIM), perb(SUBLANES, M_HEAD_DIM), perb(1, LANES)],
        out_shape=[jax.ShapeDtypeStruct((B, tp, M_WIDTH), F32),
                   jax.ShapeDtypeStruct((B, SUBLANES, 2 * M_WIDTH), F32),
                   jax.ShapeDtypeStruct((B, M_HEADS, M_HEAD_DIM, M_HEAD_DIM), F32),
                   jax.ShapeDtypeStruct((B, SUBLANES, M_HEAD_DIM), F32),
                   jax.ShapeDtypeStruct((B, 1, LANES), F32)],
        scratch_shapes=[pltpu.VMEM((chunk + SUBLANES, 2 * M_WIDTH), F32),
                        pltpu.VMEM((M_HEADS, M_HEAD_DIM, M_HEAD_DIM), F32),
                        pltpu.VMEM((SUBLANES, M_HEAD_DIM), F32),
                        pltpu.VMEM((1, LANES), F32)],
        compiler_params=_cparams("parallel", "arbitrary"), name="mlstm",
    )(m_qk, m_v, m_o, m_if, conv0p, c0, n0p, m0p, cw, conv_b.reshape(1, -1), if_bias.reshape(1, -1))
    y, conv_o, c_o, n_o, m_o_ = outs
    return (y[:, :T], conv_o[:, SUBLANES - (M_CONV - 1):], c_o, n_o[:, :M_HEADS], m_o_[:, 0, :M_HEADS])


def _rwkv_project(cols, prev, mu_ref, w0_ref, w2_ref, a0_ref, a2_ref, g2_ref, kk_ref, ka_ref, rk_ref):
    mixed = cols + (prev - cols) * mu_ref[...]
    W = R_WIDTH
    r = mixed[:, :W]
    k = mixed[:, W:2 * W]
    v = mixed[:, 2 * W:3 * W]
    o = 3 * W
    wd = mixed[:, o:o + R_DECAY_RANK]
    ad = mixed[:, o + R_DECAY_RANK:o + R_DECAY_RANK + R_A_RANK]
    gd = mixed[:, o + R_DECAY_RANK + R_A_RANK:]
    w = -_softplus(-(w0_ref[...] + _dot(jnp.tanh(wd), w2_ref[...]))) - 0.5
    a_gate = _sigmoid(a0_ref[...] + _dot(ad, a2_ref[...]))
    g = _dot(_sigmoid(gd), g2_ref[...])
    kk = k * kk_ref[...]
    k2 = k * (1.0 + (a_gate - 1.0) * ka_ref[...])
    rkk = r * k2 * rk_ref[...]
    a_h, b_h, bonus_h = [], [], []
    for h in range(R_HEADS):
        sl = slice(h * R_HEAD_DIM, (h + 1) * R_HEAD_DIM)
        kkh = kk[:, sl]
        kkh = kkh * lax.rsqrt(jnp.maximum(jnp.sum(kkh * kkh, axis=1, keepdims=True), 1e-24))
        a_h.append(-kkh)
        b_h.append(kkh * a_gate[:, sl])
        bonus_h.append(jnp.sum(rkk[:, sl], axis=1, keepdims=True) * v[:, sl])
    return r, -jnp.exp(w), k2, v, g, a_h, b_h, bonus_h


RWKV_SEQS_PER_STEP = 4
DOT_NN = (((1,), (0,)), ((), ()))
DOT_NT = (((1,), (1,)), ((), ()))
DOT_TN = (((0,), (0,)), ((), ()))


def _dot_x3(x, y, dims):
    xh = x.astype(BF16)
    xl = (x - xh.astype(F32)).astype(BF16)
    yh = y.astype(BF16)
    yl = (y - yh.astype(F32)).astype(BF16)
    d = lambda p, q: lax.dot_general(p, q, dims, preferred_element_type=F32)
    return d(xh, yh) + d(xh, yl) + d(xl, yh)


def _dot_tn(a, b):
    return lax.dot_general(a.astype(BF16), b.astype(BF16), DOT_TN, preferred_element_type=F32)


def _rwkv_chunk_kernel(cols_ref, shift0_ref, mu_ref, w0_ref, w2_ref, a0_ref, a2_ref, g2_ref, kk_ref, ka_ref,
                       rk_ref, lng_ref, lnb_ref, s0_ref, o_ref, s_out_ref, s_scr, carry, *, chunk, n_valid):
    t = pl.program_id(1)
    nt = pl.num_programs(1)
    L, N = chunk, R_HEAD_DIM

    @pl.when(t == 0)
    def _():
        s_scr[...] = s0_ref[...]
        carry[...] = shift0_ref[...]

    ti = lax.broadcasted_iota(jnp.int32, (L, L), 0)
    si = lax.broadcasted_iota(jnp.int32, (L, L), 1)
    incl = si <= ti
    strict = si < ti
    eye = (si == ti).astype(F32)
    tri = incl.astype(BF16)
    first_row = lax.broadcasted_iota(jnp.int32, (L, R_COLS), 0) == 0
    n_seq = cols_ref.shape[0]
    ah, rh, bt, kt, vh, bb, kb, gm, gate, bonus = [], [], [], [], [], [], [], [], [], []
    for i in range(n_seq):
        cols = cols_ref[i]
        prev = jnp.where(first_row, carry[i], pltpu.roll(cols, 1, 0))
        carry[i] = cols[L - 1:L, :]
        r, lw, k, v, g, a_h, b_h, bonus_h = _rwkv_project(cols, prev, mu_ref, w0_ref, w2_ref, a0_ref, a2_ref,
                                                          g2_ref, kk_ref, ka_ref, rk_ref)
        bonus += bonus_h
        if n_valid % L:
            live = (t * L + lax.broadcasted_iota(jnp.int32, (L, R_WIDTH), 0)) < n_valid
            lw, k, v = (jnp.where(live, z, 0.0) for z in (lw, k, v))
            a_h = [jnp.where(live[:, :N], z, 0.0) for z in a_h]
            b_h = [jnp.where(live[:, :N], z, 0.0) for z in b_h]
        c = _dot_f32_rhs(tri, lw)
        c_end = c[L - 1:L, :]
        e_prev = jnp.exp(c - lw)
        r_hat = r * jnp.exp(c)
        e_neg = jnp.exp(-c)
        e_end = jnp.exp(c_end - c)
        gam = jnp.exp(c_end)
        for h in range(R_HEADS):
            hs = slice(h * N, (h + 1) * N)
            ah.append(a_h[h] * e_prev[:, hs])
            rh.append(r_hat[:, hs])
            bt.append(b_h[h] * e_neg[:, hs])
            kt.append(k[:, hs] * e_neg[:, hs])
            vh.append(v[:, hs])
            bb.append(b_h[h] * e_end[:, hs])
            kb.append(k[:, hs] * e_end[:, hs])
            gm.append(gam[:, hs])
            gate.append(g[:, hs])

    heads = range(n_seq * R_HEADS)
    unit = [(i, h, slice(h * N, (h + 1) * N)) for i in range(n_seq) for h in range(R_HEADS)]
    a_ab = [jnp.where(strict, _dot_x3(ah[h], bt[h], DOT_NT), 0.0) for h in heads]
    a_ak = [jnp.where(strict, _dot_nt(ah[h], kt[h]), 0.0) for h in heads]
    a_rb = [jnp.where(incl, _dot_nt(rh[h], bt[h]), 0.0) for h in heads]
    a_rk = [jnp.where(incl, _dot_nt(rh[h], kt[h]), 0.0) for h in heads]
    inv = [eye + a_ab[h] for h in heads]
    pw = a_ab
    span = 2
    while span < L:
        pw = [_dot(pw[h], pw[h]) for h in heads]
        inv = [inv[h] + _dot(inv[h], pw[h]) for h in heads]
        span *= 2
    s = [s_scr[i, h] for i, h, _ in unit]
    rhs = [_dot_nt(ah[h], s[h]) + _dot(a_ak[h], vh[h]) for h in heads]
    u = [_dot(inv[h], rhs[h]) for h in heads]
    y = [_dot_nt(rh[h], s[h]) + _dot(a_rb[h], u[h]) + _dot(a_rk[h], vh[h]) for h in heads]
    for n, (i, h, hs) in enumerate(unit):
        s_scr[i, h] = s[n] * gm[n] + _dot_tn(u[n], bb[n]) + _dot_tn(vh[n], kb[n])
    for n, (i, h, hs) in enumerate(unit):
        mean = jnp.mean(y[n], axis=1, keepdims=True)
        yc = y[n] - mean
        var = jnp.mean(yc * yc, axis=1, keepdims=True)
        yn = yc * lax.rsqrt(var + LN_X_EPS) * lng_ref[:, hs] + lnb_ref[:, hs]
        o_ref[i, :, hs] = (yn + bonus[n]) * gate[n]

    @pl.when(t == nt - 1)
    def _():
        s_out_ref[...] = s_scr[...]


def rwkv(cols, shift0, s0, p, chunk):
    B, T, _ = cols.shape
    tp = -(-T // chunk) * chunk
    assert chunk % SUBLANES == 0 and chunk & (chunk - 1) == 0
    colsp = jnp.pad(cols, ((0, 0), (0, tp - T), (0, 0))) if tp != T else cols
    vec = lambda a: a.reshape(1, -1)
    consts = [vec(p['rwkv_mu']), vec(p['rwkv_w0']), p['rwkv_w2'].astype(BF16), vec(p['rwkv_a0']),
              p['rwkv_a2'].astype(BF16), p['rwkv_g2'].astype(BF16), vec(p['rwkv_k_k']),
              vec(p['rwkv_k_a']), vec(p['rwkv_r_k']), vec(p['rwkv_ln_g']), vec(p['rwkv_ln_b'])]
    ns = RWKV_SEQS_PER_STEP
    assert B % ns == 0
    seq_spec = lambda w: pl.BlockSpec((ns, chunk, w), lambda b, t: (b, t, 0))
    st_spec = pl.BlockSpec((ns, R_HEADS, R_HEAD_DIM, R_HEAD_DIM), lambda b, t: (b, 0, 0, 0))
    out, s_new = pl.pallas_call(
        functools.partial(_rwkv_chunk_kernel, chunk=chunk, n_valid=T),
        grid=(B // ns, tp // chunk),
        in_specs=[seq_spec(R_COLS), pl.BlockSpec((ns, 1, R_COLS), lambda b, t: (b, 0, 0))]
                 + [_resident(c.shape) for c in consts] + [st_spec],
        out_specs=[seq_spec(R_WIDTH), st_spec],
        out_shape=[jax.ShapeDtypeStruct((B, tp, R_WIDTH), F32), jax.ShapeDtypeStruct(s0.shape, F32)],
        scratch_shapes=[pltpu.VMEM((ns, R_HEADS, R_HEAD_DIM, R_HEAD_DIM), F32),
                        pltpu.VMEM((ns, 1, R_COLS), F32)],
        compiler_params=_cparams("parallel", "arbitrary"), name="rwkv_chunk",
    )(colsp, shift0, *consts, s0)
    return out[:, :T], s_new


def compress_weights(pe, w1, w2):
    d, hdim, G = A_HEAD_DIM, CMP_HIDDEN, A_KV_HEADS
    w1r = w1.reshape(2, CMP_STRIDE, d, hdim)
    same_head = jnp.eye(G, dtype=F32)
    wl = (jnp.transpose(w1r, (1, 2, 0, 3))[:, None, :, :, None, :]
          * same_head[None, :, None, None, :, None])
    wl = wl.reshape(CMP_STRIDE * G * d, 2 * G * hdim).astype(BF16)
    wpe = jnp.transpose(w1r, (1, 0, 2, 3)).reshape(CMP_STRIDE, 2 * d, hdim)
    wpe = jnp.concatenate([wpe] * G, axis=-1).astype(BF16)
    pe_l = jnp.transpose(pe.reshape(2, CMP_STRIDE, d), (1, 0, 2)).reshape(CMP_STRIDE, 1, 2 * d)
    pe_l = jnp.broadcast_to(pe_l, (CMP_STRIDE, SUBLANES, 2 * d))
    w2bd = w2[None, :, None, :] * same_head[:, None, :, None]
    return wl, wpe, pe_l, w2bd.reshape(G * hdim, G * d).astype(BF16)


def _compress_halves(strided_rows, wl_ref, n):
    x = jnp.concatenate([strided_rows(l).astype(BF16) for l in range(CMP_STRIDE)], axis=1)
    return jnp.dot(x, wl_ref[...], preferred_element_type=F32)


def _compress_finish(halves, wpe_ref, pe_ref, w2_ref):
    n = halves.shape[0]
    gh = A_KV_HEADS * CMP_HIDDEN
    pe_term = None
    for l in range(CMP_STRIDE):
        part = _dot(pe_ref[l], wpe_ref[l])
        pe_term = part if pe_term is None else pe_term + part
    second = pltpu.roll(halves[:, gh:], n - 1, 0)
    hid = _silu(halves[:, :gh] + second + pe_term[0:1, :])
    return _dot(hid, w2_ref[...])


def _compress_kernel(x_ref, wl_ref, wpe_ref, pe_ref, w2_ref, o_ref):
    n = o_ref.shape[1]
    rows = lambda l: x_ref[0, pl.ds(l, n, stride=CMP_STRIDE), :]
    o_ref[0] = _compress_finish(_compress_halves(rows, wl_ref, n), wpe_ref, pe_ref, w2_ref)


def compress_rows(rows, cw):
    B, T, W = rows.shape
    n = T // CMP_STRIDE
    return pl.pallas_call(
        _compress_kernel, grid=(B,),
        in_specs=[pl.BlockSpec((1, T, W), lambda b: (b, 0, 0))] + [_resident(c.shape) for c in cw],
        out_specs=pl.BlockSpec((1, n, W), lambda b: (b, 0, 0)),
        out_shape=jax.ShapeDtypeStruct((B, n, W), F32),
        compiler_params=_cparams("parallel"), name="nsa_compress",
    )(rows, *cw)


def _topk_mask_columns(rank, k):
    n_entries = rank.shape[0]
    row = lax.broadcasted_iota(jnp.int32, rank.shape, 0).astype(F32)
    sel = jnp.zeros(rank.shape, F32)
    for _ in range(k):
        m = jnp.max(rank, axis=0, keepdims=True)
        first = jnp.min(jnp.where(rank == m, row, float(n_entries)), axis=0, keepdims=True)
        hit = row == first
        sel = jnp.where(hit, 1.0, sel)
        rank = jnp.where(hit, LOWEST, rank)
    return sel


def _flash_update(state, s, bias, v):
    m_old, l_old, acc = state
    reps = s.shape[0] // bias.shape[0]
    n = bias.shape[0]
    s = jnp.concatenate([s[j * n:(j + 1) * n] + bias for j in range(reps)], axis=0)
    m_new = jnp.maximum(m_old, jnp.max(s, axis=1, keepdims=True))
    p = jnp.exp(s - m_new)
    alpha = jnp.exp(m_old - m_new)
    return (m_new, alpha * l_old + jnp.sum(p, axis=1, keepdims=True), alpha * acc + _dot(p, v))


def _nsa_prompt_kernel(q_ref, gate_ref, kc_ref, vc_ref, ks_ref, vs_ref, kw_ref, vw_ref, covt_ref,
                       o_ref, *, seq, tk, wk):
    i = pl.program_id(1)
    QB, d, J = Q_BLOCK, A_HEAD_DIM, A_GROUP
    nc = kc_ref.shape[1]
    nb = covt_ref.shape[0]
    st = i * QB
    q = q_ref[0] * A_SCALE
    gate = _sigmoid(gate_ref[0])
    rows = J * QB
    t_q = st + lax.broadcasted_iota(jnp.int32, (QB, 1), 0)
    t_rows = jnp.concatenate([t_q] * J, axis=0)
    blk_end = lax.broadcasted_iota(jnp.int32, (rows, nc), 1) * CMP_STRIDE + (CMP_BLOCK - 1)
    cmp_mask = blk_end <= t_rows
    jj = lax.broadcasted_iota(jnp.int32, (nb, QB), 0)
    cur = (st + lax.broadcasted_iota(jnp.int32, (1, QB), 1)) // SEL_BLOCK
    forced = (jj == 0) | (jj == cur) | (jj == cur - 1)
    zeros = lambda w: jnp.zeros((rows, w), F32)

    qgs, o_cmps, ranks = [], [], []
    for g in range(A_KV_HEADS):
        gl = slice(g * d, (g + 1) * d)
        qg = jnp.concatenate([q[:, (g * J + j) * d:(g * J + j + 1) * d] for j in range(J)],
                             axis=0).astype(BF16)
        s = jnp.where(cmp_mask, _dot_nt(qg, kc_ref[0, :, gl]), NEG)
        e = jnp.exp(s - jnp.max(s, axis=1, keepdims=True))
        p = jnp.where(cmp_mask, e / jnp.sum(e, axis=1, keepdims=True), 0.0)
        p_grp = p[0:QB]
        for j in range(1, J):
            p_grp = p_grp + p[j * QB:(j + 1) * QB]
        score = _dot_f32_rhs_nt(covt_ref[...], p_grp)
        qgs.append(qg)
        o_cmps.append(_dot(p, vc_ref[0, :, gl]))
        ranks.append(jnp.where(jj <= cur, jnp.where(forced, FORCE, score), NEG))
    sel_all = _topk_mask_columns(jnp.concatenate(ranks, axis=1), min(N_SEL, nb)).astype(BF16)

    k0w = pl.multiple_of(jnp.clip(st + QB - wk, 0, seq - wk), QB)
    rel = t_q - (k0w + lax.broadcasted_iota(jnp.int32, (1, wk), 1))
    win_bias = jnp.where((rel >= 0) & (rel < WINDOW), 0.0, NEG)

    def sel_tile(kt, states):
        k0 = pl.multiple_of(kt * tk, tk)
        s_pos = k0 + lax.broadcasted_iota(jnp.int32, (1, tk), 1)
        blk_of = k0 // SEL_BLOCK + lax.broadcasted_iota(jnp.int32, (nb, tk), 1) // SEL_BLOCK
        expand = (lax.broadcasted_iota(jnp.int32, (nb, tk), 0) == blk_of).astype(BF16)
        causal = s_pos <= t_q
        new_states = []
        for g in range(A_KV_HEADS):
            gl = slice(g * d, (g + 1) * d)
            chosen = lax.dot_general(sel_all[:, g * QB:(g + 1) * QB], expand, DOT_TN,
                                     preferred_element_type=F32) > 0.5
            bias = jnp.where(chosen & causal, 0.0, NEG)
            sc = _dot_nt(qgs[g], ks_ref[0, pl.ds(k0, tk), gl])
            new_states.append(_flash_update(states[g], sc, bias, vs_ref[0, pl.ds(k0, tk), gl]))
        return tuple(new_states)

    init = (jnp.full((rows, 1), NEG, F32), jnp.zeros((rows, 1), F32), zeros(d))
    sel_states = lax.fori_loop(0, (st + QB - 1) // tk + 1, sel_tile, (init,) * A_KV_HEADS)

    for g in range(A_KV_HEADS):
        gl = slice(g * d, (g + 1) * d)
        qg, o_cmp = qgs[g], o_cmps[g]
        _, l_sel, acc_sel = sel_states[g]
        o_sel = acc_sel / l_sel

        sw = _dot_nt(qg, kw_ref[0, pl.ds(k0w, wk), gl])
        sw = jnp.concatenate([sw[j * QB:(j + 1) * QB] + win_bias for j in range(J)], axis=0)
        pw = jnp.exp(sw - jnp.max(sw, axis=1, keepdims=True))
        o_win = _dot(pw, vw_ref[0, pl.ds(k0w, wk), gl]) / jnp.sum(pw, axis=1, keepdims=True)

        for j in range(J):
            h = g * J + j
            rs = slice(j * QB, (j + 1) * QB)
            o_ref[0, :, h * d:(h + 1) * d] = (gate[:, h:h + 1] * o_cmp[rs]
                                              + gate[:, A_HEADS + h:A_HEADS + h + 1] * o_sel[rs]
                                              + gate[:, 2 * A_HEADS + h:2 * A_HEADS + h + 1] * o_win[rs])


def _covers(n_cmp, n_blocks):
    i = np.arange(n_cmp)[:, None]
    j = np.arange(n_blocks)[None, :]
    return (i * CMP_STRIDE < (j + 1) * SEL_BLOCK) & (i * CMP_STRIDE + CMP_BLOCK > j * SEL_BLOCK)


def nsa_prompt_attend(a_q, a_gate, kc, vc, ks, vs, kw, vw):
    B, T, _ = a_q.shape
    nc = kc.shape[1]
    nb = -(-T // SEL_BLOCK)
    tk = min(8 * Q_BLOCK, T)
    wk = min(WINDOW + Q_BLOCK, T)
    cov = jnp.asarray(_covers(nc, nb).T, BF16)
    tile = lambda w: pl.BlockSpec((1, Q_BLOCK, w), lambda b, i: (b, i, 0))
    full = lambda n, w: pl.BlockSpec((1, n, w), lambda b, i: (b, 0, 0))
    W = A_KV_LANES
    return pl.pallas_call(
        functools.partial(_nsa_prompt_kernel, seq=T, tk=tk, wk=wk),
        grid=(B, T // Q_BLOCK),
        in_specs=[tile(A_WIDTH), tile(3 * A_HEADS), full(nc, W), full(nc, W),
                  full(T, W), full(T, W), full(T, W), full(T, W), _resident(cov.shape)],
        out_specs=tile(A_WIDTH),
        out_shape=jax.ShapeDtypeStruct((B, T, A_WIDTH), F32),
        compiler_params=_cparams("parallel", "arbitrary"), name="nsa_prompt",
    )(a_q, a_gate, kc, vc, ks, vs, kw, vw, cov)


GATHER_PAGES = 16
CHUNKS_PER_PAGE = PAGE_SIZE // CMP_STRIDE
HEAD_ROWS = SUBLANES
PICK_ROWS = 2 * SUBLANES


def channel_major(cache):
    L, n, tok, G, d = cache.shape
    return jnp.transpose(cache, (0, 1, 3, 4, 2)).reshape(L, n, G * d, tok)


def _nsa_sample_cmp_kernel(pt_ref, *refs, past_len):
    del pt_ref
    ng = GATHER_PAGES
    pk_refs, pv_refs = refs[:ng], refs[ng:2 * ng]
    (q_ref, wink_ref, winv_ref, kwn_ref, vwn_ref, wlk_ref, wpek_ref, pek_ref, w2k_ref,
     wlv_ref, wpev_ref, pev_ref, w2v_ref, cov_ref, ocmp_ref, owin_ref, idx_ref,
     hk_scr, hv_scr, xk_scr, xv_scr) = refs[2 * ng:]
    c = pl.program_id(1)
    page = pk_refs[0].shape[3]
    rows_per_step = ng * CHUNKS_PER_PAGE
    base = pl.multiple_of(c * rows_per_step, rows_per_step)
    for page_refs, xs, wl_ref, h_scr in ((pk_refs, xk_scr, wlk_ref, hk_scr), (pv_refs, xv_scr, wlv_ref, hv_scr)):
        for i in range(ng):
            xs[i * page:(i + 1) * page, :] = page_refs[i][0, 0].T
        rows = lambda l, xs=xs: xs[pl.ds(l, rows_per_step, stride=CMP_STRIDE), :]
        h_scr[pl.ds(base, rows_per_step), :] = _compress_halves(rows, wl_ref, rows_per_step)

    @pl.when(c == pl.num_programs(1) - 1)
    def _():
        d, R8 = A_HEAD_DIM, HEAD_ROWS
        kc = _compress_finish(hk_scr[...], wpek_ref, pek_ref, w2k_ref)
        vc = _compress_finish(hv_scr[...], wpev_ref, pev_ref, w2v_ref)
        nc = kc.shape[0]
        nb = cov_ref.shape[0]
        keep = wink_ref.shape[3]
        q_pos = past_len
        blk_end = lax.broadcasted_iota(jnp.int32, (R8, nc), 1) * CMP_STRIDE + (CMP_BLOCK - 1)
        cmp_mask = (blk_end <= q_pos) & (lax.broadcasted_iota(jnp.int32, (R8, nc), 1) < nc - 1)
        real_head = lax.broadcasted_iota(jnp.int32, (R8, nc), 0) < A_GROUP
        col8 = lax.broadcasted_iota(jnp.int32, (nb, R8), 1)
        jj = lax.broadcasted_iota(jnp.int32, (nb, R8), 0)
        cur = q_pos // SEL_BLOCK
        forced = (jj == 0) | (jj == cur) | (jj == cur - 1)
        k_idx = lax.broadcasted_iota(jnp.int32, (R8, keep), 1)
        rel = q_pos - (past_len - keep + k_idx)
        win_mask = (rel >= 0) & (rel < WINDOW)
        rank = None
        for g in range(A_KV_HEADS):
            gl = slice(g * d, (g + 1) * d)
            qg = q_ref[0, g] * A_SCALE
            s = jnp.where(cmp_mask, _dot_nt(qg, kc[:, gl]), NEG)
            e = jnp.exp(s - jnp.max(s, axis=1, keepdims=True))
            p = jnp.where(cmp_mask, e / jnp.sum(e, axis=1, keepdims=True), 0.0)
            ocmp_ref[0, g] = _dot(p, vc[:, gl])
            p_grp = jnp.sum(jnp.where(real_head, p, 0.0), axis=0, keepdims=True)
            score = _dot_f32_rhs_nt(cov_ref[...], jnp.broadcast_to(p_grp, (R8, nc)))
            rank_g = jnp.where(jj <= cur, jnp.where(forced, FORCE, score), NEG)
            rank = rank_g if rank is None else jnp.where(col8 == g, rank_g, rank)
            sw = jnp.where(win_mask, _dot(qg, wink_ref[0, 0, gl, :]), NEG)
            s_new = jnp.sum(qg * kwn_ref[0, :, gl], axis=1, keepdims=True)
            m = jnp.maximum(jnp.max(sw, axis=1, keepdims=True), s_new)
            ew = jnp.where(win_mask, jnp.exp(sw - m), 0.0)
            e_new = jnp.exp(s_new - m)
            owin_ref[0, g] = ((_dot_nt(ew, winv_ref[0, 0, gl, :]) + e_new * vwn_ref[0, :, gl])
                              / (jnp.sum(ew, axis=1, keepdims=True) + e_new))
        block = jj.astype(F32)
        out_row = lax.broadcasted_iota(jnp.int32, idx_ref.shape[1:], 0)
        picks = jnp.zeros(idx_ref.shape[1:], jnp.int32)
        for it in range(N_SEL - 1):
            m = jnp.max(rank, axis=0, keepdims=True)
            first = jnp.min(jnp.where(rank == m, block, float(nb)), axis=0, keepdims=True)
            picks = jnp.where(out_row == it, first.astype(jnp.int32), picks)
            rank = jnp.where(block == first, LOWEST, rank)
        idx_ref[0] = picks


def nsa_sample_cmp(page_table, pool_k, pool_v, q_hm, win_k, win_v, layer, kw_new, vw_new, cwk, cwv, past_len):
    B, n_pages = page_table.shape
    _, n_pool, W, page = pool_k.shape
    ng = GATHER_PAGES
    assert n_pages % ng == 0
    nc = n_pages * CHUNKS_PER_PAGE
    nb = past_len // SEL_BLOCK
    cov = jnp.asarray(_covers(nc, nb).T, BF16)
    keep = win_k.shape[3]
    page_spec = lambda i: pl.BlockSpec((1, 1, W, page), lambda b, c, pt: (layer, pt[b, c * ng + i], 0, 0))
    perb = lambda *s: pl.BlockSpec((1,) + s, lambda b, c, pt: (b,) + (0,) * len(s))
    win_spec = pl.BlockSpec((1, 1, W, keep), lambda b, c, pt: (layer, b, 0, 0))
    const = lambda a: pl.BlockSpec(a.shape, lambda b, c, pt: (0,) * a.ndim)
    consts = [*cwk, *cwv, cov]
    hw = cwk[0].shape[1]
    hm = jax.ShapeDtypeStruct((B, A_KV_HEADS, HEAD_ROWS, A_HEAD_DIM), F32)
    grid_spec = pltpu.PrefetchScalarGridSpec(
        num_scalar_prefetch=1, grid=(B, n_pages // ng),
        in_specs=[page_spec(i) for i in range(ng)] * 2
                 + [perb(A_KV_HEADS, HEAD_ROWS, A_HEAD_DIM), win_spec, win_spec, perb(1, W), perb(1, W)]
                 + [const(a) for a in consts],
        out_specs=[perb(A_KV_HEADS, HEAD_ROWS, A_HEAD_DIM), perb(A_KV_HEADS, HEAD_ROWS, A_HEAD_DIM),
                   perb(PICK_ROWS, HEAD_ROWS)],
        scratch_shapes=[pltpu.VMEM((nc, hw), F32), pltpu.VMEM((nc, hw), F32),
                        pltpu.VMEM((ng * page, W), F32), pltpu.VMEM((ng * page, W), F32)])
    return pl.pallas_call(
        functools.partial(_nsa_sample_cmp_kernel, past_len=past_len), grid_spec=grid_spec,
        out_shape=[hm, hm, jax.ShapeDtypeStruct((B, PICK_ROWS, HEAD_ROWS), jnp.int32)],
        compiler_params=_cparams("parallel", "arbitrary"), name="nsa_sample_cmp",
    )(page_table, *([pool_k] * ng), *([pool_v] * ng), q_hm, win_k, win_v, kw_new, vw_new, *consts)


SEL_PICKS_PER_STEP = 5


def _nsa_sample_sel_kernel(pt_ref, pk_ref, *refs, npk):
    del pt_ref
    P = SEL_PICKS_PER_STEP
    n_pages = A_KV_HEADS * P
    k_refs, v_refs = refs[:n_pages], refs[n_pages:2 * n_pages]
    q_ref, gate_ref, ksn_ref, vsn_ref, ocmp_ref, owin_ref, o_ref, m_scr, l_scr, acc_scr = refs[2 * n_pages:]
    b = pl.program_id(0)
    c = pl.program_id(1)
    d = A_HEAD_DIM
    page = k_refs[0].shape[3]
    tok_blk = lax.broadcasted_iota(jnp.int32, (HEAD_ROWS, page), 1) // SEL_BLOCK
    for g in range(A_KV_HEADS):
        gl = slice(g * d, (g + 1) * d)
        qg = q_ref[0, g] * A_SCALE

        @pl.when(c == 0)
        def _():
            m_scr[g] = jnp.sum(qg * ksn_ref[0, :, gl], axis=1, keepdims=True)
            l_scr[g] = jnp.ones((HEAD_ROWS, 1), F32)
            acc_scr[g] = jnp.broadcast_to(vsn_ref[0, :, gl], (HEAD_ROWS, d))

        chosen = [tok_blk == pk_ref[b, g * npk + c * P + i] % (page // SEL_BLOCK) for i in range(P)]
        s = [jnp.where(chosen[i], _dot(qg, k_refs[g * P + i][0, 0, gl, :]), NEG) for i in range(P)]
        m_old = m_scr[g]
        m = m_old
        for i in range(P):
            m = jnp.maximum(m, jnp.max(s[i], axis=1, keepdims=True))
        alpha = jnp.exp(m_old - m)
        l = alpha * l_scr[g]
        acc = alpha * acc_scr[g]
        for i in range(P):
            p = jnp.where(chosen[i], jnp.exp(s[i] - m), 0.0)
            l = l + jnp.sum(p, axis=1, keepdims=True)
            acc = acc + _dot_nt(p, v_refs[g * P + i][0, 0, gl, :])
        m_scr[g] = m
        l_scr[g] = l
        acc_scr[g] = acc

        @pl.when(c == pl.num_programs(1) - 1)
        def _():
            gate = _sigmoid(gate_ref[0, g])
            o_ref[0, g] = (gate[:, 0:1] * ocmp_ref[0, g] + gate[:, 1:2] * (acc / l)
                           + gate[:, 2:3] * owin_ref[0, g])


def nsa_sample_sel(page_table, picks, pool_k, pool_v, layer, q_hm, gate_hm, ks_new, vs_new, o_cmp, o_win):
    B = page_table.shape[0]
    _, n_pool, W, page = pool_k.shape
    per_page = page // SEL_BLOCK
    npk = picks.shape[2]
    P = SEL_PICKS_PER_STEP
    assert npk % P == 0
    picks = picks.reshape(B, A_KV_HEADS * npk)

    def blk(g, i):
        def index(b, c, pt, pk):
            return (layer, pt[b, pk[b, g * npk + c * P + i] // per_page], 0, 0)
        return pl.BlockSpec((1, 1, W, page), index)

    pages = [blk(g, i) for g in range(A_KV_HEADS) for i in range(P)]
    perb = lambda *s: pl.BlockSpec((1,) + s, lambda b, c, pt, pk: (b,) + (0,) * len(s))
    hm_spec = perb(A_KV_HEADS, HEAD_ROWS, A_HEAD_DIM)
    grid_spec = pltpu.PrefetchScalarGridSpec(
        num_scalar_prefetch=2, grid=(B, npk // P),
        in_specs=pages * 2 + [hm_spec, perb(A_KV_HEADS, HEAD_ROWS, 3), perb(1, W), perb(1, W), hm_spec, hm_spec],
        out_specs=hm_spec,
        scratch_shapes=[pltpu.VMEM((A_KV_HEADS, HEAD_ROWS, 1), F32), pltpu.VMEM((A_KV_HEADS, HEAD_ROWS, 1), F32),
                        pltpu.VMEM((A_KV_HEADS, HEAD_ROWS, A_HEAD_DIM), F32)])
    return pl.pallas_call(
        functools.partial(_nsa_sample_sel_kernel, npk=npk), grid_spec=grid_spec,
        out_shape=jax.ShapeDtypeStruct((B, A_KV_HEADS, HEAD_ROWS, A_HEAD_DIM), F32),
        compiler_params=_cparams("parallel", "arbitrary"), name="nsa_sample_sel",
    )(page_table, picks, *([pool_k] * len(pages)), *([pool_v] * len(pages)), q_hm, gate_hm, ks_new, vs_new,
      o_cmp, o_win)


def nsa_sample(a_q, a_gate, new_rows, pools, win_k, win_v, layer, page_table, cwk, cwv):
    B = a_q.shape[0]
    past_len = page_table.shape[1] * PAGE_SIZE
    _, _, ksn, vsn, kwn, vwn = new_rows
    pool_kc, pool_vc, pool_ks, pool_vs = pools
    padj = ((0, 0), (0, 0), (0, HEAD_ROWS - A_GROUP), (0, 0))
    q_hm = jnp.pad(a_q.reshape(B, A_KV_HEADS, A_GROUP, A_HEAD_DIM), padj)
    gate_hm = jnp.pad(jnp.transpose(a_gate.reshape(B, 3, A_KV_HEADS, A_GROUP), (0, 2, 3, 1)), padj)
    o_cmp, o_win, picks = nsa_sample_cmp(page_table, pool_kc, pool_vc, q_hm, win_k, win_v, layer, kwn, vwn,
                                         cwk, cwv, past_len)
    picks = jnp.swapaxes(picks, 1, 2)[:, :A_KV_HEADS, :N_SEL - 1]
    o = nsa_sample_sel(page_table, picks, pool_ks, pool_vs, layer, q_hm, gate_hm, ksn, vsn, o_cmp, o_win)
    return o[:, :, :A_GROUP].reshape(B, 1, A_WIDTH)


PROJ_GROUPS = (2 * M_WIDTH, M_WIDTH, M_WIDTH, 2 * M_HEADS, A_WIDTH) + (A_KV_LANES,) * 6 + (
    3 * A_HEADS, R_COLS, 3 * D_MODEL)
PROMPT_ROW_TILE = 256
MLSTM_CHUNK = 256
RWKV_CHUNK = 64


def _layer_params(l, w_in, w_branch_m, w_branch_a, w_branch_r, w_out, w_gate_up, w_down,
                  nsa_cmp_pe, nsa_cmp_w1, nsa_cmp_w2):
    offs = np.cumsum((0,) + PROJ_GROUPS)
    assert offs[-1] == w_in.shape[2]
    wi = w_in[l].astype(BF16)
    return {
        'proj': [wi[:, a:b] for a, b in zip(offs[:-1], offs[1:])],
        'wm': w_branch_m[l].astype(BF16), 'wa': w_branch_a[l].astype(BF16), 'wr': w_branch_r[l].astype(BF16),
        'wo': w_out[l].astype(BF16), 'wgu': w_gate_up[l].astype(BF16), 'wd': w_down[l].astype(BF16),
        'cwk': compress_weights(nsa_cmp_pe[l, 0], nsa_cmp_w1[l, 0], nsa_cmp_w2[l, 0]),
        'cwv': compress_weights(nsa_cmp_pe[l, 1], nsa_cmp_w1[l, 1], nsa_cmp_w2[l, 1]),
    }


def _trunk_layer(x, lw, lp, nsa_fn, m_state, r_state, tm, chunk, rwkv_chunk, gf, final_norm):
    B, T, D = x.shape
    x2d = x.reshape(B * T, D)
    outs = norm_proj(x2d, lp['norm1_g'], lw['proj'], tm)
    (m_qk, m_v, m_o, m_if, a_q), new_rows, (a_gate, r_cols, merge) = outs[:5], outs[5:11], outs[11:]
    b3 = lambda a: a.reshape(B, T, a.shape[-1])
    new_rows = [b3(r) for r in new_rows]
    y_m, conv_n, c_n, n_n, m_n = mlstm(b3(m_qk), b3(m_v), b3(m_o), b3(m_if), *m_state,
                                        lp['mlstm_conv_w'], lp['mlstm_conv_b'], lp['mlstm_if_bias'], chunk)
    y_a = nsa_fn(b3(a_q), b3(a_gate), new_rows)
    r_cols = b3(r_cols)
    y_r, s_n = rwkv(r_cols, r_state[0], r_state[1], lp, rwkv_chunk)
    x_new = merge_ffn(x2d, y_m.reshape(B * T, -1), y_a.reshape(B * T, -1), y_r.reshape(B * T, -1), merge,
                      lw['wm'], lw['wa'], lw['wr'], lw['wo'], lp['norm2_g'], lw['wgu'], lw['wd'], gf,
                      tm, final_norm)
    return x_new.reshape(B, T, D), new_rows, (conv_n, c_n, n_n, m_n), (r_cols[:, -1:], s_n)


def kernel(x_prompt, x_sample, cache_cmp_k, cache_cmp_v, cache_slc_k, cache_slc_v, cache_win_k, cache_win_v, state_mlstm_conv, state_mlstm_C, state_mlstm_n, state_mlstm_m, state_rwkv_shift, state_rwkv, page_table, norm1_g, w_in, mlstm_if_bias, mlstm_conv_w, mlstm_conv_b, nsa_cmp_pe, nsa_cmp_w1, nsa_cmp_w2, rwkv_mu, rwkv_w0, rwkv_w2, rwkv_a0, rwkv_a2, rwkv_g2, rwkv_k_k, rwkv_k_a, rwkv_r_k, rwkv_ln_g, rwkv_ln_b, w_branch_m, w_branch_a, w_branch_r, w_out, norm2_g, w_gate_up, w_down, final_norm_g):
    bp, tp, _ = x_prompt.shape
    bs, ts, _ = x_sample.shape
    assert ts == 1
    depth = w_in.shape[0]
    kvshape = lambda a: a.reshape(a.shape[0], a.shape[1], A_KV_HEADS, A_HEAD_DIM)
    xp, xs = x_prompt, x_sample
    pools = [channel_major(c) for c in (cache_cmp_k, cache_cmp_v, cache_slc_k, cache_slc_v)]
    win_k, win_v = channel_major(cache_win_k), channel_major(cache_win_v)
    states_p, states_s = [], []
    for l in range(depth):
        lw = _layer_params(l, w_in, w_branch_m, w_branch_a, w_branch_r, w_out, w_gate_up, w_down,
                           nsa_cmp_pe, nsa_cmp_w1, nsa_cmp_w2)
        lp = {'norm1_g': norm1_g[l], 'mlstm_if_bias': mlstm_if_bias[l], 'mlstm_conv_w': mlstm_conv_w[l],
              'mlstm_conv_b': mlstm_conv_b[l], 'rwkv_mu': rwkv_mu[l], 'rwkv_w0': rwkv_w0[l],
              'rwkv_w2': rwkv_w2[l], 'rwkv_a0': rwkv_a0[l], 'rwkv_a2': rwkv_a2[l], 'rwkv_g2': rwkv_g2[l],
              'rwkv_k_k': rwkv_k_k[l], 'rwkv_k_a': rwkv_k_a[l], 'rwkv_r_k': rwkv_r_k[l],
              'rwkv_ln_g': rwkv_ln_g[l], 'rwkv_ln_b': rwkv_ln_b[l], 'norm2_g': norm2_g[l]}
        last = l == depth - 1

        def nsa_p(a_q, a_gate, rows):
            kc = compress_rows(rows[0], lw['cwk'])
            vc = compress_rows(rows[1], lw['cwv'])
            return nsa_prompt_attend(a_q, a_gate, kc, vc, *rows[2:])

        m_zero = (jnp.zeros((bp, M_CONV - 1, 2 * M_WIDTH), F32),
                  jnp.zeros((bp, M_HEADS, M_HEAD_DIM, M_HEAD_DIM), F32),
                  jnp.zeros((bp, M_HEADS, M_HEAD_DIM), F32), jnp.zeros((bp, M_HEADS), F32))
        r_zero = (jnp.zeros((bp, 1, R_COLS), F32), jnp.zeros((bp, R_HEADS, R_HEAD_DIM, R_HEAD_DIM), F32))
        xp, rows, m_new, r_new = _trunk_layer(xp, lw, lp, nsa_p, m_zero, r_zero, PROMPT_ROW_TILE,
                                              MLSTM_CHUNK, RWKV_CHUNK, final_norm_g, last)
        keep = min(WINDOW, tp)
        states_p.append(tuple(kvshape(r) for r in rows[:4])
                        + (kvshape(rows[4][:, tp - keep:]), kvshape(rows[5][:, tp - keep:])) + m_new + r_new)

        def nsa_s(a_q, a_gate, rows, l=l):
            return nsa_sample(a_q, a_gate, rows, pools, win_k, win_v, l, page_table, lw['cwk'], lw['cwv'])

        xs, rows, m_new, r_new = _trunk_layer(
            xs, lw, lp, nsa_s,
            (state_mlstm_conv[l], state_mlstm_C[l], state_mlstm_n[l], state_mlstm_m[l]),
            (state_rwkv_shift[l], state_rwkv[l]), bs * ts, SUBLANES, SUBLANES, final_norm_g, last)
        keep = win_k.shape[3]

        def win_new(win, new_row):
            w = jnp.concatenate([win[l], jnp.swapaxes(new_row, 1, 2)], axis=2)[:, :, -keep:]
            return jnp.transpose(w.reshape(bs, A_KV_HEADS, A_HEAD_DIM, keep), (0, 3, 1, 2))

        states_s.append(tuple(kvshape(r) for r in rows[:4])
                        + (win_new(win_k, rows[4]), win_new(win_v, rows[5])) + m_new + r_new)

    stack = lambda states: [jnp.stack(a) for a in zip(*states)]
    return (xp, xs, *stack(states_p), *stack(states_s))
```
